```python
import jax, jax.numpy as jnp
from jax import lax
import numpy as np

D_MODEL = 1024
BATCH = 8
SEQ = 4096
DEPTH = 4

CHUNK = 64
Q_BLOCK = 128
EPS = 1e-6
NEG_INF = -1e30

MLA_HEADS = 8
MLA_NOPE = 64
MLA_ROPE = 32
MLA_QK = MLA_NOPE + MLA_ROPE
MLA_V = 64
Q_LORA = 256
KV_LORA = 128
ROPE_BASE = 10000.0
MLA_WIDTH = MLA_HEADS * MLA_V

CA_HEADS = 8
CA_HEAD_DIM = 64
CA_WIDTH = CA_HEADS * CA_HEAD_DIM
LEFT_CHUNKS = 8
BAND_CHUNKS = LEFT_CHUNKS + 1
BAND = BAND_CHUNKS * CHUNK
REL_CLIP = 128
N_REL = 2 * REL_CLIP + 1

D_MIX = MLA_WIDTH + CA_WIDTH
OFF_CQ = 0
OFF_CKV = OFF_CQ + Q_LORA
OFF_KR = OFF_CKV + KV_LORA
OFF_CA = OFF_KR + MLA_ROPE
IN_COLS = OFF_CA + 3 * CA_WIDTH

D_FF = 2816
CONV_W = 3

kernel_name = "hybrid_mla_chunkrel_convglu"


def rmsnorm(x, g):
    xf = x.astype(jnp.float32)
    y = xf * lax.rsqrt(jnp.mean(xf * xf, axis=-1, keepdims=True) + EPS)
    return (y * g.astype(jnp.float32)).astype(x.dtype)


def rope_tables(positions):
    inv = ROPE_BASE ** (-jnp.arange(0, MLA_ROPE, 2, dtype=jnp.float32) / MLA_ROPE)
    ang = positions.astype(jnp.float32)[..., None] * inv
    return jnp.cos(ang), jnp.sin(ang)


def apply_rope(t, cos, sin):
    t1, t2 = jnp.split(t.astype(jnp.float32), 2, axis=-1)
    c = cos[:, :, None, :]
    s = sin[:, :, None, :]
    return jnp.concatenate([t1 * c - t2 * s, t1 * s + t2 * c], axis=-1).astype(t.dtype)


def rope_tail(t, cos, sin):
    return jnp.concatenate([t[..., :MLA_NOPE], apply_rope(t[..., MLA_NOPE:], cos, sin)], axis=-1)


def mla_mixer(c_q, c_kv, k_rope, cos, sin, w_uq, w_ukv, g_qa, g_kva, g_qn, g_kn):
    B, S, _ = c_q.shape
    q = (rmsnorm(c_q, g_qa) @ w_uq).reshape(B, S, MLA_HEADS, MLA_QK)
    kv = (rmsnorm(c_kv, g_kva) @ w_ukv).reshape(B, S, MLA_HEADS, MLA_NOPE + MLA_V)
    k_nope, v = kv[..., :MLA_NOPE], kv[..., MLA_NOPE:]
    k_r = jnp.broadcast_to(k_rope[:, :, None, :], (B, S, MLA_HEADS, MLA_ROPE))
    k = jnp.concatenate([k_nope, k_r], axis=-1)
    q = rope_tail(rmsnorm(q, g_qn), cos, sin)
    k = rope_tail(rmsnorm(k, g_kn), cos, sin)
    scale = MLA_QK ** -0.5
    n_qb = S // Q_BLOCK
    q_blocks = q.reshape(B, n_qb, Q_BLOCK, MLA_HEADS, MLA_QK).transpose(1, 0, 2, 3, 4)
    k_chunk = jnp.arange(S) // CHUNK

    def one_block(args):
        qb, bidx = args
        q_chunk = (bidx * Q_BLOCK + jnp.arange(Q_BLOCK)) // CHUNK
        s = jnp.einsum('bqhd,bkhd->bhqk', qb, k).astype(jnp.float32) * scale
        mask = k_chunk[None, :] <= q_chunk[:, None]
        s = jnp.where(mask[None, None], s, NEG_INF)
        p = jax.nn.softmax(s, axis=-1).astype(v.dtype)
        return jnp.einsum('bhqk,bkhd->bqhd', p, v)

    out = lax.map(one_block, (q_blocks, jnp.arange(n_qb)))
    return out.transpose(1, 0, 2, 3, 4).reshape(B, S, MLA_WIDTH)


def chunk_rel_mixer(qkv, rel_bias, g_qn, g_kn):
    B, S, _ = qkv.shape
    nc = S // CHUNK
    qkv = qkv.reshape(B, S, 3, CA_HEADS, CA_HEAD_DIM)
    q = rmsnorm(qkv[:, :, 0], g_qn)
    k = rmsnorm(qkv[:, :, 1], g_kn)
    v = qkv[:, :, 2]
    qc = q.reshape(B, nc, CHUNK, CA_HEADS, CA_HEAD_DIM)

    def band(t):
        tc = t.reshape(B, nc, CHUNK, CA_HEADS, CA_HEAD_DIM)
        tp = jnp.pad(tc, ((0, 0), (LEFT_CHUNKS, 0), (0, 0), (0, 0), (0, 0)))
        return jnp.concatenate([tp[:, j:j + nc] for j in range(BAND_CHUNKS)], axis=2)

    kb, vb = band(k), band(v)
    scale = CA_HEAD_DIM ** -0.5
    s = jnp.einsum('bcqhd,bckhd->bhcqk', qc, kb).astype(jnp.float32) * scale
    qi = jnp.arange(CHUNK)
    kj = jnp.arange(BAND)
    dist = qi[:, None] + LEFT_CHUNKS * CHUNK - kj[None, :]
    idx = jnp.clip(dist, -REL_CLIP, REL_CLIP) + REL_CLIP
    bias = rel_bias[:, idx].astype(jnp.float32)
    valid = (jnp.arange(nc)[:, None] - LEFT_CHUNKS + kj[None, :] // CHUNK) >= 0
    s = jnp.where(valid[None, None, :, None, :], s + bias[None, :, None], NEG_INF)
    p = jax.nn.softmax(s, axis=-1).astype(vb.dtype)
    o = jnp.einsum('bhcqk,bckhd->bcqhd', p, vb)
    return o.reshape(B, S, CA_WIDTH)


def conv_glu_ffn(h, w_up, conv_w, conv_b, w_down):
    S = h.shape[1]
    u = h @ w_up
    up = jnp.pad(u, ((0, 0), (CONV_W - 1, 0), (0, 0)))
    u = sum(up[:, i:i + S] * conv_w[i] for i in range(CONV_W)) + conv_b
    g, val = u[..., :D_FF], u[..., D_FF:]
    return (jax.nn.silu(g) * val) @ w_down


def setup_inputs(seed: int = 0) -> dict:
    key = jax.random.key(seed)
    ks = jax.random.split(key, 24)
    f32 = jnp.float32

    def nrm(k, shape, scale):
        return jax.random.normal(k, shape, f32) * scale

    def gain(k, shape):
        return 1.0 + 0.02 * jax.random.normal(k, shape, f32)

    x = jax.random.normal(ks[0], (BATCH, SEQ, D_MODEL), f32)
    positions = jnp.broadcast_to(jnp.arange(SEQ, dtype=jnp.int32)[None, :], (BATCH, SEQ))
    return {
        "x": x,
        "positions": positions,
        "g_mix": gain(ks[1], (DEPTH, D_MODEL)),
        "w_in": nrm(ks[2], (DEPTH, D_MODEL, IN_COLS), D_MODEL ** -0.5),
        "w_uq": nrm(ks[3], (DEPTH, Q_LORA, MLA_HEADS * MLA_QK), Q_LORA ** -0.5),
        "w_ukv": nrm(ks[4], (DEPTH, KV_LORA, MLA_HEADS * (MLA_NOPE + MLA_V)), KV_LORA ** -0.5),
        "g_q_lora": gain(ks[5], (DEPTH, Q_LORA)),
        "g_kv_lora": gain(ks[6], (DEPTH, KV_LORA)),
        "g_mla_q": gain(ks[7], (DEPTH, MLA_QK)),
        "g_mla_k": gain(ks[8], (DEPTH, MLA_QK)),
        "g_ca_q": gain(ks[9], (DEPTH, CA_HEAD_DIM)),
        "g_ca_k": gain(ks[10], (DEPTH, CA_HEAD_DIM)),
        "rel_bias": nrm(ks[11], (DEPTH, CA_HEADS, N_REL), 0.2),
        "g_out_mla": gain(ks[12], (DEPTH, MLA_WIDTH)),
        "g_out_ca": gain(ks[13], (DEPTH, CA_WIDTH)),
        "w_out": nrm(ks[14], (DEPTH, D_MIX, D_MODEL), D_MIX ** -0.5),
        "g_ffn": gain(ks[15], (DEPTH, D_MODEL)),
        "w_up": nrm(ks[16], (DEPTH, D_MODEL, 2 * D_FF), D_MODEL ** -0.5),
        "conv_w": nrm(ks[17], (DEPTH, CONV_W, 2 * D_FF), CONV_W ** -0.5),
        "conv_b": nrm(ks[18], (DEPTH, 2 * D_FF), 0.02),
        "w_down": nrm(ks[19], (DEPTH, D_FF, D_MODEL), D_FF ** -0.5),
    }


def reference(x, positions, g_mix, w_in, w_uq, w_ukv, g_q_lora, g_kv_lora, g_mla_q, g_mla_k,
              g_ca_q, g_ca_k, rel_bias, g_out_mla, g_out_ca, w_out, g_ffn, w_up, conv_w,
              conv_b, w_down):
    cos, sin = rope_tables(positions)
    for l in range(DEPTH):
        h = rmsnorm(x, g_mix[l])
        proj = h @ w_in[l]
        c_q = proj[..., OFF_CQ:OFF_CKV]
        c_kv = proj[..., OFF_CKV:OFF_KR]
        k_rope = proj[..., OFF_KR:OFF_CA]
        qkv_b = proj[..., OFF_CA:]
        o_a = mla_mixer(c_q, c_kv, k_rope, cos, sin, w_uq[l], w_ukv[l], g_q_lora[l],
                        g_kv_lora[l], g_mla_q[l], g_mla_k[l])
        o_b = chunk_rel_mixer(qkv_b, rel_bias[l], g_ca_q[l], g_ca_k[l])
        o = jnp.concatenate([rmsnorm(o_a, g_out_mla[l]), rmsnorm(o_b, g_out_ca[l])], axis=-1)
        x = x + o @ w_out[l]
        x = x + conv_glu_ffn(rmsnorm(x, g_ffn[l]), w_up[l], conv_w[l], conv_b[l], w_down[l])
    return x
```

```python
import functools

import jax
import jax.numpy as jnp
from jax import lax
from jax.experimental import pallas as pl
from jax.experimental.pallas import tpu as pltpu

F32 = jnp.float32
BF16 = jnp.bfloat16

D_MODEL = 1024
CHUNK = 64
EPS = 1e-6
NEG_INF = -1e30

MLA_HEADS = 8
MLA_NOPE = 64
MLA_ROPE = 32
MLA_QK = MLA_NOPE + MLA_ROPE
MLA_V = 64
Q_LORA = 256
KV_LORA = 128
ROPE_BASE = 10000.0
MLA_WIDTH = MLA_HEADS * MLA_V

CA_HEADS = 8
CA_HEAD_DIM = 64
CA_WIDTH = CA_HEADS * CA_HEAD_DIM
LEFT_CHUNKS = 8
REL_CLIP = 128

OFF_CQ = 0
OFF_CKV = OFF_CQ + Q_LORA
OFF_KR = OFF_CKV + KV_LORA
OFF_CA = OFF_KR + MLA_ROPE

D_FF = 2816
CONV_W = 3

LANES = 128
HEAD_SLOT = LANES
HALF_ROPE = MLA_ROPE // 2

P_CQ = 0
P_CKV = P_CQ + Q_LORA
P_KR = P_CKV + KV_LORA
P_CAQ = P_KR + HEAD_SLOT
P_CAK = P_CAQ + CA_WIDTH
P_CAV = P_CAK + CA_WIDTH
P_COLS = P_CAV + CA_WIDTH

TM_IN = 512
CA_PAD = LEFT_CHUNKS * CHUNK
CA_TQ = 256
CA_WIN = CA_PAD + CA_TQ
BASE_W = CA_WIN + CA_TQ
MLA_TQ = 256
TM_FFN = 512
FF_CHUNK = 256
N_FF_CHUNKS = D_FF // FF_CHUNK
CARRY_ROWS = 8

VMEM_LIMIT = 56 * 1024 * 1024


def _rms(x, g):
    return x * lax.rsqrt(jnp.mean(x * x, axis=-1, keepdims=True) + EPS) * g


def _dot(a, b):
    return jnp.dot(a, b, preferred_element_type=F32)


def _dot_nt(a, b):
    return lax.dot_general(a, b, (((1,), (1,)), ((), ())), preferred_element_type=F32)


def _rope_kernel(pos_ref, inv_ref, c_ref, shi_ref, slo_ref):
    ang = pos_ref[...] * inv_ref[...]
    c = jnp.cos(ang)
    s = jnp.sin(ang)
    lane = lax.broadcasted_iota(jnp.int32, ang.shape, 1)
    lo = (lane >= MLA_NOPE) & (lane < MLA_NOPE + HALF_ROPE)
    hi = (lane >= MLA_NOPE + HALF_ROPE) & (lane < MLA_QK)
    c_ref[...] = c
    shi_ref[...] = jnp.where(hi, s, 0.0)
    slo_ref[...] = jnp.where(lo, -s, 0.0)


def _rope_tables(pos_f, inv_full):
    t = pos_f.shape[0]
    tm = 1024
    out = jax.ShapeDtypeStruct((t, LANES), F32)
    return pl.pallas_call(
        _rope_kernel,
        grid=(t // tm,),
        in_specs=[pl.BlockSpec((tm, 1), lambda i: (i, 0)),
                  pl.BlockSpec((1, LANES), lambda i: (0, 0))],
        out_specs=[pl.BlockSpec((tm, LANES), lambda i: (i, 0))] * 3,
        out_shape=[out] * 3,
        compiler_params=pltpu.CompilerParams(dimension_semantics=("arbitrary",)),
    )(pos_f, inv_full)


def _bias_kernel(base_ref, out_ref):
    x = jnp.broadcast_to(base_ref[...], (CA_TQ, BASE_W))
    row = lax.broadcasted_iota(jnp.int32, (CA_TQ, BASE_W), 0)
    for b in range(CA_TQ.bit_length() - 1):
        x = jnp.where(((row >> b) & 1) == 1, pltpu.roll(x, 1 << b, 1), x)
    t = x[:, CA_TQ:]
    i = lax.broadcasted_iota(jnp.int32, (CA_TQ, CA_WIN), 0) // CHUNK
    j = lax.broadcasted_iota(jnp.int32, (CA_TQ, CA_WIN), 1) // CHUNK
    valid = (j >= i) & (j <= i + LEFT_CHUNKS)
    out_ref[...] = jnp.where(valid, t, NEG_INF)


def _bias_tiles(base):
    n = base.shape[0]
    return pl.pallas_call(
        _bias_kernel,
        grid=(n,),
        in_specs=[pl.BlockSpec((None, 1, BASE_W), lambda i: (i, 0, 0))],
        out_specs=pl.BlockSpec((None, CA_TQ, CA_WIN), lambda i: (i, 0, 0)),
        out_shape=jax.ShapeDtypeStruct((n, CA_TQ, CA_WIN), F32),
        compiler_params=pltpu.CompilerParams(dimension_semantics=("arbitrary",)),
    )(base)


def _inproj_kernel(x_ref, c_ref, shi_ref, slo_ref, gmix_ref, w1_ref, gqa_ref, gkva_ref,
                   wuq_ref, wuk_ref, wuv_ref, gq_ref, gk_ref, gcaq_ref, gcak_ref, bd_ref,
                   kpad_ref, vpad_ref,
                   qm_ref, km_ref, vm_ref, qc_ref, kc_ref, vc_ref):
    del kpad_ref, vpad_ref
    h = _rms(x_ref[...], gmix_ref[...]).astype(BF16)
    proj = _dot(h, w1_ref[...])
    cqn = _rms(proj[:, P_CQ:P_CKV], gqa_ref[...]).astype(BF16)
    ckvn = _rms(proj[:, P_CKV:P_KR], gkva_ref[...]).astype(BF16)
    kr = proj[:, P_KR:P_CAQ]
    q = _dot(cqn, wuq_ref[...])
    kn = _dot(ckvn, wuk_ref[...])
    vm_ref[...] = _dot(ckvn, wuv_ref[...]).astype(BF16)

    cos = c_ref[...]
    s_hi = shi_ref[...]
    s_lo = slo_ref[...]

    def norm_rope(t, g):
        r = lax.rsqrt(jnp.sum(t * t, axis=-1, keepdims=True) * (1.0 / MLA_QK) + EPS)
        y = t * r * g
        return (y * cos + pltpu.roll(y, HALF_ROPE, 1) * s_hi
                + pltpu.roll(y, HEAD_SLOT - HALF_ROPE, 1) * s_lo)

    for hd in range(MLA_HEADS):
        sl = slice(hd * HEAD_SLOT, (hd + 1) * HEAD_SLOT)
        qm_ref[:, sl] = norm_rope(q[:, sl], gq_ref[...]).astype(BF16)
        km_ref[:, sl] = norm_rope(kn[:, sl] + kr, gk_ref[...]).astype(BF16)

    def head_norm(t, g):
        ss = _dot((t * t).astype(BF16), bd_ref[...])
        return t * lax.rsqrt(ss * (1.0 / CA_HEAD_DIM) + EPS) * g

    qc_ref[...] = head_norm(proj[:, P_CAQ:P_CAK], gcaq_ref[...]).astype(BF16)
    kc_ref[...] = head_norm(proj[:, P_CAK:P_CAV], gcak_ref[...]).astype(BF16)
    vc_ref[...] = proj[:, P_CAV:P_COLS].astype(BF16)


def _inproj(l, x, cos, s_hi, s_lo, wp, kpad, vpad):
    b, s, _ = x.shape
    n = s // TM_IN
    pad_blocks = CA_PAD // TM_IN

    def tok(width):
        return pl.BlockSpec((None, TM_IN, width), lambda bi, i: (bi, i, 0))

    def lw(shape):
        return pl.BlockSpec((None,) + shape, lambda bi, i: (l,) + (0,) * len(shape))

    const = pl.BlockSpec((CA_WIDTH, CA_WIDTH), lambda bi, i: (0, 0))
    padded = pl.BlockSpec((None, TM_IN, CA_WIDTH), lambda bi, i: (bi, i + pad_blocks, 0))
    any_spec = pl.BlockSpec(memory_space=pl.ANY)
    tok_shape = lambda w: jax.ShapeDtypeStruct((b, s, w), BF16)
    pad_shape = jax.ShapeDtypeStruct((b, s + CA_PAD, CA_WIDTH), BF16)
    return pl.pallas_call(
        _inproj_kernel,
        grid=(b, n),
        in_specs=[tok(D_MODEL), tok(LANES), tok(LANES), tok(LANES),
                  lw((1, D_MODEL)), lw((D_MODEL, P_COLS)), lw((1, Q_LORA)), lw((1, KV_LORA)),
                  lw((Q_LORA, MLA_HEADS * HEAD_SLOT)), lw((KV_LORA, MLA_HEADS * HEAD_SLOT)),
                  lw((KV_LORA, MLA_WIDTH)), lw((1, HEAD_SLOT)), lw((1, HEAD_SLOT)),
                  lw((1, CA_WIDTH)), lw((1, CA_WIDTH)), const, any_spec, any_spec],
        out_specs=[tok(MLA_HEADS * HEAD_SLOT), tok(MLA_HEADS * HEAD_SLOT), tok(MLA_WIDTH),
                   tok(CA_WIDTH), padded, padded],
        out_shape=[tok_shape(MLA_HEADS * HEAD_SLOT), tok_shape(MLA_HEADS * HEAD_SLOT),
                   tok_shape(MLA_WIDTH), tok_shape(CA_WIDTH), pad_shape, pad_shape],
        input_output_aliases={16: 4, 17: 5},
        compiler_params=pltpu.CompilerParams(
            dimension_semantics=("arbitrary", "arbitrary"), vmem_limit_bytes=VMEM_LIMIT),
    )(x, cos, s_hi, s_lo, wp["g_mix"], wp["w1"], wp["g_qa"], wp["g_kva"], wp["w_uq"],
      wp["w_uk"], wp["w_uv"], wp["g_q"], wp["g_k"], wp["g_caq"], wp["g_cak"], wp["bd"],
      kpad, vpad)


def _mla_kernel(q_ref, k_ref, v_ref, o_ref, m_ref, l_ref, acc_ref):
    i = pl.program_id(2)
    tq = MLA_TQ
    lane = lax.broadcasted_iota(jnp.int32, (1, LANES), 1)
    low = lane < MLA_V
    m_ref[...] = jnp.full(m_ref.shape, NEG_INF, F32)
    l_ref[...] = jnp.zeros(l_ref.shape, F32)
    acc_ref[...] = jnp.zeros(acc_ref.shape, F32)

    def step(j, masked):
        rows = pl.ds(pl.multiple_of(j * tq, tq), tq)
        v = v_ref[rows, :]
        zero = jnp.zeros_like(v)
        pv = []
        alphas = []
        for hh in range(2):
            sl = slice(hh * HEAD_SLOT, (hh + 1) * HEAD_SLOT)
            s = _dot_nt(q_ref[:, sl], k_ref[rows, sl])
            if masked:
                r = lax.broadcasted_iota(jnp.int32, s.shape, 0) // CHUNK
                c = lax.broadcasted_iota(jnp.int32, s.shape, 1) // CHUNK
                s = jnp.where(c <= r, s, NEG_INF)
            m_prev = m_ref[hh]
            m_new = jnp.maximum(m_prev, jnp.max(s, axis=-1, keepdims=True))
            alpha = jnp.exp(m_prev - m_new)
            p = jnp.exp(s - m_new)
            l_ref[hh] = alpha * l_ref[hh] + jnp.sum(p, axis=-1, keepdims=True)
            m_ref[hh] = m_new
            vh = jnp.where(low if hh == 0 else ~low, v, zero)
            pv.append(_dot(p.astype(BF16), vh))
            alphas.append(alpha)
        acc_ref[...] = acc_ref[...] * jnp.where(low, alphas[0], alphas[1]) + pv[0] + pv[1]

    def body(j, carry):
        step(j, False)
        return carry

    lax.fori_loop(0, i, body, 0)
    step(i, True)
    inv = jnp.where(low, 1.0 / l_ref[0], 1.0 / l_ref[1])
    o_ref[...] = (acc_ref[...] * inv).astype(BF16)


def _mla_attention(qm, km, vm):
    b, s, _ = qm.shape
    pairs = MLA_HEADS // 2
    return pl.pallas_call(
        _mla_kernel,
        grid=(b, pairs, s // MLA_TQ),
        in_specs=[pl.BlockSpec((None, MLA_TQ, 2 * HEAD_SLOT), lambda bi, p, i: (bi, i, p)),
                  pl.BlockSpec((None, s, 2 * HEAD_SLOT), lambda bi, p, i: (bi, 0, p)),
                  pl.BlockSpec((None, s, LANES), lambda bi, p, i: (bi, 0, p))],
        out_specs=pl.BlockSpec((None, MLA_TQ, LANES), lambda bi, p, i: (bi, i, p)),
        out_shape=jax.ShapeDtypeStruct((b, s, MLA_WIDTH), BF16),
        scratch_shapes=[pltpu.VMEM((2, MLA_TQ, 1), F32), pltpu.VMEM((2, MLA_TQ, 1), F32),
                        pltpu.VMEM((MLA_TQ, LANES), F32)],
        compiler_params=pltpu.CompilerParams(
            dimension_semantics=("arbitrary", "arbitrary", "arbitrary"),
            vmem_limit_bytes=VMEM_LIMIT),
    )(qm, km, vm)


def _ca_kernel(q_ref, k_ref, v_ref, bias_ref, o_ref):
    qi = pl.program_id(1)
    start = pl.multiple_of(qi * CA_TQ, CA_TQ)
    rows = pl.ds(start, CA_WIN)
    col = lax.broadcasted_iota(jnp.int32, (CA_TQ, CA_WIN), 1)
    real = col >= CA_PAD - qi * CA_TQ
    lane = lax.broadcasted_iota(jnp.int32, (1, LANES), 1)
    low = lane < CA_HEAD_DIM
    for p in range(CA_HEADS // 2):
        sl = slice(p * LANES, (p + 1) * LANES)
        q = q_ref[:, sl]
        k = k_ref[rows, sl]
        v = v_ref[rows, sl]
        out = None
        for hh in range(2):
            sel = low if hh == 0 else ~low
            s = _dot_nt(jnp.where(sel, q, jnp.zeros_like(q)), k) + bias_ref[2 * p + hh]
            s = jnp.where(real, s, NEG_INF)
            m = jnp.max(s, axis=-1, keepdims=True)
            e = jnp.exp(s - m)
            denom = jnp.sum(e, axis=-1, keepdims=True)
            o = _dot(e.astype(BF16), jnp.where(sel, v, jnp.zeros_like(v))) * (1.0 / denom)
            out = o if out is None else out + o
        o_ref[:, sl] = out.astype(BF16)


def _chunk_attention(l, qc, kc, vc, bias):
    b, s, _ = qc.shape
    return pl.pallas_call(
        _ca_kernel,
        grid=(b, s // CA_TQ),
        in_specs=[pl.BlockSpec((None, CA_TQ, CA_WIDTH), lambda bi, i: (bi, i, 0)),
                  pl.BlockSpec((None, s + CA_PAD, CA_WIDTH), lambda bi, i: (bi, 0, 0)),
                  pl.BlockSpec((None, s + CA_PAD, CA_WIDTH), lambda bi, i: (bi, 0, 0)),
                  pl.BlockSpec((None, CA_HEADS, CA_TQ, CA_WIN), lambda bi, i: (l, 0, 0, 0))],
        out_specs=pl.BlockSpec((None, CA_TQ, CA_WIDTH), lambda bi, i: (bi, i, 0)),
        out_shape=jax.ShapeDtypeStruct((b, s, CA_WIDTH), BF16),
        compiler_params=pltpu.CompilerParams(
            dimension_semantics=("arbitrary", "arbitrary"), vmem_limit_bytes=VMEM_LIMIT),
    )(qc, kc, vc, bias)


def _ffn_kernel(x_ref, oa_ref, ob_ref, goa_ref, gob_ref, wout_ref, gffn_ref, wup_ref, cw_ref,
                cb_ref, wdown_ref, out_ref, carry_ref, u_ref, x1_ref, h_ref, acc_ref):
    tm = TM_FFN

    @pl.when(pl.program_id(1) == 0)
    def _():
        carry_ref[...] = jnp.zeros(carry_ref.shape, F32)

    na = _rms(oa_ref[...].astype(F32), goa_ref[...]).astype(BF16)
    nb = _rms(ob_ref[...].astype(F32), gob_ref[...]).astype(BF16)
    x1 = (x_ref[...] + _dot(na, wout_ref[0:MLA_WIDTH, :])
          + _dot(nb, wout_ref[MLA_WIDTH:MLA_WIDTH + CA_WIDTH, :]))
    x1_ref[...] = x1
    h_ref[...] = _rms(x1, gffn_ref[...]).astype(BF16)
    acc_ref[...] = jnp.zeros(acc_ref.shape, F32)

    def body(c, carry):
        u = _dot(h_ref[...], wup_ref[c])
        u_ref[0:CARRY_ROWS, :] = carry_ref[c]
        u_ref[CARRY_ROWS:CARRY_ROWS + tm, :] = u
        carry_ref[c] = u[tm - CARRY_ROWS:, :]
        cw = cw_ref[c]
        y = cb_ref[c]
        for tap in range(CONV_W):
            off = CARRY_ROWS - (CONV_W - 1) + tap
            y = y + u_ref[off:off + tm, :] * cw[tap:tap + 1, :]
        g = y[:, :FF_CHUNK]
        a = g * (1.0 / (1.0 + jnp.exp(-g))) * y[:, FF_CHUNK:]
        acc_ref[...] += _dot(a.astype(BF16), wdown_ref[c])
        return carry

    lax.fori_loop(0, N_FF_CHUNKS, body, 0)
    out_ref[...] = x1_ref[...] + acc_ref[...]


def _outproj_ffn(l, x, oa, ob, wp):
    b, s, _ = x.shape

    def tok(width):
        return pl.BlockSpec((None, TM_FFN, width), lambda bi, i: (bi, i, 0))

    def lw(shape):
        return pl.BlockSpec((None,) + shape, lambda bi, i: (l,) + (0,) * len(shape),
                            pipeline_mode=pl.Buffered(1))

    return pl.pallas_call(
        _ffn_kernel,
        grid=(b, s // TM_FFN),
        in_specs=[tok(D_MODEL), tok(MLA_WIDTH), tok(CA_WIDTH),
                  lw((1, MLA_WIDTH)), lw((1, CA_WIDTH)), lw((D_MODEL, D_MODEL)),
                  lw((1, D_MODEL)), lw((N_FF_CHUNKS, D_MODEL, 2 * FF_CHUNK)),
                  lw((N_FF_CHUNKS, CARRY_ROWS, 2 * FF_CHUNK)),
                  lw((N_FF_CHUNKS, 1, 2 * FF_CHUNK)),
                  lw((N_FF_CHUNKS, FF_CHUNK, D_MODEL))],
        out_specs=tok(D_MODEL),
        out_shape=jax.ShapeDtypeStruct((b, s, D_MODEL), F32),
        scratch_shapes=[pltpu.VMEM((N_FF_CHUNKS, CARRY_ROWS, 2 * FF_CHUNK), F32),
                        pltpu.VMEM((CARRY_ROWS + TM_FFN, 2 * FF_CHUNK), F32),
                        pltpu.VMEM((TM_FFN, D_MODEL), F32),
                        pltpu.VMEM((TM_FFN, D_MODEL), BF16),
                        pltpu.VMEM((TM_FFN, D_MODEL), F32)],
        compiler_params=pltpu.CompilerParams(
            dimension_semantics=("arbitrary", "arbitrary"), vmem_limit_bytes=VMEM_LIMIT),
    )(x, oa, ob, wp["g_oa"], wp["g_ob"], wp["w_out"], wp["g_ffn"], wp["w_up"], wp["conv_w"],
      wp["conv_b"], wp["w_down"])


def _prepare(g_mix, w_in, w_uq, w_ukv, g_q_lora, g_kv_lora, g_mla_q, g_mla_k, g_ca_q, g_ca_k,
             rel_bias, g_out_mla, g_out_ca, w_out, g_ffn, w_up, conv_w, conv_b, w_down):
    depth = w_in.shape[0]
    row = lambda g: g[:, None, :]
    zpad = HEAD_SLOT - MLA_QK

    kr_slot = jnp.pad(w_in[:, :, OFF_KR:OFF_CA], ((0, 0), (0, 0), (MLA_NOPE, zpad)))
    w1 = jnp.concatenate([w_in[:, :, OFF_CQ:OFF_KR], kr_slot, w_in[:, :, OFF_CA:]], axis=-1)

    wq = w_uq.reshape(depth, Q_LORA, MLA_HEADS, MLA_QK)
    wq = jnp.pad(wq, ((0, 0), (0, 0), (0, 0), (0, zpad))).reshape(depth, Q_LORA, -1)
    wkv = w_ukv.reshape(depth, KV_LORA, MLA_HEADS, MLA_NOPE + MLA_V)
    wk = jnp.pad(wkv[..., :MLA_NOPE], ((0, 0), (0, 0), (0, 0), (0, HEAD_SLOT - MLA_NOPE)))
    wk = wk.reshape(depth, KV_LORA, -1)
    wv = wkv[..., MLA_NOPE:].reshape(depth, KV_LORA, MLA_WIDTH)

    g_q = jnp.pad(g_mla_q * (MLA_QK ** -0.5), ((0, 0), (0, zpad)))
    g_k = jnp.pad(g_mla_k, ((0, 0), (0, zpad)))
    g_caq = jnp.tile(g_ca_q * (CA_HEAD_DIM ** -0.5), (1, CA_HEADS))
    g_cak = jnp.tile(g_ca_k, (1, CA_HEADS))
    head = jnp.arange(CA_WIDTH) // CA_HEAD_DIM
    bd = (head[:, None] == head[None, :]).astype(BF16)

    ext = jnp.pad(rel_bias, ((0, 0), (0, 0), (CA_TQ - 1 - REL_CLIP, CA_WIN - REL_CLIP)),
                  mode="edge")
    base = ext[:, :, ::-1].reshape(depth * CA_HEADS, 1, BASE_W)

    def chunks(w):
        g = w[..., :D_FF].reshape(w.shape[:-1] + (N_FF_CHUNKS, FF_CHUNK))
        v = w[..., D_FF:].reshape(w.shape[:-1] + (N_FF_CHUNKS, FF_CHUNK))
        return jnp.concatenate([g, v], axis=-1)

    w_up_c = jnp.moveaxis(chunks(w_up), 2, 1)
    conv_c = jnp.moveaxis(chunks(conv_w), 2, 1)
    conv_c = jnp.pad(conv_c, ((0, 0), (0, 0), (0, CARRY_ROWS - CONV_W), (0, 0)))
    cb_c = chunks(conv_b)[:, :, None, :]
    return {
        "g_mix": row(g_mix), "w1": w1.astype(BF16), "g_qa": row(g_q_lora),
        "g_kva": row(g_kv_lora), "w_uq": wq.astype(BF16), "w_uk": wk.astype(BF16),
        "w_uv": wv.astype(BF16), "g_q": row(g_q), "g_k": row(g_k), "g_caq": row(g_caq),
        "g_cak": row(g_cak), "bd": bd, "base": base,
        "g_oa": row(g_out_mla), "g_ob": row(g_out_ca), "w_out": w_out.astype(BF16),
        "g_ffn": row(g_ffn), "w_up": w_up_c.astype(BF16), "conv_w": conv_c, "conv_b": cb_c,
        "w_down": w_down.reshape(depth, N_FF_CHUNKS, FF_CHUNK, D_MODEL).astype(BF16),
    }


def kernel(x, positions, g_mix, w_in, w_uq, w_ukv, g_q_lora, g_kv_lora, g_mla_q, g_mla_k,
           g_ca_q, g_ca_k, rel_bias, g_out_mla, g_out_ca, w_out, g_ffn, w_up, conv_w, conv_b,
           w_down):
    b, s, _ = x.shape
    depth = w_in.shape[0]
    assert s % TM_IN == 0 and s % TM_FFN == 0 and s % CA_TQ == 0 and s % MLA_TQ == 0
    assert CA_PAD % TM_IN == 0
    wp = _prepare(g_mix, w_in, w_uq, w_ukv, g_q_lora, g_kv_lora, g_mla_q, g_mla_k, g_ca_q,
                  g_ca_k, rel_bias, g_out_mla, g_out_ca, w_out, g_ffn, w_up, conv_w, conv_b,
                  w_down)

    inv = ROPE_BASE ** (-jnp.arange(0, MLA_ROPE, 2, dtype=F32) / MLA_ROPE)
    inv_full = jnp.zeros((1, LANES), F32)
    inv_full = inv_full.at[0, MLA_NOPE:MLA_NOPE + HALF_ROPE].set(inv)
    inv_full = inv_full.at[0, MLA_NOPE + HALF_ROPE:MLA_QK].set(inv)
    pos_f = positions.astype(F32).reshape(b * s, 1)
    cos, s_hi, s_lo = [t.reshape(b, s, LANES) for t in _rope_tables(pos_f, inv_full)]
    bias = _bias_tiles(wp["base"]).reshape(depth, CA_HEADS, CA_TQ, CA_WIN)

    for l in range(depth):
        kpad = jnp.zeros((b, s + CA_PAD, CA_WIDTH), BF16)
        vpad = jnp.zeros((b, s + CA_PAD, CA_WIDTH), BF16)
        qm, km, vm, qc, kc, vc = _inproj(l, x, cos, s_hi, s_lo, wp, kpad, vpad)
        oa = _mla_attention(qm, km, vm)
        ob = _chunk_attention(l, qc, kc, vc, bias)
        x = _outproj_ffn(l, x, oa, ob, wp)
    return x
```

```python
import functools

import jax
import jax.numpy as jnp
from jax import lax
from jax.experimental import pallas as pl
from jax.experimental.pallas import tpu as pltpu

F32 = jnp.float32
BF16 = jnp.bfloat16

D_MODEL = 1024
CHUNK = 64
EPS = 1e-6
NEG_INF = -1e30
LOG2_E = 1.4426950408889634

MLA_HEADS = 8
MLA_NOPE = 64
MLA_ROPE = 32
MLA_QK = MLA_NOPE + MLA_ROPE
MLA_V = 64
Q_LORA = 256
KV_LORA = 128
ROPE_BASE = 10000.0
MLA_WIDTH = MLA_HEADS * MLA_V

CA_HEADS = 8
CA_HEAD_DIM = 64
CA_WIDTH = CA_HEADS * CA_HEAD_DIM
LEFT_CHUNKS = 8
REL_CLIP = 128

OFF_CQ = 0
OFF_CKV = OFF_CQ + Q_LORA
OFF_KR = OFF_CKV + KV_LORA
OFF_CA = OFF_KR + MLA_ROPE

D_FF = 2816
CONV_W = 3

LANES = 128
HEAD_SLOT = LANES
HALF_ROPE = MLA_ROPE // 2

P_CQ = 0
P_CKV = P_CQ + Q_LORA
P_KR = P_CKV + KV_LORA
P_CAQ = P_KR + HEAD_SLOT
P_CAK = P_CAQ + CA_WIDTH
P_CAV = P_CAK + CA_WIDTH
P_COLS = P_CAV + CA_WIDTH

TM_IN = 512
CA_PAD = LEFT_CHUNKS * CHUNK
CA_TQ = 256
CA_WIN = CA_PAD + CA_TQ
BASE_W = CA_WIN + CA_TQ
MLA_TQ = 256
MLA_HPS = 8
MLA_LOOKAHEAD = 4
MLA_PV_DELAY = 1
TM_FFN = 512
FF_CHUNK = 256
N_FF_CHUNKS = D_FF // FF_CHUNK
CARRY_ROWS = 8

VMEM_LIMIT = 56 * 1024 * 1024


def _rms(x, g):
    return x * lax.rsqrt(jnp.mean(x * x, axis=-1, keepdims=True) + EPS) * g


def _dot(a, b):
    return jnp.dot(a, b, preferred_element_type=F32)


def _dot_nt(a, b):
    return lax.dot_general(a, b, (((1,), (1,)), ((), ())), preferred_element_type=F32)


def _rope_kernel(pos_ref, inv_ref, c_ref, shi_ref, slo_ref):
    ang = pos_ref[...] * inv_ref[...]
    c = jnp.cos(ang)
    s = jnp.sin(ang)
    lane = lax.broadcasted_iota(jnp.int32, ang.shape, 1)
    lo = (lane >= MLA_NOPE) & (lane < MLA_NOPE + HALF_ROPE)
    hi = (lane >= MLA_NOPE + HALF_ROPE) & (lane < MLA_QK)
    c_ref[...] = c
    shi_ref[...] = jnp.where(hi, s, 0.0)
    slo_ref[...] = jnp.where(lo, -s, 0.0)


def _rope_tables(pos_f, inv_full):
    t = pos_f.shape[0]
    tm = 1024
    out = jax.ShapeDtypeStruct((t, LANES), F32)
    return pl.pallas_call(
        _rope_kernel,
        grid=(t // tm,),
        in_specs=[pl.BlockSpec((tm, 1), lambda i: (i, 0)),
                  pl.BlockSpec((1, LANES), lambda i: (0, 0))],
        out_specs=[pl.BlockSpec((tm, LANES), lambda i: (i, 0))] * 3,
        out_shape=[out] * 3,
        compiler_params=pltpu.CompilerParams(dimension_semantics=("arbitrary",)),
    )(pos_f, inv_full)


def _bias_kernel(base_ref, out_ref):
    x = jnp.broadcast_to(base_ref[...], (CA_TQ, BASE_W))
    row = lax.broadcasted_iota(jnp.int32, (CA_TQ, BASE_W), 0)
    for b in range(CA_TQ.bit_length() - 1):
        x = jnp.where(((row >> b) & 1) == 1, pltpu.roll(x, 1 << b, 1), x)
    t = x[:, CA_TQ:]
    i = lax.broadcasted_iota(jnp.int32, (CA_TQ, CA_WIN), 0) // CHUNK
    j = lax.broadcasted_iota(jnp.int32, (CA_TQ, CA_WIN), 1) // CHUNK
    valid = (j >= i) & (j <= i + LEFT_CHUNKS)
    out_ref[...] = jnp.where(valid, t, NEG_INF)


def _bias_tiles(base):
    n = base.shape[0]
    return pl.pallas_call(
        _bias_kernel,
        grid=(n,),
        in_specs=[pl.BlockSpec((None, 1, BASE_W), lambda i: (i, 0, 0))],
        out_specs=pl.BlockSpec((None, CA_TQ, CA_WIN), lambda i: (i, 0, 0)),
        out_shape=jax.ShapeDtypeStruct((n, CA_TQ, CA_WIN), F32),
        compiler_params=pltpu.CompilerParams(dimension_semantics=("arbitrary",)),
    )(base)


def _inproj_kernel(x_ref, c_ref, shi_ref, slo_ref, gmix_ref, w1_ref, gqa_ref, gkva_ref,
                   wuq_ref, wuk_ref, wuv_ref, gq_ref, gk_ref, gcaq_ref, gcak_ref, bd_ref,
                   kpad_ref, vpad_ref,
                   qm_ref, km_ref, vm_ref, qc_ref, kc_ref, vc_ref):
    del kpad_ref, vpad_ref
    h = _rms(x_ref[...], gmix_ref[...]).astype(BF16)
    proj = _dot(h, w1_ref[...])
    cqn = _rms(proj[:, P_CQ:P_CKV], gqa_ref[...]).astype(BF16)
    ckvn = _rms(proj[:, P_CKV:P_KR], gkva_ref[...]).astype(BF16)
    kr = proj[:, P_KR:P_CAQ]
    q = _dot(cqn, wuq_ref[...])
    kn = _dot(ckvn, wuk_ref[...])
    vm_ref[...] = _dot(ckvn, wuv_ref[...]).astype(BF16)

    cos = c_ref[...]
    s_hi = shi_ref[...]
    s_lo = slo_ref[...]

    def norm_rope(t, g):
        r = lax.rsqrt(jnp.sum(t * t, axis=-1, keepdims=True) * (1.0 / MLA_QK) + EPS)
        y = t * r * g
        return (y * cos + pltpu.roll(y, HALF_ROPE, 1) * s_hi
                + pltpu.roll(y, HEAD_SLOT - HALF_ROPE, 1) * s_lo)

    for hd in range(MLA_HEADS):
        sl = slice(hd * HEAD_SLOT, (hd + 1) * HEAD_SLOT)
        qm_ref[:, sl] = norm_rope(q[:, sl], gq_ref[...]).astype(BF16)
        km_ref[:, sl] = norm_rope(kn[:, sl] + kr, gk_ref[...]).astype(BF16)

    def head_norm(t, g):
        ss = _dot((t * t).astype(BF16), bd_ref[...])
        return t * lax.rsqrt(ss * (1.0 / CA_HEAD_DIM) + EPS) * g

    qc_ref[...] = head_norm(proj[:, P_CAQ:P_CAK], gcaq_ref[...]).astype(BF16)
    kc_ref[...] = head_norm(proj[:, P_CAK:P_CAV], gcak_ref[...]).astype(BF16)
    vc_ref[...] = proj[:, P_CAV:P_COLS].astype(BF16)


def _inproj(l, x, cos, s_hi, s_lo, wp, kpad, vpad):
    b, s, _ = x.shape
    n = s // TM_IN
    pad_blocks = CA_PAD // TM_IN

    def tok(width):
        return pl.BlockSpec((None, TM_IN, width), lambda bi, i: (bi, i, 0))

    def lw(shape):
        return pl.BlockSpec((None,) + shape, lambda bi, i: (l,) + (0,) * len(shape))

    const = pl.BlockSpec((CA_WIDTH, CA_WIDTH), lambda bi, i: (0, 0))
    padded = pl.BlockSpec((None, TM_IN, CA_WIDTH), lambda bi, i: (bi, i + pad_blocks, 0))
    any_spec = pl.BlockSpec(memory_space=pl.ANY)
    tok_shape = lambda w: jax.ShapeDtypeStruct((b, s, w), BF16)
    pad_shape = jax.ShapeDtypeStruct((b, s + CA_PAD, CA_WIDTH), BF16)
    return pl.pallas_call(
        _inproj_kernel,
        grid=(b, n),
        in_specs=[tok(D_MODEL), tok(LANES), tok(LANES), tok(LANES),
                  lw((1, D_MODEL)), lw((D_MODEL, P_COLS)), lw((1, Q_LORA)), lw((1, KV_LORA)),
                  lw((Q_LORA, MLA_HEADS * HEAD_SLOT)), lw((KV_LORA, MLA_HEADS * HEAD_SLOT)),
                  lw((KV_LORA, MLA_WIDTH)), lw((1, HEAD_SLOT)), lw((1, HEAD_SLOT)),
                  lw((1, CA_WIDTH)), lw((1, CA_WIDTH)), const, any_spec, any_spec],
        out_specs=[tok(MLA_HEADS * HEAD_SLOT), tok(MLA_HEADS * HEAD_SLOT), tok(MLA_WIDTH),
                   tok(CA_WIDTH), padded, padded],
        out_shape=[tok_shape(MLA_HEADS * HEAD_SLOT), tok_shape(MLA_HEADS * HEAD_SLOT),
                   tok_shape(MLA_WIDTH), tok_shape(CA_WIDTH), pad_shape, pad_shape],
        input_output_aliases={16: 4, 17: 5},
        compiler_params=pltpu.CompilerParams(
            dimension_semantics=("arbitrary", "arbitrary"), vmem_limit_bytes=VMEM_LIMIT),
    )(x, cos, s_hi, s_lo, wp["g_mix"], wp["w1"], wp["g_qa"], wp["g_kva"], wp["w_uq"],
      wp["w_uk"], wp["w_uv"], wp["g_q"], wp["g_k"], wp["g_caq"], wp["g_cak"], wp["bd"],
      kpad, vpad)


def _mla_kernel(q_ref, k_ref, vt_ref, o_ref, m_ref, acc_ref, s_ref):
    i = pl.program_id(2)
    t = MLA_TQ
    ones = jnp.ones((MLA_V, t), BF16)
    m_ref[...] = jnp.full(m_ref.shape, NEG_INF, F32)
    acc_ref[...] = jnp.zeros(acc_ref.shape, F32)

    def scores(j, hh):
        rows = pl.ds(pl.multiple_of(j * t, t), t)
        sl = slice(hh * HEAD_SLOT, (hh + 1) * HEAD_SLOT)
        return _dot_nt(k_ref[rows, sl], q_ref[:, sl])

    def step(j, last):
        def accumulate(hh, p, alpha):
            v1 = jnp.concatenate([vt_ref[hh, j], ones], axis=0)
            acc_ref[hh] = acc_ref[hh] * alpha + _dot(v1, p)

        pending = {hh: s_ref[hh] for hh in range(MLA_LOOKAHEAD)}
        done = {}
        for hh in range(MLA_HPS):
            st = pending.pop(hh)
            ahead = hh + MLA_LOOKAHEAD
            if ahead < MLA_HPS:
                pending[ahead] = scores(j, ahead)
            elif not last:
                s_ref[ahead - MLA_HPS] = scores(j + 1, ahead - MLA_HPS)
            if last:
                kc = lax.broadcasted_iota(jnp.int32, st.shape, 0) // CHUNK
                qc = lax.broadcasted_iota(jnp.int32, st.shape, 1) // CHUNK
                st = jnp.where(kc <= qc, st, NEG_INF)
            m_prev = m_ref[hh]
            m_new = jnp.maximum(m_prev, jnp.max(st, axis=0, keepdims=True))
            m_ref[hh] = m_new
            done[hh] = (jnp.exp2(st - m_new).astype(BF16), jnp.exp2(m_prev - m_new))
            if hh >= MLA_PV_DELAY:
                accumulate(hh - MLA_PV_DELAY, *done.pop(hh - MLA_PV_DELAY))
        for hh in sorted(done):
            accumulate(hh, *done[hh])

    for hh in range(MLA_LOOKAHEAD):
        s_ref[hh] = scores(0, hh)

    def body(j, carry):
        step(j, False)
        return carry

    lax.fori_loop(0, i, body, 0)
    step(i, True)
    for pr in range(MLA_HPS // 2):
        ot = jnp.concatenate([acc_ref[hh, :MLA_V, :] / acc_ref[hh, MLA_V:, :]
                              for hh in (2 * pr, 2 * pr + 1)], axis=0)
        o_ref[:, pr * LANES:(pr + 1) * LANES] = ot.T.astype(BF16)


def _mla_attention(qm, km, vm):
    b, s, _ = qm.shape
    groups = MLA_HEADS // MLA_HPS
    nk = s // MLA_TQ
    vt = vm.reshape(b, nk, MLA_TQ, MLA_HEADS, MLA_V).transpose(0, 3, 1, 4, 2)
    return pl.pallas_call(
        _mla_kernel,
        grid=(b, groups, s // MLA_TQ),
        in_specs=[pl.BlockSpec((None, MLA_TQ, MLA_HPS * HEAD_SLOT), lambda bi, g, i: (bi, i, g)),
                  pl.BlockSpec((None, s, MLA_HPS * HEAD_SLOT), lambda bi, g, i: (bi, 0, g)),
                  pl.BlockSpec((None, MLA_HPS, nk, MLA_V, MLA_TQ),
                               lambda bi, g, i: (bi, g, 0, 0, 0))],
        out_specs=pl.BlockSpec((None, MLA_TQ, MLA_HPS * MLA_V), lambda bi, g, i: (bi, i, g)),
        out_shape=jax.ShapeDtypeStruct((b, s, MLA_WIDTH), BF16),
        scratch_shapes=[pltpu.VMEM((MLA_HPS, 1, MLA_TQ), F32),
                        pltpu.VMEM((MLA_HPS, LANES, MLA_TQ), F32),
                        pltpu.VMEM((MLA_LOOKAHEAD, MLA_TQ, MLA_TQ), F32)],
        compiler_params=pltpu.CompilerParams(
            dimension_semantics=("arbitrary", "arbitrary", "arbitrary"),
            vmem_limit_bytes=VMEM_LIMIT),
    )(qm, km, vt)


def _ca_kernel(q_ref, k_ref, v_ref, bias_ref, o_ref):
    qi = pl.program_id(1)
    start = pl.multiple_of(qi * CA_TQ, CA_TQ)
    rows = pl.ds(start, CA_WIN)
    col = lax.broadcasted_iota(jnp.int32, (CA_TQ, CA_WIN), 1)
    real = col >= CA_PAD - qi * CA_TQ
    lane = lax.broadcasted_iota(jnp.int32, (1, LANES), 1)
    low = lane < CA_HEAD_DIM
    for p in range(CA_HEADS // 2):
        sl = slice(p * LANES, (p + 1) * LANES)
        q = q_ref[:, sl]
        k = k_ref[rows, sl]
        v = v_ref[rows, sl]
        out = None
        for hh in range(2):
            sel = low if hh == 0 else ~low
            s = _dot_nt(jnp.where(sel, q, jnp.zeros_like(q)), k) + bias_ref[2 * p + hh]
            s = jnp.where(real, s, NEG_INF)
            m = jnp.max(s, axis=-1, keepdims=True)
            e = jnp.exp(s - m)
            denom = jnp.sum(e, axis=-1, keepdims=True)
            o = _dot(e.astype(BF16), jnp.where(sel, v, jnp.zeros_like(v))) * (1.0 / denom)
            out = o if out is None else out + o
        o_ref[:, sl] = out.astype(BF16)


def _chunk_attention(l, qc, kc, vc, bias):
    b, s, _ = qc.shape
    return pl.pallas_call(
        _ca_kernel,
        grid=(b, s // CA_TQ),
        in_specs=[pl.BlockSpec((None, CA_TQ, CA_WIDTH), lambda bi, i: (bi, i, 0)),
                  pl.BlockSpec((None, s + CA_PAD, CA_WIDTH), lambda bi, i: (bi, 0, 0)),
                  pl.BlockSpec((None, s + CA_PAD, CA_WIDTH), lambda bi, i: (bi, 0, 0)),
                  pl.BlockSpec((None, CA_HEADS, CA_TQ, CA_WIN), lambda bi, i: (l, 0, 0, 0))],
        out_specs=pl.BlockSpec((None, CA_TQ, CA_WIDTH), lambda bi, i: (bi, i, 0)),
        out_shape=jax.ShapeDtypeStruct((b, s, CA_WIDTH), BF16),
        compiler_params=pltpu.CompilerParams(
            dimension_semantics=("arbitrary", "arbitrary"), vmem_limit_bytes=VMEM_LIMIT),
    )(qc, kc, vc, bias)


def _ffn_kernel(x_ref, oa_ref, ob_ref, goa_ref, gob_ref, wout_ref, gffn_ref, wup_ref, cw_ref,
                cb_ref, wdown_ref, out_ref, carry_ref, u_ref, x1_ref, h_ref, acc_ref):
    tm = TM_FFN

    @pl.when(pl.program_id(1) == 0)
    def _():
        carry_ref[...] = jnp.zeros(carry_ref.shape, F32)

    na = _rms(oa_ref[...].astype(F32), goa_ref[...]).astype(BF16)
    nb = _rms(ob_ref[...].astype(F32), gob_ref[...]).astype(BF16)
    x1 = (x_ref[...] + _dot(na, wout_ref[0:MLA_WIDTH, :])
          + _dot(nb, wout_ref[MLA_WIDTH:MLA_WIDTH + CA_WIDTH, :]))
    x1_ref[...] = x1
    h_ref[...] = _rms(x1, gffn_ref[...]).astype(BF16)
    acc_ref[...] = jnp.zeros(acc_ref.shape, F32)

    def body(c, carry):
        u = _dot(h_ref[...], wup_ref[c])
        u_ref[0:CARRY_ROWS, :] = carry_ref[c]
        u_ref[CARRY_ROWS:CARRY_ROWS + tm, :] = u
        carry_ref[c] = u[tm - CARRY_ROWS:, :]
        cw = cw_ref[c]
        y = cb_ref[c]
        for tap in range(CONV_W):
            off = CARRY_ROWS - (CONV_W - 1) + tap
            y = y + u_ref[off:off + tm, :] * cw[tap:tap + 1, :]
        g = y[:, :FF_CHUNK]
        a = g * (1.0 / (1.0 + jnp.exp(-g))) * y[:, FF_CHUNK:]
        acc_ref[...] += _dot(a.astype(BF16), wdown_ref[c])
        return carry

    lax.fori_loop(0, N_FF_CHUNKS, body, 0)
    out_ref[...] = x1_ref[...] + acc_ref[...]


def _outproj_ffn(l, x, oa, ob, wp):
    b, s, _ = x.shape

    def tok(width):
        return pl.BlockSpec((None, TM_FFN, width), lambda bi, i: (bi, i, 0))

    def lw(shape):
        return pl.BlockSpec((None,) + shape, lambda bi, i: (l,) + (0,) * len(shape),
                            pipeline_mode=pl.Buffered(1))

    return pl.pallas_call(
        _ffn_kernel,
        grid=(b, s // TM_FFN),
        in_specs=[tok(D_MODEL), tok(MLA_WIDTH), tok(CA_WIDTH),
                  lw((1, MLA_WIDTH)), lw((1, CA_WIDTH)), lw((D_MODEL, D_MODEL)),
                  lw((1, D_MODEL)), lw((N_FF_CHUNKS, D_MODEL, 2 * FF_CHUNK)),
                  lw((N_FF_CHUNKS, CARRY_ROWS, 2 * FF_CHUNK)),
                  lw((N_FF_CHUNKS, 1, 2 * FF_CHUNK)),
                  lw((N_FF_CHUNKS, FF_CHUNK, D_MODEL))],
        out_specs=tok(D_MODEL),
        out_shape=jax.ShapeDtypeStruct((b, s, D_MODEL), F32),
        scratch_shapes=[pltpu.VMEM((N_FF_CHUNKS, CARRY_ROWS, 2 * FF_CHUNK), F32),
                        pltpu.VMEM((CARRY_ROWS + TM_FFN, 2 * FF_CHUNK), F32),
                        pltpu.VMEM((TM_FFN, D_MODEL), F32),
                        pltpu.VMEM((TM_FFN, D_MODEL), BF16),
                        pltpu.VMEM((TM_FFN, D_MODEL), F32)],
        compiler_params=pltpu.CompilerParams(
            dimension_semantics=("arbitrary", "arbitrary"), vmem_limit_bytes=VMEM_LIMIT),
    )(x, oa, ob, wp["g_oa"], wp["g_ob"], wp["w_out"], wp["g_ffn"], wp["w_up"], wp["conv_w"],
      wp["conv_b"], wp["w_down"])


def _prepare(g_mix, w_in, w_uq, w_ukv, g_q_lora, g_kv_lora, g_mla_q, g_mla_k, g_ca_q, g_ca_k,
             rel_bias, g_out_mla, g_out_ca, w_out, g_ffn, w_up, conv_w, conv_b, w_down):
    depth = w_in.shape[0]
    row = lambda g: g[:, None, :]
    zpad = HEAD_SLOT - MLA_QK

    kr_slot = jnp.pad(w_in[:, :, OFF_KR:OFF_CA], ((0, 0), (0, 0), (MLA_NOPE, zpad)))
    w1 = jnp.concatenate([w_in[:, :, OFF_CQ:OFF_KR], kr_slot, w_in[:, :, OFF_CA:]], axis=-1)

    wq = w_uq.reshape(depth, Q_LORA, MLA_HEADS, MLA_QK)
    wq = jnp.pad(wq, ((0, 0), (0, 0), (0, 0), (0, zpad))).reshape(depth, Q_LORA, -1)
    wkv = w_ukv.reshape(depth, KV_LORA, MLA_HEADS, MLA_NOPE + MLA_V)
    wk = jnp.pad(wkv[..., :MLA_NOPE], ((0, 0), (0, 0), (0, 0), (0, HEAD_SLOT - MLA_NOPE)))
    wk = wk.reshape(depth, KV_LORA, -1)
    wv = wkv[..., MLA_NOPE:].reshape(depth, KV_LORA, MLA_WIDTH)

    g_q = jnp.pad(g_mla_q * (MLA_QK ** -0.5 * LOG2_E), ((0, 0), (0, zpad)))
    g_k = jnp.pad(g_mla_k, ((0, 0), (0, zpad)))
    g_caq = jnp.tile(g_ca_q * (CA_HEAD_DIM ** -0.5), (1, CA_HEADS))
    g_cak = jnp.tile(g_ca_k, (1, CA_HEADS))
    head = jnp.arange(CA_WIDTH) // CA_HEAD_DIM
    bd = (head[:, None] == head[None, :]).astype(BF16)

    ext = jnp.pad(rel_bias, ((0, 0), (0, 0), (CA_TQ - 1 - REL_CLIP, CA_WIN - REL_CLIP)),
                  mode="edge")
    base = ext[:, :, ::-1].reshape(depth * CA_HEADS, 1, BASE_W)

    def chunks(w):
        g = w[..., :D_FF].reshape(w.shape[:-1] + (N_FF_CHUNKS, FF_CHUNK))
        v = w[..., D_FF:].reshape(w.shape[:-1] + (N_FF_CHUNKS, FF_CHUNK))
        return jnp.concatenate([g, v], axis=-1)

    w_up_c = jnp.moveaxis(chunks(w_up), 2, 1)
    conv_c = jnp.moveaxis(chunks(conv_w), 2, 1)
    conv_c = jnp.pad(conv_c, ((0, 0), (0, 0), (0, CARRY_ROWS - CONV_W), (0, 0)))
    cb_c = chunks(conv_b)[:, :, None, :]
    return {
        "g_mix": row(g_mix), "w1": w1.astype(BF16), "g_qa": row(g_q_lora),
        "g_kva": row(g_kv_lora), "w_uq": wq.astype(BF16), "w_uk": wk.astype(BF16),
        "w_uv": wv.astype(BF16), "g_q": row(g_q), "g_k": row(g_k), "g_caq": row(g_caq),
        "g_cak": row(g_cak), "bd": bd, "base": base,
        "g_oa": row(g_out_mla), "g_ob": row(g_out_ca), "w_out": w_out.astype(BF16),
        "g_ffn": row(g_ffn), "w_up": w_up_c.astype(BF16), "conv_w": conv_c, "conv_b": cb_c,
        "w_down": w_down.reshape(depth, N_FF_CHUNKS, FF_CHUNK, D_MODEL).astype(BF16),
    }


def kernel(x, positions, g_mix, w_in, w_uq, w_ukv, g_q_lora, g_kv_lora, g_mla_q, g_mla_k,
           g_ca_q, g_ca_k, rel_bias, g_out_mla, g_out_ca, w_out, g_ffn, w_up, conv_w, conv_b,
           w_down):
    b, s, _ = x.shape
    depth = w_in.shape[0]
    assert s % TM_IN == 0 and s % TM_FFN == 0 and s % CA_TQ == 0 and s % MLA_TQ == 0
    assert CA_PAD % TM_IN == 0
    wp = _prepare(g_mix, w_in, w_uq, w_ukv, g_q_lora, g_kv_lora, g_mla_q, g_mla_k, g_ca_q,
                  g_ca_k, rel_bias, g_out_mla, g_out_ca, w_out, g_ffn, w_up, conv_w, conv_b,
                  w_down)

    inv = ROPE_BASE ** (-jnp.arange(0, MLA_ROPE, 2, dtype=F32) / MLA_ROPE)
    inv_full = jnp.zeros((1, LANES), F32)
    inv_full = inv_full.at[0, MLA_NOPE:MLA_NOPE + HALF_ROPE].set(inv)
    inv_full = inv_full.at[0, MLA_NOPE + HALF_ROPE:MLA_QK].set(inv)
    pos_f = positions.astype(F32).reshape(b * s, 1)
    cos, s_hi, s_lo = [t.reshape(b, s, LANES) for t in _rope_tables(pos_f, inv_full)]
    bias = _bias_tiles(wp["base"]).reshape(depth, CA_HEADS, CA_TQ, CA_WIN)

    for l in range(depth):
        kpad = jnp.zeros((b, s + CA_PAD, CA_WIDTH), BF16)
        vpad = jnp.zeros((b, s + CA_PAD, CA_WIDTH), BF16)
        qm, km, vm, qc, kc, vc = _inproj(l, x, cos, s_hi, s_lo, wp, kpad, vpad)
        oa = _mla_attention(qm, km, vm)
        ob = _chunk_attention(l, qc, kc, vc, bias)
        x = _outproj_ffn(l, x, oa, ob, wp)
    return x
```

```python
import jax
import jax.numpy as jnp
from jax import lax
from jax.experimental import pallas as pl
from jax.experimental.pallas import tpu as pltpu

F32 = jnp.float32
BF16 = jnp.bfloat16

D_MODEL = 1024
CHUNK = 64
EPS = 1e-6
NEG_INF = -1e30
LOG2_E = 1.4426950408889634

MLA_HEADS = 8
MLA_NOPE = 64
MLA_ROPE = 32
MLA_QK = MLA_NOPE + MLA_ROPE
MLA_V = 64
Q_LORA = 256
KV_LORA = 128
ROPE_BASE = 10000.0
MLA_WIDTH = MLA_HEADS * MLA_V

CA_HEADS = 8
CA_HEAD_DIM = 64
CA_WIDTH = CA_HEADS * CA_HEAD_DIM
LEFT_CHUNKS = 8
REL_CLIP = 128

OFF_CQ = 0
OFF_CKV = OFF_CQ + Q_LORA
OFF_KR = OFF_CKV + KV_LORA
OFF_CA = OFF_KR + MLA_ROPE

D_FF = 2816
CONV_W = 3

LANES = 128
HEAD_SLOT = LANES
HALF_ROPE = MLA_ROPE // 2
MLA_SLOTS = MLA_HEADS * HEAD_SLOT

P_CQ = 0
P_CKV = P_CQ + Q_LORA
P_KR = P_CKV + KV_LORA
P_KRS = P_KR + HEAD_SLOT
P_CAQ = P_KRS + HEAD_SLOT
P_CAK = P_CAQ + CA_WIDTH
P_CAV = P_CAK + CA_WIDTH
P_COLS = P_CAV + CA_WIDTH

BLK = 256
TM_IN = 1024
CA_PAD = LEFT_CHUNKS * CHUNK
KV_PAD = max(CA_PAD, TM_IN)
CA_TQ = BLK
CA_WIN = CA_PAD + CA_TQ
BASE_W = CA_WIN + CA_TQ
CA_VARIANTS = CA_PAD // CA_TQ + 1
CA_LOOKAHEAD = 3
MLA_TQ = BLK
MLA_HPS = 8
MLA_LOOKAHEAD = 4
MLA_PV_DELAY = 1
TM_FFN = 512
FF_CHUNK = 256
N_FF_CHUNKS = D_FF // FF_CHUNK
CARRY_ROWS = 8

VMEM_LIMIT = 56 * 1024 * 1024


def _rms(x, g):
    return x * lax.rsqrt(jnp.mean(x * x, axis=-1, keepdims=True) + EPS) * g


def _dot(a, b):
    return jnp.dot(a, b, preferred_element_type=F32)


def _dot_nt(a, b):
    return lax.dot_general(a, b, (((1,), (1,)), ((), ())), preferred_element_type=F32)


def _rope_kernel(pos_ref, inv_ref, c_ref, s_ref):
    ang = pos_ref[...] * inv_ref[...]
    s = jnp.sin(ang)
    lane = lax.broadcasted_iota(jnp.int32, ang.shape, 1)
    first_half = lane < MLA_NOPE + HALF_ROPE
    c_ref[...] = jnp.cos(ang)
    s_ref[...] = jnp.where(first_half, -s, s)


def _rope_tables(pos_f, inv_full):
    t = pos_f.shape[0]
    tm = 1024
    out = jax.ShapeDtypeStruct((t, LANES), F32)
    return pl.pallas_call(
        _rope_kernel,
        grid=(t // tm,),
        in_specs=[pl.BlockSpec((tm, 1), lambda i: (i, 0)),
                  pl.BlockSpec((1, LANES), lambda i: (0, 0))],
        out_specs=[pl.BlockSpec((tm, LANES), lambda i: (i, 0))] * 2,
        out_shape=[out] * 2,
        compiler_params=pltpu.CompilerParams(dimension_semantics=("arbitrary",)),
    )(pos_f, inv_full)


def _bias_kernel(base_ref, out_ref):
    x = jnp.broadcast_to(base_ref[...], (CA_TQ, BASE_W))
    row = lax.broadcasted_iota(jnp.int32, (CA_TQ, BASE_W), 0)
    for b in range(CA_TQ.bit_length() - 1):
        x = jnp.where(((row >> b) & 1) == 1, pltpu.roll(x, 1 << b, 1), x)
    t = x[:, CA_TQ:] * LOG2_E
    i = lax.broadcasted_iota(jnp.int32, (CA_TQ, CA_WIN), 0) // CHUNK
    j = lax.broadcasted_iota(jnp.int32, (CA_TQ, CA_WIN), 1) // CHUNK
    key = lax.broadcasted_iota(jnp.int32, (CA_TQ, CA_WIN), 1)
    real = key >= CA_PAD - pl.program_id(1) * CA_TQ
    valid = (j >= i) & (j <= i + LEFT_CHUNKS) & real
    out_ref[...] = jnp.where(valid, t, NEG_INF).T


def _bias_tiles(base, depth):
    return pl.pallas_call(
        _bias_kernel,
        grid=(depth * CA_HEADS, CA_VARIANTS),
        in_specs=[pl.BlockSpec((None, 1, BASE_W), lambda i, v: (i, 0, 0))],
        out_specs=pl.BlockSpec((None, None, None, CA_WIN, CA_TQ),
                               lambda i, v: (i // CA_HEADS, v, i % CA_HEADS, 0, 0)),
        out_shape=jax.ShapeDtypeStruct((depth, CA_VARIANTS, CA_HEADS, CA_WIN, CA_TQ), F32),
        compiler_params=pltpu.CompilerParams(dimension_semantics=("arbitrary", "arbitrary")),
    )(base)


def _inproj_kernel(x_ref, c_ref, s_ref, gmix_ref, w1_ref, gqa_ref, gkva_ref, wuq_ref, wuk_ref,
                   wuv_ref, gq_ref, gqs_ref, gk_ref, gks_ref, gcaq_ref, gcak_ref, bd_ref,
                   kpad_ref, vpad_ref,
                   qm_ref, km_ref, vt_ref, qc_ref, kc_ref, vct_ref):
    del kpad_ref, vpad_ref
    n_sub = TM_IN // BLK

    def project(r):
        rows = slice(r * BLK, (r + 1) * BLK)
        h = _rms(x_ref[rows, :], gmix_ref[...]).astype(BF16)
        return _dot(h, w1_ref[...])

    def up_project(proj):
        cqn = _rms(proj[:, P_CQ:P_CKV], gqa_ref[...]).astype(BF16)
        ckvn = _rms(proj[:, P_CKV:P_KR], gkva_ref[...]).astype(BF16)
        caq = proj[:, P_CAQ:P_CAK]
        cak = proj[:, P_CAK:P_CAV]
        return {
            "q2": _dot(cqn, wuq_ref[...]),
            "kn": _dot(ckvn, wuk_ref[...]),
            "v": _dot(ckvn, wuv_ref[...]),
            "ssq": _dot((caq * caq).astype(BF16), bd_ref[...]),
            "ssk": _dot((cak * cak).astype(BF16), bd_ref[...]),
        }

    def transpose_heads(t, ref, r):
        for p in range(t.shape[1] // LANES):
            tt = t[:, p * LANES:(p + 1) * LANES].T.astype(BF16)
            ref[2 * p, r] = tt[:LANES // 2]
            ref[2 * p + 1, r] = tt[LANES // 2:]

    def finish(r, proj, up):
        rows = slice(r * BLK, (r + 1) * BLK)
        cos = c_ref[rows, :]
        sin = s_ref[rows, :]
        gq_c, gq_s = gq_ref[...] * cos, gqs_ref[...] * sin
        gk_c, gk_s = gk_ref[...] * cos, gks_ref[...] * sin
        kr = proj[:, P_KR:P_KRS]
        kr_s = proj[:, P_KRS:P_CAQ] * gk_s
        for hd in range(MLA_HEADS):
            sl = slice(hd * HEAD_SLOT, (hd + 1) * HEAD_SLOT)
            ssl = slice(MLA_SLOTS + hd * HEAD_SLOT, MLA_SLOTS + (hd + 1) * HEAD_SLOT)
            q = up["q2"][:, sl]
            rq = lax.rsqrt(jnp.sum(q * q, axis=-1, keepdims=True) * (1.0 / MLA_QK) + EPS)
            qm_ref[rows, sl] = ((q * gq_c + up["q2"][:, ssl] * gq_s) * rq).astype(BF16)
            k = up["kn"][:, sl] + kr
            rk = lax.rsqrt(jnp.sum(k * k, axis=-1, keepdims=True) * (1.0 / MLA_QK) + EPS)
            km_ref[rows, sl] = ((k * gk_c + kr_s) * rk).astype(BF16)
        transpose_heads(up["v"], vt_ref, r)
        caq = proj[:, P_CAQ:P_CAK]
        cak = proj[:, P_CAK:P_CAV]
        qc_ref[rows, :] = (caq * lax.rsqrt(up["ssq"] * (1.0 / CA_HEAD_DIM) + EPS)
                           * gcaq_ref[...]).astype(BF16)
        kc_ref[rows, :] = (cak * lax.rsqrt(up["ssk"] * (1.0 / CA_HEAD_DIM) + EPS)
                           * gcak_ref[...]).astype(BF16)
        transpose_heads(proj[:, P_CAV:P_COLS], vct_ref, r)

    proj = project(0)
    for r in range(n_sub):
        nxt = project(r + 1) if r + 1 < n_sub else None
        finish(r, proj, up_project(proj))
        proj = nxt


def _inproj(l, x, cos, sin, wp, kpad, vpad):
    b, s, _ = x.shape
    n_sub = TM_IN // BLK
    pad_tiles = KV_PAD // TM_IN

    def tok(width):
        return pl.BlockSpec((None, TM_IN, width), lambda bi, i: (bi, i, 0))

    def lw(shape):
        return pl.BlockSpec((None,) + shape, lambda bi, i: (l,) + (0,) * len(shape),
                            pipeline_mode=pl.Buffered(1))

    const = pl.BlockSpec((CA_WIDTH, CA_WIDTH), lambda bi, i: (0, 0),
                         pipeline_mode=pl.Buffered(1))
    any_spec = pl.BlockSpec(memory_space=pl.ANY)
    heads_t = lambda off: pl.BlockSpec((None, MLA_HEADS, n_sub, LANES // 2, BLK),
                                       lambda bi, i: (bi, 0, i + off, 0, 0))
    tok_shape = lambda w: jax.ShapeDtypeStruct((b, s, w), BF16)
    return pl.pallas_call(
        _inproj_kernel,
        grid=(b, s // TM_IN),
        in_specs=[tok(D_MODEL), tok(LANES), tok(LANES),
                  lw((1, D_MODEL)), lw((D_MODEL, P_COLS)), lw((1, Q_LORA)), lw((1, KV_LORA)),
                  lw((Q_LORA, 2 * MLA_SLOTS)), lw((KV_LORA, MLA_SLOTS)),
                  lw((KV_LORA, MLA_WIDTH)), lw((1, HEAD_SLOT)), lw((1, HEAD_SLOT)),
                  lw((1, HEAD_SLOT)), lw((1, HEAD_SLOT)),
                  lw((1, CA_WIDTH)), lw((1, CA_WIDTH)), const, any_spec, any_spec],
        out_specs=[tok(MLA_SLOTS), tok(MLA_SLOTS), heads_t(0), tok(CA_WIDTH),
                   pl.BlockSpec((None, TM_IN, CA_WIDTH), lambda bi, i: (bi, i + pad_tiles, 0)),
                   heads_t(pad_tiles)],
        out_shape=[tok_shape(MLA_SLOTS), tok_shape(MLA_SLOTS),
                   jax.ShapeDtypeStruct((b, MLA_HEADS, s // BLK, MLA_V, BLK), BF16),
                   tok_shape(CA_WIDTH), jax.ShapeDtypeStruct(kpad.shape, kpad.dtype),
                   jax.ShapeDtypeStruct(vpad.shape, vpad.dtype)],
        input_output_aliases={17: 4, 18: 5},
        compiler_params=pltpu.CompilerParams(
            dimension_semantics=("arbitrary", "arbitrary"), vmem_limit_bytes=VMEM_LIMIT),
    )(x, cos, sin, wp["g_mix"], wp["w1"], wp["g_qa"], wp["g_kva"], wp["w_uq"], wp["w_uk"],
      wp["w_uv"], wp["g_q"], wp["g_qs"], wp["g_k"], wp["g_ks"], wp["g_caq"], wp["g_cak"],
      wp["bd"], kpad, vpad)


def _mla_kernel(q_ref, k_ref, vt_ref, o_ref, m_ref, acc_ref, s_ref):
    i = pl.program_id(2)
    t = MLA_TQ
    ones = jnp.ones((MLA_V, t), BF16)
    m_ref[...] = jnp.full(m_ref.shape, NEG_INF, F32)
    acc_ref[...] = jnp.zeros(acc_ref.shape, F32)

    def scores(j, hh):
        rows = pl.ds(pl.multiple_of(j * t, t), t)
        sl = slice(hh * HEAD_SLOT, (hh + 1) * HEAD_SLOT)
        return _dot_nt(k_ref[rows, sl], q_ref[:, sl])

    def step(j, last):
        def accumulate(hh, p, alpha):
            v1 = jnp.concatenate([vt_ref[hh, j], ones], axis=0)
            acc_ref[hh] = acc_ref[hh] * alpha + _dot(v1, p)

        pending = {hh: s_ref[hh] for hh in range(MLA_LOOKAHEAD)}
        done = {}
        for hh in range(MLA_HPS):
            st = pending.pop(hh)
            ahead = hh + MLA_LOOKAHEAD
            if ahead < MLA_HPS:
                pending[ahead] = scores(j, ahead)
            elif not last:
                s_ref[ahead - MLA_HPS] = scores(j + 1, ahead - MLA_HPS)
            if last:
                kc = lax.broadcasted_iota(jnp.int32, st.shape, 0) // CHUNK
                qc = lax.broadcasted_iota(jnp.int32, st.shape, 1) // CHUNK
                st = jnp.where(kc <= qc, st, NEG_INF)
            m_prev = m_ref[hh]
            m_new = jnp.maximum(m_prev, jnp.max(st, axis=0, keepdims=True))
            m_ref[hh] = m_new
            done[hh] = (jnp.exp2(st - m_new).astype(BF16), jnp.exp2(m_prev - m_new))
            if hh >= MLA_PV_DELAY:
                accumulate(hh - MLA_PV_DELAY, *done.pop(hh - MLA_PV_DELAY))
        for hh in sorted(done):
            accumulate(hh, *done[hh])

    for hh in range(MLA_LOOKAHEAD):
        s_ref[hh] = scores(0, hh)

    def body(j, carry):
        step(j, False)
        return carry

    lax.fori_loop(0, i, body, 0)
    step(i, True)
    for pr in range(MLA_HPS // 2):
        ot = jnp.concatenate([acc_ref[hh, :MLA_V, :] / acc_ref[hh, MLA_V:, :]
                              for hh in (2 * pr, 2 * pr + 1)], axis=0)
        o_ref[:, pr * LANES:(pr + 1) * LANES] = ot.T.astype(BF16)


def _mla_attention(qm, km, vt):
    b, s, _ = qm.shape
    groups = MLA_HEADS // MLA_HPS
    nk = s // MLA_TQ
    return pl.pallas_call(
        _mla_kernel,
        grid=(b, groups, s // MLA_TQ),
        in_specs=[pl.BlockSpec((None, MLA_TQ, MLA_HPS * HEAD_SLOT), lambda bi, g, i: (bi, i, g)),
                  pl.BlockSpec((None, s, MLA_HPS * HEAD_SLOT), lambda bi, g, i: (bi, 0, g)),
                  pl.BlockSpec((None, MLA_HPS, nk, MLA_V, MLA_TQ),
                               lambda bi, g, i: (bi, g, 0, 0, 0))],
        out_specs=pl.BlockSpec((None, MLA_TQ, MLA_HPS * MLA_V), lambda bi, g, i: (bi, i, g)),
        out_shape=jax.ShapeDtypeStruct((b, s, MLA_WIDTH), BF16),
        scratch_shapes=[pltpu.VMEM((MLA_HPS, 1, MLA_TQ), F32),
                        pltpu.VMEM((MLA_HPS, LANES, MLA_TQ), F32),
                        pltpu.VMEM((MLA_LOOKAHEAD, MLA_TQ, MLA_TQ), F32)],
        compiler_params=pltpu.CompilerParams(
            dimension_semantics=("arbitrary", "arbitrary", "arbitrary"),
            vmem_limit_bytes=VMEM_LIMIT),
    )(qm, km, vt)


def _ca_kernel(q_ref, k_ref, vt_ref, bias_ref, o_ref):
    qi = pl.program_id(1)
    lead = KV_PAD - CA_PAD
    rows = pl.ds(pl.multiple_of(lead + qi * CA_TQ, BLK), CA_WIN)
    blk0 = lead // BLK + qi
    low =lax.broadcasted_iota(jnp.int32, (1, LANES), 1) < CA_HEAD_DIM
    ones = jnp.ones((CA_HEAD_DIM, CA_WIN), BF16)

    def scores(h):
        sl = slice((h // 2) * LANES, (h // 2 + 1) * LANES)
        q = q_ref[:, sl]
        qh = jnp.where(low if h % 2 == 0 else ~low, q, jnp.zeros_like(q))
        return _dot_nt(k_ref[rows, sl], qh)

    def attend(h, p):
        vt = jnp.concatenate([vt_ref[h, blk0 + kb] for kb in range(CA_WIN // BLK)], axis=1)
        acc = _dot(jnp.concatenate([vt, ones], axis=0), p)
        return acc[:CA_HEAD_DIM] / acc[CA_HEAD_DIM:]

    pending = {h: scores(h) for h in range(CA_LOOKAHEAD)}
    probs = {}
    outs = {}

    def flush(h):
        outs[h] = attend(h, probs.pop(h))
        if h % 2 == 1:
            ot = jnp.concatenate([outs.pop(h - 1), outs.pop(h)], axis=0)
            o_ref[:, (h // 2) * LANES:(h // 2 + 1) * LANES] = ot.T.astype(BF16)

    for h in range(CA_HEADS):
        st = pending.pop(h)
        if h + CA_LOOKAHEAD < CA_HEADS:
            pending[h + CA_LOOKAHEAD] = scores(h + CA_LOOKAHEAD)
        st = st + bias_ref[h]
        probs[h] = jnp.exp2(st - jnp.max(st, axis=0, keepdims=True)).astype(BF16)
        if h >= 1:
            flush(h - 1)
    flush(CA_HEADS - 1)


def _chunk_attention(l, qc, kc, vct, bias):
    b, s, _ = qc.shape
    nblk = vct.shape[2]
    return pl.pallas_call(
        _ca_kernel,
        grid=(b, s // CA_TQ),
        in_specs=[pl.BlockSpec((None, CA_TQ, CA_WIDTH), lambda bi, i: (bi, i, 0)),
                  pl.BlockSpec((None, s + KV_PAD, CA_WIDTH), lambda bi, i: (bi, 0, 0)),
                  pl.BlockSpec((None, CA_HEADS, nblk, CA_HEAD_DIM, BLK),
                               lambda bi, i: (bi, 0, 0, 0, 0)),
                  pl.BlockSpec((None, None, CA_HEADS, CA_WIN, CA_TQ),
                               lambda bi, i: (l, jnp.minimum(i, CA_VARIANTS - 1), 0, 0, 0))],
        out_specs=pl.BlockSpec((None, CA_TQ, CA_WIDTH), lambda bi, i: (bi, i, 0)),
        out_shape=jax.ShapeDtypeStruct((b, s, CA_WIDTH), BF16),
        compiler_params=pltpu.CompilerParams(
            dimension_semantics=("arbitrary", "arbitrary"), vmem_limit_bytes=VMEM_LIMIT),
    )(qc, kc, vct, bias)


def _ffn_kernel(x_ref, oa_ref, ob_ref, goa_ref, gob_ref, wout_ref, gffn_ref, wup_ref, cw_ref,
                cb_ref, wdown_ref, out_ref, carry_ref, u_ref, h_ref):
    tm = TM_FFN

    @pl.when(pl.program_id(1) == 0)
    def _():
        carry_ref[...] = jnp.zeros(carry_ref.shape, F32)

    na = _rms(oa_ref[...].astype(F32), goa_ref[...]).astype(BF16)
    nb = _rms(ob_ref[...].astype(F32), gob_ref[...]).astype(BF16)
    x1 = (x_ref[...] + _dot(na, wout_ref[0:MLA_WIDTH, :])
          + _dot(nb, wout_ref[MLA_WIDTH:MLA_WIDTH + CA_WIDTH, :]))
    out_ref[...] = x1
    h_ref[...] = _rms(x1, gffn_ref[...]).astype(BF16)

    def up(c, slot):
        u = _dot(h_ref[...], wup_ref[c])
        u_ref[slot, 0:CARRY_ROWS, :] = carry_ref[c]
        u_ref[slot, CARRY_ROWS:CARRY_ROWS + tm, :] = u
        carry_ref[c] = u[tm - CARRY_ROWS:, :]

    def act_down(c, slot):
        cw = cw_ref[c]
        y = cb_ref[c]
        for tap in range(CONV_W):
            off = CARRY_ROWS - (CONV_W - 1) + tap
            y = y + u_ref[slot, off:off + tm, :] * cw[tap:tap + 1, :]
        g = y[:, :FF_CHUNK]
        a = g * (1.0 / (1.0 + jnp.exp(-g))) * y[:, FF_CHUNK:]
        out_ref[...] += _dot(a.astype(BF16), wdown_ref[c])

    up(0, 0)

    def body(k, carry):
        c = 2 * k
        up(c + 1, 1)
        act_down(c, 0)
        up(c + 2, 0)
        act_down(c + 1, 1)
        return carry

    lax.fori_loop(0, (N_FF_CHUNKS - 1) // 2, body, 0)
    act_down(N_FF_CHUNKS - 1, 0)


def _outproj_ffn(l, x, oa, ob, wp):
    b, s, _ = x.shape

    def tok(width):
        return pl.BlockSpec((None, TM_FFN, width), lambda bi, i: (bi, i, 0))

    def lw(shape):
        return pl.BlockSpec((None,) + shape, lambda bi, i: (l,) + (0,) * len(shape),
                            pipeline_mode=pl.Buffered(1))

    return pl.pallas_call(
        _ffn_kernel,
        grid=(b, s // TM_FFN),
        in_specs=[tok(D_MODEL), tok(MLA_WIDTH), tok(CA_WIDTH),
                  lw((1, MLA_WIDTH)), lw((1, CA_WIDTH)), lw((D_MODEL, D_MODEL)),
                  lw((1, D_MODEL)), lw((N_FF_CHUNKS, D_MODEL, 2 * FF_CHUNK)),
                  lw((N_FF_CHUNKS, CARRY_ROWS, 2 * FF_CHUNK)),
                  lw((N_FF_CHUNKS, 1, 2 * FF_CHUNK)),
                  lw((N_FF_CHUNKS, FF_CHUNK, D_MODEL))],
        out_specs=tok(D_MODEL),
        out_shape=jax.ShapeDtypeStruct((b, s, D_MODEL), F32),
        scratch_shapes=[pltpu.VMEM((N_FF_CHUNKS, CARRY_ROWS, 2 * FF_CHUNK), F32),
                        pltpu.VMEM((2, CARRY_ROWS + TM_FFN, 2 * FF_CHUNK), F32),
                        pltpu.VMEM((TM_FFN, D_MODEL), BF16)],
        compiler_params=pltpu.CompilerParams(
            dimension_semantics=("arbitrary", "arbitrary"), vmem_limit_bytes=VMEM_LIMIT),
    )(x, oa, ob, wp["g_oa"], wp["g_ob"], wp["w_out"], wp["g_ffn"], wp["w_up"], wp["conv_w"],
      wp["conv_b"], wp["w_down"])


def _rope_slot(first, second, axis_pad):
    t = jnp.concatenate([first, second], axis=-1)
    return jnp.pad(t, axis_pad + ((MLA_NOPE, HEAD_SLOT - MLA_QK),))


def _prepare(g_mix, w_in, w_uq, w_ukv, g_q_lora, g_kv_lora, g_mla_q, g_mla_k, g_ca_q, g_ca_k,
             rel_bias, g_out_mla, g_out_ca, w_out, g_ffn, w_up, conv_w, conv_b, w_down):
    depth = w_in.shape[0]
    row = lambda g: g[:, None, :]
    zpad = HEAD_SLOT - MLA_QK
    r1 = slice(MLA_NOPE, MLA_NOPE + HALF_ROPE)
    r2 = slice(MLA_NOPE + HALF_ROPE, MLA_QK)

    w_kr = w_in[:, :, OFF_KR:OFF_CA]
    kr_slot = _rope_slot(w_kr[..., :HALF_ROPE], w_kr[..., HALF_ROPE:], ((0, 0), (0, 0)))
    kr_swap = _rope_slot(w_kr[..., HALF_ROPE:], w_kr[..., :HALF_ROPE], ((0, 0), (0, 0)))
    w1 = jnp.concatenate([w_in[:, :, OFF_CQ:OFF_KR], kr_slot, kr_swap, w_in[:, :, OFF_CA:]],
                         axis=-1)

    wq = w_uq.reshape(depth, Q_LORA, MLA_HEADS, MLA_QK)
    wq_slot = jnp.pad(wq, ((0, 0), (0, 0), (0, 0), (0, zpad))).reshape(depth, Q_LORA, -1)
    wq_swap = _rope_slot(wq[..., r2], wq[..., r1], ((0, 0), (0, 0), (0, 0)))
    wq2 = jnp.concatenate([wq_slot, wq_swap.reshape(depth, Q_LORA, -1)], axis=-1)
    wkv = w_ukv.reshape(depth, KV_LORA, MLA_HEADS, MLA_NOPE + MLA_V)
    wk = jnp.pad(wkv[..., :MLA_NOPE], ((0, 0), (0, 0), (0, 0), (0, HEAD_SLOT - MLA_NOPE)))
    wk = wk.reshape(depth, KV_LORA, -1)
    wv = wkv[..., MLA_NOPE:].reshape(depth, KV_LORA, MLA_WIDTH)

    gq = g_mla_q * (MLA_QK ** -0.5 * LOG2_E)
    g_q = jnp.pad(gq, ((0, 0), (0, zpad)))
    g_qs = _rope_slot(gq[:, r2], gq[:, r1], ((0, 0),))
    g_k = jnp.pad(g_mla_k, ((0, 0), (0, zpad)))
    g_ks = _rope_slot(g_mla_k[:, r2], g_mla_k[:, r1], ((0, 0),))
    g_caq = jnp.tile(g_ca_q * (CA_HEAD_DIM ** -0.5 * LOG2_E), (1, CA_HEADS))
    g_cak = jnp.tile(g_ca_k, (1, CA_HEADS))
    head = jnp.arange(CA_WIDTH) // CA_HEAD_DIM
    bd = (head[:, None] == head[None, :]).astype(BF16)

    ext = jnp.pad(rel_bias, ((0, 0), (0, 0), (CA_TQ - 1 - REL_CLIP, CA_WIN - REL_CLIP)),
                  mode="edge")
    base = ext[:, :, ::-1].reshape(depth * CA_HEADS, 1, BASE_W)

    def chunks(w):
        g = w[..., :D_FF].reshape(w.shape[:-1] + (N_FF_CHUNKS, FF_CHUNK))
        v = w[..., D_FF:].reshape(w.shape[:-1] + (N_FF_CHUNKS, FF_CHUNK))
        return jnp.concatenate([g, v], axis=-1)

    w_up_c = jnp.moveaxis(chunks(w_up), 2, 1)
    conv_c = jnp.moveaxis(chunks(conv_w), 2, 1)
    conv_c = jnp.pad(conv_c, ((0, 0), (0, 0), (0, CARRY_ROWS - CONV_W), (0, 0)))
    cb_c = chunks(conv_b)[:, :, None, :]
    return {
        "g_mix": row(g_mix), "w1": w1.astype(BF16), "g_qa": row(g_q_lora),
        "g_kva": row(g_kv_lora), "w_uq": wq2.astype(BF16), "w_uk": wk.astype(BF16),
        "w_uv": wv.astype(BF16), "g_q": row(g_q), "g_qs": row(g_qs), "g_k": row(g_k),
        "g_ks": row(g_ks), "g_caq": row(g_caq), "g_cak": row(g_cak), "bd": bd, "base": base,
        "g_oa": row(g_out_mla), "g_ob": row(g_out_ca), "w_out": w_out.astype(BF16),
        "g_ffn": row(g_ffn), "w_up": w_up_c.astype(BF16), "conv_w": conv_c, "conv_b": cb_c,
        "w_down": w_down.reshape(depth, N_FF_CHUNKS, FF_CHUNK, D_MODEL).astype(BF16),
    }


def kernel(x, positions, g_mix, w_in, w_uq, w_ukv, g_q_lora, g_kv_lora, g_mla_q, g_mla_k,
           g_ca_q, g_ca_k, rel_bias, g_out_mla, g_out_ca, w_out, g_ffn, w_up, conv_w, conv_b,
           w_down):
    b, s, _ = x.shape
    depth = w_in.shape[0]
    assert s % TM_IN == 0 and s % TM_FFN == 0 and KV_PAD % TM_IN == 0
    assert (KV_PAD - CA_PAD) % BLK == 0
    wp = _prepare(g_mix, w_in, w_uq, w_ukv, g_q_lora, g_kv_lora, g_mla_q, g_mla_k, g_ca_q,
                  g_ca_k, rel_bias, g_out_mla, g_out_ca, w_out, g_ffn, w_up, conv_w, conv_b,
                  w_down)

    inv = ROPE_BASE ** (-jnp.arange(0, MLA_ROPE, 2, dtype=F32) / MLA_ROPE)
    inv_full = jnp.zeros((1, LANES), F32)
    inv_full = inv_full.at[0, MLA_NOPE:MLA_NOPE + HALF_ROPE].set(inv)
    inv_full = inv_full.at[0, MLA_NOPE + HALF_ROPE:MLA_QK].set(inv)
    pos_f = positions.astype(F32).reshape(b * s, 1)
    cos, sin = [t.reshape(b, s, LANES) for t in _rope_tables(pos_f, inv_full)]
    bias = _bias_tiles(wp["base"], depth)

    for l in range(depth):
        kpad = jnp.zeros((b, s + KV_PAD, CA_WIDTH), BF16)
        vpad = jnp.zeros((b, CA_HEADS, (s + KV_PAD) // BLK, CA_HEAD_DIM, BLK), BF16)
        qm, km, vt, qc, kc, vct = _inproj(l, x, cos, sin, wp, kpad, vpad)
        oa = _mla_attention(qm, km, vt)
        ob = _chunk_attention(l, qc, kc, vct, bias)
        x = _outproj_ffn(l, x, oa, ob, wp)
    return x
```

```python
import jax
import jax.numpy as jnp
from jax import lax
from jax.experimental import pallas as pl
from jax.experimental.pallas import tpu as pltpu

F32 = jnp.float32
BF16 = jnp.bfloat16

D_MODEL = 1024
CHUNK = 64
EPS = 1e-6
NEG_INF = -1e30
LOG2_E = 1.4426950408889634

MLA_HEADS = 8
MLA_NOPE = 64
MLA_ROPE = 32
MLA_QK = MLA_NOPE + MLA_ROPE
MLA_V = 64
Q_LORA = 256
KV_LORA = 128
ROPE_BASE = 10000.0
MLA_WIDTH = MLA_HEADS * MLA_V

CA_HEADS = 8
CA_HEAD_DIM = 64
CA_WIDTH = CA_HEADS * CA_HEAD_DIM
LEFT_CHUNKS = 8
REL_CLIP = 128

OFF_CQ = 0
OFF_CKV = OFF_CQ + Q_LORA
OFF_KR = OFF_CKV + KV_LORA
OFF_CA = OFF_KR + MLA_ROPE

D_FF = 2816
CONV_W = 3

LANES = 128
HEAD_SLOT = LANES
HALF_ROPE = MLA_ROPE // 2
MLA_SLOTS = MLA_HEADS * HEAD_SLOT

P_CQ = 0
P_CKV = P_CQ + Q_LORA
P_KR = P_CKV + KV_LORA
P_KRS = P_KR + HEAD_SLOT
P_CAQ = P_KRS + HEAD_SLOT
P_CAK = P_CAQ + CA_WIDTH
P_CAV = P_CAK + CA_WIDTH
P_COLS = P_CAV + CA_WIDTH

SUM_ROWS = 16
BLK = 256
TM_IN = 1024
CA_PAD = LEFT_CHUNKS * CHUNK
KV_PAD = max(CA_PAD, TM_IN)
CA_TQ = BLK
CA_WIN = CA_PAD + CA_TQ
BASE_W = CA_WIN + CA_TQ
CA_VARIANTS = CA_PAD // CA_TQ + 1
CA_LOOKAHEAD = 3
CA_QBLOCKS = 2
MLA_TQ = BLK
MLA_HPS = 8
MLA_LOOKAHEAD = 4
MLA_PV_DELAY = 1
TM_FFN = 512
FF_CHUNK = 256
N_FF_CHUNKS = D_FF // FF_CHUNK
CARRY_ROWS = 8

VMEM_LIMIT = 56 * 1024 * 1024


def _rms(x, g):
    return x * lax.rsqrt(jnp.mean(x * x, axis=-1, keepdims=True) + EPS) * g


def _dot(a, b):
    return jnp.dot(a, b, preferred_element_type=F32)


def _dot_nt(a, b):
    return lax.dot_general(a, b, (((1,), (1,)), ((), ())), preferred_element_type=F32)


def _rope_kernel(pos_ref, inv_ref, c_ref, s_ref):
    ang = pos_ref[...] * inv_ref[...]
    s = jnp.sin(ang)
    lane = lax.broadcasted_iota(jnp.int32, ang.shape, 1)
    first_half = lane < MLA_NOPE + HALF_ROPE
    c_ref[...] = jnp.cos(ang)
    s_ref[...] = jnp.where(first_half, -s, s)


def _rope_tables(pos_f, inv_full):
    t = pos_f.shape[0]
    tm = 1024
    out = jax.ShapeDtypeStruct((t, LANES), F32)
    return pl.pallas_call(
        _rope_kernel,
        grid=(t // tm,),
        in_specs=[pl.BlockSpec((tm, 1), lambda i: (i, 0)),
                  pl.BlockSpec((1, LANES), lambda i: (0, 0))],
        out_specs=[pl.BlockSpec((tm, LANES), lambda i: (i, 0))] * 2,
        out_shape=[out] * 2,
        compiler_params=pltpu.CompilerParams(dimension_semantics=("arbitrary",)),
    )(pos_f, inv_full)


def _bias_kernel(base_ref, out_ref, tile_ref):
    @pl.when(pl.program_id(1) == 0)
    def _():
        x = jnp.broadcast_to(base_ref[...], (CA_TQ, BASE_W))
        row = lax.broadcasted_iota(jnp.int32, (CA_TQ, BASE_W), 0)
        for b in range(CA_TQ.bit_length() - 1):
            x = jnp.where(((row >> b) & 1) == 1, pltpu.roll(x, 1 << b, 1), x)
        t = x[:, CA_TQ:] * LOG2_E
        i = lax.broadcasted_iota(jnp.int32, (CA_TQ, CA_WIN), 0) // CHUNK
        j = lax.broadcasted_iota(jnp.int32, (CA_TQ, CA_WIN), 1) // CHUNK
        band = (j >= i) & (j <= i + LEFT_CHUNKS)
        tile_ref[...] = jnp.where(band, t, NEG_INF).T

    key = lax.broadcasted_iota(jnp.int32, (CA_WIN, CA_TQ), 0)
    real = key >= CA_PAD - pl.program_id(1) * CA_TQ
    out_ref[...] = jnp.where(real, tile_ref[...], NEG_INF)


def _bias_tiles(base, depth):
    return pl.pallas_call(
        _bias_kernel,
        grid=(depth * CA_HEADS, CA_VARIANTS),
        in_specs=[pl.BlockSpec((None, 1, BASE_W), lambda i, v: (i, 0, 0))],
        out_specs=pl.BlockSpec((None, None, None, CA_WIN, CA_TQ),
                               lambda i, v: (i // CA_HEADS, v, i % CA_HEADS, 0, 0)),
        out_shape=jax.ShapeDtypeStruct((depth, CA_VARIANTS, CA_HEADS, CA_WIN, CA_TQ), F32),
        scratch_shapes=[pltpu.VMEM((CA_WIN, CA_TQ), F32)],
        compiler_params=pltpu.CompilerParams(dimension_semantics=("arbitrary", "arbitrary")),
    )(base)


def _inproj_kernel(x_ref, c_ref, s_ref, gmix_ref, w1_ref, gqa_ref, gkva_ref, wuq_ref, wuk_ref,
                   wuv_ref, gq_ref, gqs_ref, gk_ref, gks_ref, gcaq_ref, gcak_ref, bd_ref,
                   kpad_ref, vpad_ref,
                   qm_ref, km_ref, vt_ref, qc_ref, kc_ref, vct_ref):
    del kpad_ref, vpad_ref
    n_sub = TM_IN // BLK

    def project(r):
        rows = slice(r * BLK, (r + 1) * BLK)
        h = _rms(x_ref[rows, :], gmix_ref[...]).astype(BF16)
        return _dot(h, w1_ref[...])

    def up_project(proj):
        cqn = _rms(proj[:, P_CQ:P_CKV], gqa_ref[...]).astype(BF16)
        ckvn = _rms(proj[:, P_CKV:P_KR], gkva_ref[...]).astype(BF16)
        caq = proj[:, P_CAQ:P_CAK]
        cak = proj[:, P_CAK:P_CAV]
        return {
            "q2": _dot(cqn, wuq_ref[...]),
            "kn": _dot(ckvn, wuk_ref[...]),
            "v": _dot(ckvn, wuv_ref[...]),
            "ssq": _dot((caq * caq).astype(BF16), bd_ref[...]),
            "ssk": _dot((cak * cak).astype(BF16), bd_ref[...]),
        }

    def transpose_heads(t, ref, r):
        for p in range(t.shape[1] // LANES):
            tt = t[:, p * LANES:(p + 1) * LANES].T.astype(BF16)
            ref[2 * p, r] = tt[:LANES // 2]
            ref[2 * p + 1, r] = tt[LANES // 2:]

    def finish(r, proj, up):
        rows = slice(r * BLK, (r + 1) * BLK)
        cos = c_ref[rows, :]
        sin = s_ref[rows, :]
        gq_c, gq_s = gq_ref[...] * cos, gqs_ref[...] * sin
        gk_c, gk_s = gk_ref[...] * cos, gks_ref[...] * sin
        kr = proj[:, P_KR:P_KRS]
        kr_s = proj[:, P_KRS:P_CAQ] * gk_s
        for hd in range(MLA_HEADS):
            sl = slice(hd * HEAD_SLOT, (hd + 1) * HEAD_SLOT)
            ssl = slice(MLA_SLOTS + hd * HEAD_SLOT, MLA_SLOTS + (hd + 1) * HEAD_SLOT)
            q = up["q2"][:, sl]
            rq = lax.rsqrt(jnp.sum(q * q, axis=-1, keepdims=True) * (1.0 / MLA_QK) + EPS)
            qm_ref[rows, sl] = ((q * gq_c + up["q2"][:, ssl] * gq_s) * rq).astype(BF16)
            k = up["kn"][:, sl] + kr
            rk = lax.rsqrt(jnp.sum(k * k, axis=-1, keepdims=True) * (1.0 / MLA_QK) + EPS)
            km_ref[rows, sl] = ((k * gk_c + kr_s) * rk).astype(BF16)
        transpose_heads(up["v"], vt_ref, r)
        caq = proj[:, P_CAQ:P_CAK]
        cak = proj[:, P_CAK:P_CAV]
        qc_ref[rows, :] = (caq * lax.rsqrt(up["ssq"] * (1.0 / CA_HEAD_DIM) + EPS)
                           * gcaq_ref[...]).astype(BF16)
        kc_ref[rows, :] = (cak * lax.rsqrt(up["ssk"] * (1.0 / CA_HEAD_DIM) + EPS)
                           * gcak_ref[...]).astype(BF16)
        transpose_heads(proj[:, P_CAV:P_COLS], vct_ref, r)

    proj = project(0)
    for r in range(n_sub):
        nxt = project(r + 1) if r + 1 < n_sub else None
        finish(r, proj, up_project(proj))
        proj = nxt


def _inproj(l, x, cos, sin, wp, kpad, vpad):
    b, s, _ = x.shape
    n_sub = TM_IN // BLK
    pad_tiles = KV_PAD // TM_IN

    def tok(width):
        return pl.BlockSpec((None, TM_IN, width), lambda bi, i: (bi, i, 0))

    def lw(shape):
        return pl.BlockSpec((None,) + shape, lambda bi, i: (l,) + (0,) * len(shape),
                            pipeline_mode=pl.Buffered(1))

    const = pl.BlockSpec((CA_WIDTH, CA_WIDTH), lambda bi, i: (0, 0),
                         pipeline_mode=pl.Buffered(1))
    any_spec = pl.BlockSpec(memory_space=pl.ANY)
    heads_t = lambda off: pl.BlockSpec((None, MLA_HEADS, n_sub, LANES // 2, BLK),
                                       lambda bi, i: (bi, 0, i + off, 0, 0))
    tok_shape = lambda w: jax.ShapeDtypeStruct((b, s, w), BF16)
    return pl.pallas_call(
        _inproj_kernel,
        grid=(b, s // TM_IN),
        in_specs=[tok(D_MODEL), tok(LANES), tok(LANES),
                  lw((1, D_MODEL)), lw((D_MODEL, P_COLS)), lw((1, Q_LORA)), lw((1, KV_LORA)),
                  lw((Q_LORA, 2 * MLA_SLOTS)), lw((KV_LORA, MLA_SLOTS)),
                  lw((KV_LORA, MLA_WIDTH)), lw((1, HEAD_SLOT)), lw((1, HEAD_SLOT)),
                  lw((1, HEAD_SLOT)), lw((1, HEAD_SLOT)),
                  lw((1, CA_WIDTH)), lw((1, CA_WIDTH)), const, any_spec, any_spec],
        out_specs=[tok(MLA_SLOTS), tok(MLA_SLOTS), heads_t(0), tok(CA_WIDTH),
                   pl.BlockSpec((None, TM_IN, CA_WIDTH), lambda bi, i: (bi, i + pad_tiles, 0)),
                   heads_t(pad_tiles)],
        out_shape=[tok_shape(MLA_SLOTS), tok_shape(MLA_SLOTS),
                   jax.ShapeDtypeStruct((b, MLA_HEADS, s // BLK, MLA_V, BLK), BF16),
                   tok_shape(CA_WIDTH), jax.ShapeDtypeStruct(kpad.shape, kpad.dtype),
                   jax.ShapeDtypeStruct(vpad.shape, vpad.dtype)],
        input_output_aliases={17: 4, 18: 5},
        compiler_params=pltpu.CompilerParams(
            dimension_semantics=("arbitrary", "arbitrary"), vmem_limit_bytes=VMEM_LIMIT),
    )(x, cos, sin, wp["g_mix"], wp["w1"], wp["g_qa"], wp["g_kva"], wp["w_uq"], wp["w_uk"],
      wp["w_uv"], wp["g_q"], wp["g_qs"], wp["g_k"], wp["g_ks"], wp["g_caq"], wp["g_cak"],
      wp["bd"], kpad, vpad)


def _mla_kernel(q_ref, k_ref, vt_ref, o_ref, m_ref, acc_ref, s_ref):
    i = pl.program_id(2)
    t = MLA_TQ
    ones = jnp.ones((SUM_ROWS, t), BF16)
    m_ref[...] = jnp.full(m_ref.shape, NEG_INF, F32)
    acc_ref[...] = jnp.zeros(acc_ref.shape, F32)

    def scores(j, hh):
        rows = pl.ds(pl.multiple_of(j * t, t), t)
        sl = slice(hh * HEAD_SLOT, (hh + 1) * HEAD_SLOT)
        return _dot_nt(k_ref[rows, sl], q_ref[:, sl])

    def step(j, last):
        def accumulate(hh, p, alpha):
            v1 = jnp.concatenate([vt_ref[hh, j], ones], axis=0)
            acc_ref[hh] = acc_ref[hh] * alpha + _dot(v1, p)

        pending = {hh: s_ref[hh] for hh in range(MLA_LOOKAHEAD)}
        done = {}
        for hh in range(MLA_HPS):
            st = pending.pop(hh)
            ahead = hh + MLA_LOOKAHEAD
            if ahead < MLA_HPS:
                pending[ahead] = scores(j, ahead)
            elif not last:
                s_ref[ahead - MLA_HPS] = scores(j + 1, ahead - MLA_HPS)
            if last:
                kc = lax.broadcasted_iota(jnp.int32, st.shape, 0) // CHUNK
                qc = lax.broadcasted_iota(jnp.int32, st.shape, 1) // CHUNK
                st = jnp.where(kc <= qc, st, NEG_INF)
            m_prev = m_ref[hh]
            m_new = jnp.maximum(m_prev, jnp.max(st, axis=0, keepdims=True))
            m_ref[hh] = m_new
            done[hh] = (jnp.exp2(st - m_new).astype(BF16), jnp.exp2(m_prev - m_new))
            if hh >= MLA_PV_DELAY:
                accumulate(hh - MLA_PV_DELAY, *done.pop(hh - MLA_PV_DELAY))
        for hh in sorted(done):
            accumulate(hh, *done[hh])

    for hh in range(MLA_LOOKAHEAD):
        s_ref[hh] = scores(0, hh)

    def body(k, carry):
        step(2 * k, False)
        step(2 * k + 1, False)
        return carry

    lax.fori_loop(0, i // 2, body, 0)

    @pl.when(i % 2 == 1)
    def _():
        step(i - 1, False)

    step(i, True)
    for pr in range(MLA_HPS // 2):
        ot = jnp.concatenate([acc_ref[hh, :MLA_V, :] / acc_ref[hh, MLA_V:MLA_V + 1, :]
                              for hh in (2 * pr, 2 * pr + 1)], axis=0)
        o_ref[:, pr * LANES:(pr + 1) * LANES] = ot.T.astype(BF16)


def _mla_attention(qm, km, vt):
    b, s, _ = qm.shape
    groups = MLA_HEADS // MLA_HPS
    nk = s // MLA_TQ
    return pl.pallas_call(
        _mla_kernel,
        grid=(b, groups, s // MLA_TQ),
        in_specs=[pl.BlockSpec((None, MLA_TQ, MLA_HPS * HEAD_SLOT), lambda bi, g, i: (bi, i, g)),
                  pl.BlockSpec((None, s, MLA_HPS * HEAD_SLOT), lambda bi, g, i: (bi, 0, g)),
                  pl.BlockSpec((None, MLA_HPS, nk, MLA_V, MLA_TQ),
                               lambda bi, g, i: (bi, g, 0, 0, 0))],
        out_specs=pl.BlockSpec((None, MLA_TQ, MLA_HPS * MLA_V), lambda bi, g, i: (bi, i, g)),
        out_shape=jax.ShapeDtypeStruct((b, s, MLA_WIDTH), BF16),
        scratch_shapes=[pltpu.VMEM((MLA_HPS, 1, MLA_TQ), F32),
                        pltpu.VMEM((MLA_HPS, MLA_V + SUM_ROWS, MLA_TQ), F32),
                        pltpu.VMEM((MLA_LOOKAHEAD, MLA_TQ, MLA_TQ), F32)],
        compiler_params=pltpu.CompilerParams(
            dimension_semantics=("arbitrary", "arbitrary", "arbitrary"),
            vmem_limit_bytes=VMEM_LIMIT),
    )(qm, km, vt)


def _ca_kernel(q_ref, k_ref, vt_ref, bias_ref, o_ref):
    lead = KV_PAD - CA_PAD
    low = lax.broadcasted_iota(jnp.int32, (1, LANES), 1) < CA_HEAD_DIM
    ones = jnp.ones((SUM_ROWS, CA_WIN), BF16)
    items = [(sub, h) for sub in range(CA_QBLOCKS) for h in range(CA_HEADS)]

    def block(sub):
        return pl.program_id(1) * CA_QBLOCKS + sub

    def scores(sub, h):
        rows = pl.ds(pl.multiple_of(lead + block(sub) * CA_TQ, BLK), CA_WIN)
        sl = slice((h // 2) * LANES, (h // 2 + 1) * LANES)
        q = q_ref[sub * CA_TQ:(sub + 1) * CA_TQ, sl]
        qh = jnp.where(low if h % 2 == 0 else ~low, q, jnp.zeros_like(q))
        return _dot_nt(k_ref[rows, sl], qh)

    def attend(sub, h, p):
        blk0 = lead // BLK + block(sub)
        vt = jnp.concatenate([vt_ref[h, blk0 + kb] for kb in range(CA_WIN // BLK)], axis=1)
        acc = _dot(jnp.concatenate([vt, ones], axis=0), p)
        return acc[:CA_HEAD_DIM] / acc[CA_HEAD_DIM:CA_HEAD_DIM + 1]

    pending = {it: scores(*it) for it in items[:CA_LOOKAHEAD]}
    probs = {}
    outs = {}

    def flush(sub, h):
        outs[h] = attend(sub, h, probs.pop((sub, h)))
        if h % 2 == 1:
            ot = jnp.concatenate([outs.pop(h - 1), outs.pop(h)], axis=0)
            o_ref[sub * CA_TQ:(sub + 1) * CA_TQ, (h // 2) * LANES:(h // 2 + 1) * LANES] = (
                ot.T.astype(BF16))

    for n, (sub, h) in enumerate(items):
        st = pending.pop((sub, h))
        if n + CA_LOOKAHEAD < len(items):
            ahead = items[n + CA_LOOKAHEAD]
            pending[ahead] = scores(*ahead)
        variant = jnp.minimum(block(sub), CA_VARIANTS - 1)
        st = st + bias_ref[variant, h]
        probs[(sub, h)] = jnp.exp2(st - jnp.max(st, axis=0, keepdims=True)).astype(BF16)
        if n >= 1:
            flush(*items[n - 1])
    flush(*items[-1])


def _chunk_attention(l, qc, kc, vct, bias):
    b, s, _ = qc.shape
    nblk = vct.shape[2]
    tq = CA_QBLOCKS * CA_TQ
    return pl.pallas_call(
        _ca_kernel,
        grid=(b, s // tq),
        in_specs=[pl.BlockSpec((None, tq, CA_WIDTH), lambda bi, i: (bi, i, 0)),
                  pl.BlockSpec((None, s + KV_PAD, CA_WIDTH), lambda bi, i: (bi, 0, 0)),
                  pl.BlockSpec((None, CA_HEADS, nblk, CA_HEAD_DIM, BLK),
                               lambda bi, i: (bi, 0, 0, 0, 0)),
                  pl.BlockSpec((None, CA_VARIANTS, CA_HEADS, CA_WIN, CA_TQ),
                               lambda bi, i: (l, 0, 0, 0, 0), pipeline_mode=pl.Buffered(1))],
        out_specs=pl.BlockSpec((None, tq, CA_WIDTH), lambda bi, i: (bi, i, 0)),
        out_shape=jax.ShapeDtypeStruct((b, s, CA_WIDTH), BF16),
        compiler_params=pltpu.CompilerParams(
            dimension_semantics=("arbitrary", "arbitrary"), vmem_limit_bytes=VMEM_LIMIT),
    )(qc, kc, vct, bias)


def _ffn_kernel(x_ref, oa_ref, ob_ref, goa_ref, gob_ref, wout_ref, gffn_ref, wup_ref, cw_ref,
                cb_ref, wdown_ref, out_ref, carry_ref, u_ref, h_ref):
    tm = TM_FFN

    @pl.when(pl.program_id(1) == 0)
    def _():
        carry_ref[...] = jnp.zeros(carry_ref.shape, F32)

    na = _rms(oa_ref[...].astype(F32), goa_ref[...]).astype(BF16)
    nb = _rms(ob_ref[...].astype(F32), gob_ref[...]).astype(BF16)
    x1 = (x_ref[...] + _dot(na, wout_ref[0:MLA_WIDTH, :])
          + _dot(nb, wout_ref[MLA_WIDTH:MLA_WIDTH + CA_WIDTH, :]))
    out_ref[...] = x1
    h_ref[...] = _rms(x1, gffn_ref[...]).astype(BF16)

    def up(c, slot):
        u = _dot(h_ref[...], wup_ref[c])
        u_ref[slot, 0:CARRY_ROWS, :] = carry_ref[c]
        u_ref[slot, CARRY_ROWS:CARRY_ROWS + tm, :] = u
        carry_ref[c] = u[tm - CARRY_ROWS:, :]

    def act_down(c, slot):
        cw = cw_ref[c]
        y = cb_ref[c]
        for tap in range(CONV_W):
            off = CARRY_ROWS - (CONV_W - 1) + tap
            y = y + u_ref[slot, off:off + tm, :] * cw[tap:tap + 1, :]
        g = y[:, :FF_CHUNK]
        a = g * (1.0 / (1.0 + jnp.exp(-g))) * y[:, FF_CHUNK:]
        out_ref[...] += _dot(a.astype(BF16), wdown_ref[c])

    up(0, 0)

    def body(k, carry):
        c = 2 * k
        up(c + 1, 1)
        act_down(c, 0)
        up(c + 2, 0)
        act_down(c + 1, 1)
        return carry

    lax.fori_loop(0, (N_FF_CHUNKS - 1) // 2, body, 0)
    act_down(N_FF_CHUNKS - 1, 0)


def _outproj_ffn(l, x, oa, ob, wp):
    b, s, _ = x.shape

    def tok(width):
        return pl.BlockSpec((None, TM_FFN, width), lambda bi, i: (bi, i, 0))

    def lw(shape):
        return pl.BlockSpec((None,) + shape, lambda bi, i: (l,) + (0,) * len(shape),
                            pipeline_mode=pl.Buffered(1))

    return pl.pallas_call(
        _ffn_kernel,
        grid=(b, s // TM_FFN),
        in_specs=[tok(D_MODEL), tok(MLA_WIDTH), tok(CA_WIDTH),
                  lw((1, MLA_WIDTH)), lw((1, CA_WIDTH)), lw((D_MODEL, D_MODEL)),
                  lw((1, D_MODEL)), lw((N_FF_CHUNKS, D_MODEL, 2 * FF_CHUNK)),
                  lw((N_FF_CHUNKS, CARRY_ROWS, 2 * FF_CHUNK)),
                  lw((N_FF_CHUNKS, 1, 2 * FF_CHUNK)),
                  lw((N_FF_CHUNKS, FF_CHUNK, D_MODEL))],
        out_specs=tok(D_MODEL),
        out_shape=jax.ShapeDtypeStruct((b, s, D_MODEL), F32),
        scratch_shapes=[pltpu.VMEM((N_FF_CHUNKS, CARRY_ROWS, 2 * FF_CHUNK), F32),
                        pltpu.VMEM((2, CARRY_ROWS + TM_FFN, 2 * FF_CHUNK), F32),
                        pltpu.VMEM((TM_FFN, D_MODEL), BF16)],
        compiler_params=pltpu.CompilerParams(
            dimension_semantics=("arbitrary", "arbitrary"), vmem_limit_bytes=VMEM_LIMIT),
    )(x, oa, ob, wp["g_oa"], wp["g_ob"], wp["w_out"], wp["g_ffn"], wp["w_up"], wp["conv_w"],
      wp["conv_b"], wp["w_down"])


def _rope_slot(first, second, axis_pad):
    t = jnp.concatenate([first, second], axis=-1)
    return jnp.pad(t, axis_pad + ((MLA_NOPE, HEAD_SLOT - MLA_QK),))


def _prepare(g_mix, w_in, w_uq, w_ukv, g_q_lora, g_kv_lora, g_mla_q, g_mla_k, g_ca_q, g_ca_k,
             rel_bias, g_out_mla, g_out_ca, w_out, g_ffn, w_up, conv_w, conv_b, w_down):
    depth = w_in.shape[0]
    row = lambda g: g[:, None, :]
    zpad = HEAD_SLOT - MLA_QK
    r1 = slice(MLA_NOPE, MLA_NOPE + HALF_ROPE)
    r2 = slice(MLA_NOPE + HALF_ROPE, MLA_QK)

    w_kr = w_in[:, :, OFF_KR:OFF_CA]
    kr_slot = _rope_slot(w_kr[..., :HALF_ROPE], w_kr[..., HALF_ROPE:], ((0, 0), (0, 0)))
    kr_swap = _rope_slot(w_kr[..., HALF_ROPE:], w_kr[..., :HALF_ROPE], ((0, 0), (0, 0)))
    w1 = jnp.concatenate([w_in[:, :, OFF_CQ:OFF_KR], kr_slot, kr_swap, w_in[:, :, OFF_CA:]],
                         axis=-1)

    wq = w_uq.reshape(depth, Q_LORA, MLA_HEADS, MLA_QK)
    wq_slot = jnp.pad(wq, ((0, 0), (0, 0), (0, 0), (0, zpad))).reshape(depth, Q_LORA, -1)
    wq_swap = _rope_slot(wq[..., r2], wq[..., r1], ((0, 0), (0, 0), (0, 0)))
    wq2 = jnp.concatenate([wq_slot, wq_swap.reshape(depth, Q_LORA, -1)], axis=-1)
    wkv = w_ukv.reshape(depth, KV_LORA, MLA_HEADS, MLA_NOPE + MLA_V)
    wk = jnp.pad(wkv[..., :MLA_NOPE], ((0, 0), (0, 0), (0, 0), (0, HEAD_SLOT - MLA_NOPE)))
    wk = wk.reshape(depth, KV_LORA, -1)
    wv = wkv[..., MLA_NOPE:].reshape(depth, KV_LORA, MLA_WIDTH)

    gq = g_mla_q * (MLA_QK ** -0.5 * LOG2_E)
    g_q = jnp.pad(gq, ((0, 0), (0, zpad)))
    g_qs = _rope_slot(gq[:, r2], gq[:, r1], ((0, 0),))
    g_k = jnp.pad(g_mla_k, ((0, 0), (0, zpad)))
    g_ks = _rope_slot(g_mla_k[:, r2], g_mla_k[:, r1], ((0, 0),))
    g_caq = jnp.tile(g_ca_q * (CA_HEAD_DIM ** -0.5 * LOG2_E), (1, CA_HEADS))
    g_cak = jnp.tile(g_ca_k, (1, CA_HEADS))
    head = jnp.arange(CA_WIDTH) // CA_HEAD_DIM
    bd = (head[:, None] == head[None, :]).astype(BF16)

    ext = jnp.pad(rel_bias, ((0, 0), (0, 0), (CA_TQ - 1 - REL_CLIP, CA_WIN - REL_CLIP)),
                  mode="edge")
    base = ext[:, :, ::-1].reshape(depth * CA_HEADS, 1, BASE_W)

    def chunks(w):
        g = w[..., :D_FF].reshape(w.shape[:-1] + (N_FF_CHUNKS, FF_CHUNK))
        v = w[..., D_FF:].reshape(w.shape[:-1] + (N_FF_CHUNKS, FF_CHUNK))
        return jnp.concatenate([g, v], axis=-1)

    w_up_c = jnp.moveaxis(chunks(w_up), 2, 1)
    conv_c = jnp.moveaxis(chunks(conv_w), 2, 1)
    conv_c = jnp.pad(conv_c, ((0, 0), (0, 0), (0, CARRY_ROWS - CONV_W), (0, 0)))
    cb_c = chunks(conv_b)[:, :, None, :]
    return {
        "g_mix": row(g_mix), "w1": w1.astype(BF16), "g_qa": row(g_q_lora),
        "g_kva": row(g_kv_lora), "w_uq": wq2.astype(BF16), "w_uk": wk.astype(BF16),
        "w_uv": wv.astype(BF16), "g_q": row(g_q), "g_qs": row(g_qs), "g_k": row(g_k),
        "g_ks": row(g_ks), "g_caq": row(g_caq), "g_cak": row(g_cak), "bd": bd, "base": base,
        "g_oa": row(g_out_mla), "g_ob": row(g_out_ca), "w_out": w_out.astype(BF16),
        "g_ffn": row(g_ffn), "w_up": w_up_c.astype(BF16), "conv_w": conv_c, "conv_b": cb_c,
        "w_down": w_down.reshape(depth, N_FF_CHUNKS, FF_CHUNK, D_MODEL).astype(BF16),
    }


def kernel(x, positions, g_mix, w_in, w_uq, w_ukv, g_q_lora, g_kv_lora, g_mla_q, g_mla_k,
           g_ca_q, g_ca_k, rel_bias, g_out_mla, g_out_ca, w_out, g_ffn, w_up, conv_w, conv_b,
           w_down):
    b, s, _ = x.shape
    depth = w_in.shape[0]
    assert s % TM_IN == 0 and s % TM_FFN == 0 and KV_PAD % TM_IN == 0
    assert (KV_PAD - CA_PAD) % BLK == 0
    wp = _prepare(g_mix, w_in, w_uq, w_ukv, g_q_lora, g_kv_lora, g_mla_q, g_mla_k, g_ca_q,
                  g_ca_k, rel_bias, g_out_mla, g_out_ca, w_out, g_ffn, w_up, conv_w, conv_b,
                  w_down)

    inv = ROPE_BASE ** (-jnp.arange(0, MLA_ROPE, 2, dtype=F32) / MLA_ROPE)
    inv_full = jnp.zeros((1, LANES), F32)
    inv_full = inv_full.at[0, MLA_NOPE:MLA_NOPE + HALF_ROPE].set(inv)
    inv_full = inv_full.at[0, MLA_NOPE + HALF_ROPE:MLA_QK].set(inv)
    pos_f = positions.astype(F32).reshape(b * s, 1)
    cos, sin = [t.reshape(b, s, LANES) for t in _rope_tables(pos_f, inv_full)]
    bias = _bias_tiles(wp["base"], depth)

    for l in range(depth):
        kpad = jnp.zeros((b, s + KV_PAD, CA_WIDTH), BF16)
        vpad = jnp.zeros((b, CA_HEADS, (s + KV_PAD) // BLK, CA_HEAD_DIM, BLK), BF16)
        qm, km, vt, qc, kc, vct = _inproj(l, x, cos, sin, wp, kpad, vpad)
        oa = _mla_attention(qm, km, vt)
        ob = _chunk_attention(l, qc, kc, vct, bias)
        x = _outproj_ffn(l, x, oa, ob, wp)
    return x
```

```python
import jax
import jax.numpy as jnp
from jax import lax
from jax.experimental import pallas as pl
from jax.experimental.pallas import tpu as pltpu

F32 = jnp.float32
BF16 = jnp.bfloat16

D_MODEL = 1024
CHUNK = 64
EPS = 1e-6
NEG_INF = -1e30
LOG2_E = 1.4426950408889634

MLA_HEADS = 8
MLA_NOPE = 64
MLA_ROPE = 32
MLA_QK = MLA_NOPE + MLA_ROPE
MLA_V = 64
Q_LORA = 256
KV_LORA = 128
ROPE_BASE = 10000.0
MLA_WIDTH = MLA_HEADS * MLA_V

CA_HEADS = 8
CA_HEAD_DIM = 64
CA_WIDTH = CA_HEADS * CA_HEAD_DIM
LEFT_CHUNKS = 8
REL_CLIP = 128

OFF_CQ = 0
OFF_CKV = OFF_CQ + Q_LORA
OFF_KR = OFF_CKV + KV_LORA
OFF_CA = OFF_KR + MLA_ROPE

D_FF = 2816
CONV_W = 3

LANES = 128
HEAD_SLOT = LANES
HALF_ROPE = MLA_ROPE // 2
MLA_SLOTS = MLA_HEADS * HEAD_SLOT

P_CQ = 0
P_CKV = P_CQ + Q_LORA
P_KR = P_CKV + KV_LORA
P_KRS = P_KR + HEAD_SLOT
P_CAQ = P_KRS + HEAD_SLOT
P_CAK = P_CAQ + CA_WIDTH
P_CAV = P_CAK + CA_WIDTH
P_COLS = P_CAV + CA_WIDTH

SUM_ROWS = 16
BLK = 256
TM_IN = 1024
CA_PAD = LEFT_CHUNKS * CHUNK
KV_PAD = max(CA_PAD, TM_IN)
CA_TQ = BLK
CA_WIN = CA_PAD + CA_TQ
BASE_W = CA_WIN + CA_TQ
CA_VARIANTS = CA_PAD // CA_TQ + 1
CA_LOOKAHEAD = 3
CA_QBLOCKS = 2
MLA_TQ = BLK
MLA_HPS = 8
MLA_LOOKAHEAD = 4
MLA_PV_DELAY = 1
TM_FFN = 512
FF_CHUNK = 256
N_FF_CHUNKS = D_FF // FF_CHUNK
CARRY_ROWS = 8

VMEM_LIMIT = 56 * 1024 * 1024


def _rms(x, g):
    return x * lax.rsqrt(jnp.mean(x * x, axis=-1, keepdims=True) + EPS) * g


def _dot(a, b):
    return jnp.dot(a, b, preferred_element_type=F32)


def _dot_nt(a, b):
    return lax.dot_general(a, b, (((1,), (1,)), ((), ())), preferred_element_type=F32)


def _rope_kernel(pos_ref, inv_ref, c_ref, s_ref):
    ang = pos_ref[...] * inv_ref[...]
    s = jnp.sin(ang)
    lane = lax.broadcasted_iota(jnp.int32, ang.shape, 1)
    first_half = lane < MLA_NOPE + HALF_ROPE
    c_ref[...] = jnp.cos(ang)
    s_ref[...] = jnp.where(first_half, -s, s)


def _rope_tables(pos_f, inv_full):
    t = pos_f.shape[0]
    tm = 1024
    out = jax.ShapeDtypeStruct((t, LANES), F32)
    return pl.pallas_call(
        _rope_kernel,
        grid=(t // tm,),
        in_specs=[pl.BlockSpec((tm, 1), lambda i: (i, 0)),
                  pl.BlockSpec((1, LANES), lambda i: (0, 0))],
        out_specs=[pl.BlockSpec((tm, LANES), lambda i: (i, 0))] * 2,
        out_shape=[out] * 2,
        compiler_params=pltpu.CompilerParams(dimension_semantics=("arbitrary",)),
    )(pos_f, inv_full)


def _bias_kernel(base_ref, out_ref, tile_ref):
    @pl.when(pl.program_id(1) == 0)
    def _():
        x = jnp.broadcast_to(base_ref[...], (CA_TQ, BASE_W))
        row = lax.broadcasted_iota(jnp.int32, (CA_TQ, BASE_W), 0)
        for b in range(CA_TQ.bit_length() - 1):
            x = jnp.where(((row >> b) & 1) == 1, pltpu.roll(x, 1 << b, 1), x)
        t = x[:, CA_TQ:] * LOG2_E
        i = lax.broadcasted_iota(jnp.int32, (CA_TQ, CA_WIN), 0) // CHUNK
        j = lax.broadcasted_iota(jnp.int32, (CA_TQ, CA_WIN), 1) // CHUNK
        band = (j >= i) & (j <= i + LEFT_CHUNKS)
        tile_ref[...] = jnp.where(band, t, NEG_INF).T

    key = lax.broadcasted_iota(jnp.int32, (CA_WIN, CA_TQ), 0)
    real = key >= CA_PAD - pl.program_id(1) * CA_TQ
    out_ref[...] = jnp.where(real, tile_ref[...], NEG_INF)


def _bias_tiles(base, depth):
    return pl.pallas_call(
        _bias_kernel,
        grid=(depth * CA_HEADS, CA_VARIANTS),
        in_specs=[pl.BlockSpec((None, 1, BASE_W), lambda i, v: (i, 0, 0))],
        out_specs=pl.BlockSpec((None, None, None, CA_WIN, CA_TQ),
                               lambda i, v: (i // CA_HEADS, v, i % CA_HEADS, 0, 0)),
        out_shape=jax.ShapeDtypeStruct((depth, CA_VARIANTS, CA_HEADS, CA_WIN, CA_TQ), F32),
        scratch_shapes=[pltpu.VMEM((CA_WIN, CA_TQ), F32)],
        compiler_params=pltpu.CompilerParams(dimension_semantics=("arbitrary", "arbitrary")),
    )(base)


def _inproj_kernel(x_ref, c_ref, s_ref, gmix_ref, w1_ref, gqa_ref, gkva_ref, wuq_ref, wuk_ref,
                   wuv_ref, gq_ref, gqs_ref, gk_ref, gks_ref, gcaq_ref, gcak_ref, bd_ref,
                   kpad_ref, vpad_ref,
                   qt_ref, km_ref, vt_ref, qc_ref, kc_ref, vct_ref):
    del kpad_ref, vpad_ref
    n_sub = TM_IN // BLK

    def project(r):
        rows = slice(r * BLK, (r + 1) * BLK)
        h = _rms(x_ref[rows, :], gmix_ref[...]).astype(BF16)
        return _dot(h, w1_ref[...])

    def up_project(proj):
        cqn = _rms(proj[:, P_CQ:P_CKV], gqa_ref[...]).astype(BF16)
        ckvn = _rms(proj[:, P_CKV:P_KR], gkva_ref[...]).astype(BF16)
        caq = proj[:, P_CAQ:P_CAK]
        cak = proj[:, P_CAK:P_CAV]
        return {
            "q2": _dot(cqn, wuq_ref[...]),
            "kn": _dot(ckvn, wuk_ref[...]),
            "v": _dot(ckvn, wuv_ref[...]),
            "ssq": _dot((caq * caq).astype(BF16), bd_ref[...]),
            "ssk": _dot((cak * cak).astype(BF16), bd_ref[...]),
        }

    def transpose_heads(t, ref, r):
        for p in range(t.shape[1] // LANES):
            tt = t[:, p * LANES:(p + 1) * LANES].T.astype(BF16)
            ref[2 * p, r] = tt[:LANES // 2]
            ref[2 * p + 1, r] = tt[LANES // 2:]

    def finish(r, proj, up):
        rows = slice(r * BLK, (r + 1) * BLK)
        cos = c_ref[rows, :]
        sin = s_ref[rows, :]
        gq_c, gq_s = gq_ref[...] * cos, gqs_ref[...] * sin
        gk_c, gk_s = gk_ref[...] * cos, gks_ref[...] * sin
        kr = proj[:, P_KR:P_KRS]
        kr_s = proj[:, P_KRS:P_CAQ] * gk_s
        for hd in range(MLA_HEADS):
            sl = slice(hd * HEAD_SLOT, (hd + 1) * HEAD_SLOT)
            ssl = slice(MLA_SLOTS + hd * HEAD_SLOT, MLA_SLOTS + (hd + 1) * HEAD_SLOT)
            q = up["q2"][:, sl]
            rq = lax.rsqrt(jnp.sum(q * q, axis=-1, keepdims=True) * (1.0 / MLA_QK) + EPS)
            qr = (q * gq_c + up["q2"][:, ssl] * gq_s) * rq
            qt_ref[hd, :, rows] = qr.T.astype(BF16)
            k = up["kn"][:, sl] + kr
            rk = lax.rsqrt(jnp.sum(k * k, axis=-1, keepdims=True) * (1.0 / MLA_QK) + EPS)
            km_ref[rows, sl] = ((k * gk_c + kr_s) * rk).astype(BF16)
        transpose_heads(up["v"], vt_ref, r)
        caq = proj[:, P_CAQ:P_CAK]
        cak = proj[:, P_CAK:P_CAV]
        qc_ref[rows, :] = (caq * lax.rsqrt(up["ssq"] * (1.0 / CA_HEAD_DIM) + EPS)
                           * gcaq_ref[...]).astype(BF16)
        kc_ref[rows, :] = (cak * lax.rsqrt(up["ssk"] * (1.0 / CA_HEAD_DIM) + EPS)
                           * gcak_ref[...]).astype(BF16)
        transpose_heads(proj[:, P_CAV:P_COLS], vct_ref, r)

    proj = project(0)
    for r in range(n_sub):
        nxt = project(r + 1) if r + 1 < n_sub else None
        finish(r, proj, up_project(proj))
        proj = nxt


def _inproj(l, x, cos, sin, wp, kpad, vpad):
    b, s, _ = x.shape
    n_sub = TM_IN // BLK
    pad_tiles = KV_PAD // TM_IN

    def tok(width):
        return pl.BlockSpec((None, TM_IN, width), lambda bi, i: (bi, i, 0))

    def lw(shape):
        return pl.BlockSpec((None,) + shape, lambda bi, i: (l,) + (0,) * len(shape),
                            pipeline_mode=pl.Buffered(1))

    const = pl.BlockSpec((CA_WIDTH, CA_WIDTH), lambda bi, i: (0, 0),
                         pipeline_mode=pl.Buffered(1))
    any_spec = pl.BlockSpec(memory_space=pl.ANY)
    heads_t = lambda off: pl.BlockSpec((None, MLA_HEADS, n_sub, LANES // 2, BLK),
                                       lambda bi, i: (bi, 0, i + off, 0, 0))
    tok_shape = lambda w: jax.ShapeDtypeStruct((b, s, w), BF16)
    return pl.pallas_call(
        _inproj_kernel,
        grid=(b, s // TM_IN),
        in_specs=[tok(D_MODEL), tok(LANES), tok(LANES),
                  lw((1, D_MODEL)), lw((D_MODEL, P_COLS)), lw((1, Q_LORA)), lw((1, KV_LORA)),
                  lw((Q_LORA, 2 * MLA_SLOTS)), lw((KV_LORA, MLA_SLOTS)),
                  lw((KV_LORA, MLA_WIDTH)), lw((1, HEAD_SLOT)), lw((1, HEAD_SLOT)),
                  lw((1, HEAD_SLOT)), lw((1, HEAD_SLOT)),
                  lw((1, CA_WIDTH)), lw((1, CA_WIDTH)), const, any_spec, any_spec],
        out_specs=[pl.BlockSpec((None, MLA_HEADS, HEAD_SLOT, TM_IN), lambda bi, i: (bi, 0, 0, i)),
                   tok(MLA_SLOTS), heads_t(0), tok(CA_WIDTH),
                   pl.BlockSpec((None, TM_IN, CA_WIDTH), lambda bi, i: (bi, i + pad_tiles, 0)),
                   heads_t(pad_tiles)],
        out_shape=[jax.ShapeDtypeStruct((b, MLA_HEADS, HEAD_SLOT, s), BF16), tok_shape(MLA_SLOTS),
                   jax.ShapeDtypeStruct((b, MLA_HEADS, s // BLK, MLA_V, BLK), BF16),
                   tok_shape(CA_WIDTH), jax.ShapeDtypeStruct(kpad.shape, kpad.dtype),
                   jax.ShapeDtypeStruct(vpad.shape, vpad.dtype)],
        input_output_aliases={17: 4, 18: 5},
        compiler_params=pltpu.CompilerParams(
            dimension_semantics=("arbitrary", "arbitrary"), vmem_limit_bytes=VMEM_LIMIT),
    )(x, cos, sin, wp["g_mix"], wp["w1"], wp["g_qa"], wp["g_kva"], wp["w_uq"], wp["w_uk"],
      wp["w_uv"], wp["g_q"], wp["g_qs"], wp["g_k"], wp["g_ks"], wp["g_caq"], wp["g_cak"],
      wp["bd"], kpad, vpad)


def _mla_kernel(qt_ref, k_ref, vt_ref, o_ref, m_ref, acc_ref, s_ref):
    i = pl.program_id(2)
    t = MLA_TQ
    ones = jnp.ones((SUM_ROWS, t), BF16)
    m_ref[...] = jnp.full(m_ref.shape, NEG_INF, F32)
    acc_ref[...] = jnp.zeros(acc_ref.shape, F32)

    def scores(j, hh):
        rows = pl.ds(pl.multiple_of(j * t, t), t)
        sl = slice(hh * HEAD_SLOT, (hh + 1) * HEAD_SLOT)
        return _dot(k_ref[rows, sl], qt_ref[hh])

    def step(j, last):
        def accumulate(hh, p, alpha):
            v1 = jnp.concatenate([vt_ref[hh, j], ones], axis=0)
            acc_ref[hh] = acc_ref[hh] * alpha + _dot(v1, p)

        pending = {hh: s_ref[hh] for hh in range(MLA_LOOKAHEAD)}
        done = {}
        for hh in range(MLA_HPS):
            st = pending.pop(hh)
            ahead = hh + MLA_LOOKAHEAD
            if ahead < MLA_HPS:
                pending[ahead] = scores(j, ahead)
            elif not last:
                s_ref[ahead - MLA_HPS] = scores(j + 1, ahead - MLA_HPS)
            if last:
                kc = lax.broadcasted_iota(jnp.int32, st.shape, 0) // CHUNK
                qc = lax.broadcasted_iota(jnp.int32, st.shape, 1) // CHUNK
                st = jnp.where(kc <= qc, st, NEG_INF)
            m_prev = m_ref[hh]
            m_new = jnp.maximum(m_prev, jnp.max(st, axis=0, keepdims=True))
            m_ref[hh] = m_new
            done[hh] = (jnp.exp2(st - m_new).astype(BF16), jnp.exp2(m_prev - m_new))
            if hh >= MLA_PV_DELAY:
                accumulate(hh - MLA_PV_DELAY, *done.pop(hh - MLA_PV_DELAY))
        for hh in sorted(done):
            accumulate(hh, *done[hh])

    for hh in range(MLA_LOOKAHEAD):
        s_ref[hh] = scores(0, hh)

    def body(k, carry):
        step(2 * k, False)
        step(2 * k + 1, False)
        return carry

    lax.fori_loop(0, i // 2, body, 0)

    @pl.when(i % 2 == 1)
    def _():
        step(i - 1, False)

    step(i, True)
    for pr in range(MLA_HPS // 2):
        ot = jnp.concatenate([acc_ref[hh, :MLA_V, :] / acc_ref[hh, MLA_V:MLA_V + 1, :]
                              for hh in (2 * pr, 2 * pr + 1)], axis=0)
        o_ref[:, pr * LANES:(pr + 1) * LANES] = ot.T.astype(BF16)


def _mla_attention(qt, km, vt):
    b, s, _ = km.shape
    groups = MLA_HEADS // MLA_HPS
    nk = s // MLA_TQ
    return pl.pallas_call(
        _mla_kernel,
        grid=(b, groups, s // MLA_TQ),
        in_specs=[pl.BlockSpec((None, MLA_HPS, HEAD_SLOT, MLA_TQ), lambda bi, g, i: (bi, g, 0, i)),
                  pl.BlockSpec((None, s, MLA_HPS * HEAD_SLOT), lambda bi, g, i: (bi, 0, g)),
                  pl.BlockSpec((None, MLA_HPS, nk, MLA_V, MLA_TQ),
                               lambda bi, g, i: (bi, g, 0, 0, 0))],
        out_specs=pl.BlockSpec((None, MLA_TQ, MLA_HPS * MLA_V), lambda bi, g, i: (bi, i, g)),
        out_shape=jax.ShapeDtypeStruct((b, s, MLA_WIDTH), BF16),
        scratch_shapes=[pltpu.VMEM((MLA_HPS, 1, MLA_TQ), F32),
                        pltpu.VMEM((MLA_HPS, MLA_V + SUM_ROWS, MLA_TQ), F32),
                        pltpu.VMEM((MLA_LOOKAHEAD, MLA_TQ, MLA_TQ), F32)],
        compiler_params=pltpu.CompilerParams(
            dimension_semantics=("arbitrary", "arbitrary", "arbitrary"),
            vmem_limit_bytes=VMEM_LIMIT),
    )(qt, km, vt)


def _ca_kernel(q_ref, k_ref, vt_ref, bias_ref, o_ref):
    lead = KV_PAD - CA_PAD
    low = lax.broadcasted_iota(jnp.int32, (1, LANES), 1) < CA_HEAD_DIM
    ones = jnp.ones((SUM_ROWS, CA_WIN), BF16)
    items = [(sub, h) for sub in range(CA_QBLOCKS) for h in range(CA_HEADS)]

    def block(sub):
        return pl.program_id(1) * CA_QBLOCKS + sub

    def scores(sub, h):
        rows = pl.ds(pl.multiple_of(lead + block(sub) * CA_TQ, BLK), CA_WIN)
        sl = slice((h // 2) * LANES, (h // 2 + 1) * LANES)
        q = q_ref[sub * CA_TQ:(sub + 1) * CA_TQ, sl]
        qh = jnp.where(low if h % 2 == 0 else ~low, q, jnp.zeros_like(q))
        return _dot_nt(k_ref[rows, sl], qh)

    def attend(sub, h, p):
        blk0 = lead // BLK + block(sub)
        vt = jnp.concatenate([vt_ref[h, blk0 + kb] for kb in range(CA_WIN // BLK)], axis=1)
        acc = _dot(jnp.concatenate([vt, ones], axis=0), p)
        return acc[:CA_HEAD_DIM] / acc[CA_HEAD_DIM:CA_HEAD_DIM + 1]

    pending = {it: scores(*it) for it in items[:CA_LOOKAHEAD]}
    probs = {}
    outs = {}

    def flush(sub, h):
        outs[h] = attend(sub, h, probs.pop((sub, h)))
        if h % 2 == 1:
            ot = jnp.concatenate([outs.pop(h - 1), outs.pop(h)], axis=0)
            o_ref[sub * CA_TQ:(sub + 1) * CA_TQ, (h // 2) * LANES:(h // 2 + 1) * LANES] = (
                ot.T.astype(BF16))

    for n, (sub, h) in enumerate(items):
        st = pending.pop((sub, h))
        if n + CA_LOOKAHEAD < len(items):
            ahead = items[n + CA_LOOKAHEAD]
            pending[ahead] = scores(*ahead)
        variant = jnp.minimum(block(sub), CA_VARIANTS - 1)
        st = st + bias_ref[variant, h]
        probs[(sub, h)] = jnp.exp2(st - jnp.max(st, axis=0, keepdims=True)).astype(BF16)
        if n >= 1:
            flush(*items[n - 1])
    flush(*items[-1])


def _chunk_attention(l, qc, kc, vct, bias):
    b, s, _ = qc.shape
    nblk = vct.shape[2]
    tq = CA_QBLOCKS * CA_TQ
    return pl.pallas_call(
        _ca_kernel,
        grid=(b, s // tq),
        in_specs=[pl.BlockSpec((None, tq, CA_WIDTH), lambda bi, i: (bi, i, 0)),
                  pl.BlockSpec((None, s + KV_PAD, CA_WIDTH), lambda bi, i: (bi, 0, 0)),
                  pl.BlockSpec((None, CA_HEADS, nblk, CA_HEAD_DIM, BLK),
                               lambda bi, i: (bi, 0, 0, 0, 0)),
                  pl.BlockSpec((None, CA_VARIANTS, CA_HEADS, CA_WIN, CA_TQ),
                               lambda bi, i: (l, 0, 0, 0, 0), pipeline_mode=pl.Buffered(1))],
        out_specs=pl.BlockSpec((None, tq, CA_WIDTH), lambda bi, i: (bi, i, 0)),
        out_shape=jax.ShapeDtypeStruct((b, s, CA_WIDTH), BF16),
        compiler_params=pltpu.CompilerParams(
            dimension_semantics=("arbitrary", "arbitrary"), vmem_limit_bytes=VMEM_LIMIT),
    )(qc, kc, vct, bias)


def _ffn_kernel(x_ref, oa_ref, ob_ref, goa_ref, gob_ref, wout_ref, gffn_ref, wup_ref, cw_ref,
                cb_ref, wdown_ref, out_ref, carry_ref, u_ref, h_ref):
    tm = TM_FFN

    @pl.when(pl.program_id(1) == 0)
    def _():
        carry_ref[...] = jnp.zeros(carry_ref.shape, F32)

    na = _rms(oa_ref[...].astype(F32), goa_ref[...]).astype(BF16)
    nb = _rms(ob_ref[...].astype(F32), gob_ref[...]).astype(BF16)
    x1 = (x_ref[...] + _dot(na, wout_ref[0:MLA_WIDTH, :])
          + _dot(nb, wout_ref[MLA_WIDTH:MLA_WIDTH + CA_WIDTH, :]))
    out_ref[...] = x1
    h_ref[...] = _rms(x1, gffn_ref[...]).astype(BF16)

    def up(c, slot):
        u = _dot(h_ref[...], wup_ref[c])
        u_ref[slot, 0:CARRY_ROWS, :] = carry_ref[c]
        u_ref[slot, CARRY_ROWS:CARRY_ROWS + tm, :] = u
        carry_ref[c] = u[tm - CARRY_ROWS:, :]

    def act_down(c, slot):
        cw = cw_ref[c]
        y = cb_ref[c]
        for tap in range(CONV_W):
            off = CARRY_ROWS - (CONV_W - 1) + tap
            y = y + u_ref[slot, off:off + tm, :] * cw[tap:tap + 1, :]
        g = y[:, :FF_CHUNK]
        a = g * (1.0 / (1.0 + jnp.exp(-g))) * y[:, FF_CHUNK:]
        out_ref[...] += _dot(a.astype(BF16), wdown_ref[c])

    up(0, 0)

    def body(k, carry):
        c = 2 * k
        up(c + 1, 1)
        act_down(c, 0)
        up(c + 2, 0)
        act_down(c + 1, 1)
        return carry

    lax.fori_loop(0, (N_FF_CHUNKS - 1) // 2, body, 0)
    act_down(N_FF_CHUNKS - 1, 0)


def _outproj_ffn(l, x, oa, ob, wp):
    b, s, _ = x.shape

    def tok(width):
        return pl.BlockSpec((None, TM_FFN, width), lambda bi, i: (bi, i, 0))

    def lw(shape):
        return pl.BlockSpec((None,) + shape, lambda bi, i: (l,) + (0,) * len(shape),
                            pipeline_mode=pl.Buffered(1))

    return pl.pallas_call(
        _ffn_kernel,
        grid=(b, s // TM_FFN),
        in_specs=[tok(D_MODEL), tok(MLA_WIDTH), tok(CA_WIDTH),
                  lw((1, MLA_WIDTH)), lw((1, CA_WIDTH)), lw((D_MODEL, D_MODEL)),
                  lw((1, D_MODEL)), lw((N_FF_CHUNKS, D_MODEL, 2 * FF_CHUNK)),
                  lw((N_FF_CHUNKS, CARRY_ROWS, 2 * FF_CHUNK)),
                  lw((N_FF_CHUNKS, 1, 2 * FF_CHUNK)),
                  lw((N_FF_CHUNKS, FF_CHUNK, D_MODEL))],
        out_specs=tok(D_MODEL),
        out_shape=jax.ShapeDtypeStruct((b, s, D_MODEL), F32),
        scratch_shapes=[pltpu.VMEM((N_FF_CHUNKS, CARRY_ROWS, 2 * FF_CHUNK), F32),
                        pltpu.VMEM((2, CARRY_ROWS + TM_FFN, 2 * FF_CHUNK), F32),
                        pltpu.VMEM((TM_FFN, D_MODEL), BF16)],
        compiler_params=pltpu.CompilerParams(
            dimension_semantics=("arbitrary", "arbitrary"), vmem_limit_bytes=VMEM_LIMIT),
    )(x, oa, ob, wp["g_oa"], wp["g_ob"], wp["w_out"], wp["g_ffn"], wp["w_up"], wp["conv_w"],
      wp["conv_b"], wp["w_down"])


def _rope_slot(first, second, axis_pad):
    t = jnp.concatenate([first, second], axis=-1)
    return jnp.pad(t, axis_pad + ((MLA_NOPE, HEAD_SLOT - MLA_QK),))


def _prepare(g_mix, w_in, w_uq, w_ukv, g_q_lora, g_kv_lora, g_mla_q, g_mla_k, g_ca_q, g_ca_k,
             rel_bias, g_out_mla, g_out_ca, w_out, g_ffn, w_up, conv_w, conv_b, w_down):
    depth = w_in.shape[0]
    row = lambda g: g[:, None, :]
    zpad = HEAD_SLOT - MLA_QK
    r1 = slice(MLA_NOPE, MLA_NOPE + HALF_ROPE)
    r2 = slice(MLA_NOPE + HALF_ROPE, MLA_QK)

    w_kr = w_in[:, :, OFF_KR:OFF_CA]
    kr_slot = _rope_slot(w_kr[..., :HALF_ROPE], w_kr[..., HALF_ROPE:], ((0, 0), (0, 0)))
    kr_swap = _rope_slot(w_kr[..., HALF_ROPE:], w_kr[..., :HALF_ROPE], ((0, 0), (0, 0)))
    w1 = jnp.concatenate([w_in[:, :, OFF_CQ:OFF_KR], kr_slot, kr_swap, w_in[:, :, OFF_CA:]],
                         axis=-1)

    wq = w_uq.reshape(depth, Q_LORA, MLA_HEADS, MLA_QK)
    wq_slot = jnp.pad(wq, ((0, 0), (0, 0), (0, 0), (0, zpad))).reshape(depth, Q_LORA, -1)
    wq_swap = _rope_slot(wq[..., r2], wq[..., r1], ((0, 0), (0, 0), (0, 0)))
    wq2 = jnp.concatenate([wq_slot, wq_swap.reshape(depth, Q_LORA, -1)], axis=-1)
    wkv = w_ukv.reshape(depth, KV_LORA, MLA_HEADS, MLA_NOPE + MLA_V)
    wk = jnp.pad(wkv[..., :MLA_NOPE], ((0, 0), (0, 0), (0, 0), (0, HEAD_SLOT - MLA_NOPE)))
    wk = wk.reshape(depth, KV_LORA, -1)
    wv = wkv[..., MLA_NOPE:].reshape(depth, KV_LORA, MLA_WIDTH)

    gq = g_mla_q * (MLA_QK ** -0.5 * LOG2_E)
    g_q = jnp.pad(gq, ((0, 0), (0, zpad)))
    g_qs = _rope_slot(gq[:, r2], gq[:, r1], ((0, 0),))
    g_k = jnp.pad(g_mla_k, ((0, 0), (0, zpad)))
    g_ks = _rope_slot(g_mla_k[:, r2], g_mla_k[:, r1], ((0, 0),))
    g_caq = jnp.tile(g_ca_q * (CA_HEAD_DIM ** -0.5 * LOG2_E), (1, CA_HEADS))
    g_cak = jnp.tile(g_ca_k, (1, CA_HEADS))
    head = jnp.arange(CA_WIDTH) // CA_HEAD_DIM
    bd = (head[:, None] == head[None, :]).astype(BF16)

    ext = jnp.pad(rel_bias, ((0, 0), (0, 0), (CA_TQ - 1 - REL_CLIP, CA_WIN - REL_CLIP)),
                  mode="edge")
    base = ext[:, :, ::-1].reshape(depth * CA_HEADS, 1, BASE_W)

    def chunks(w):
        g = w[..., :D_FF].reshape(w.shape[:-1] + (N_FF_CHUNKS, FF_CHUNK))
        v = w[..., D_FF:].reshape(w.shape[:-1] + (N_FF_CHUNKS, FF_CHUNK))
        return jnp.concatenate([g, v], axis=-1)

    w_up_c = jnp.moveaxis(chunks(w_up), 2, 1)
    conv_c = jnp.moveaxis(chunks(conv_w), 2, 1)
    conv_c = jnp.pad(conv_c, ((0, 0), (0, 0), (0, CARRY_ROWS - CONV_W), (0, 0)))
    cb_c = chunks(conv_b)[:, :, None, :]
    return {
        "g_mix": row(g_mix), "w1": w1.astype(BF16), "g_qa": row(g_q_lora),
        "g_kva": row(g_kv_lora), "w_uq": wq2.astype(BF16), "w_uk": wk.astype(BF16),
        "w_uv": wv.astype(BF16), "g_q": row(g_q), "g_qs": row(g_qs), "g_k": row(g_k),
        "g_ks": row(g_ks), "g_caq": row(g_caq), "g_cak": row(g_cak), "bd": bd, "base": base,
        "g_oa": row(g_out_mla), "g_ob": row(g_out_ca), "w_out": w_out.astype(BF16),
        "g_ffn": row(g_ffn), "w_up": w_up_c.astype(BF16), "conv_w": conv_c, "conv_b": cb_c,
        "w_down": w_down.reshape(depth, N_FF_CHUNKS, FF_CHUNK, D_MODEL).astype(BF16),
    }


def kernel(x, positions, g_mix, w_in, w_uq, w_ukv, g_q_lora, g_kv_lora, g_mla_q, g_mla_k,
           g_ca_q, g_ca_k, rel_bias, g_out_mla, g_out_ca, w_out, g_ffn, w_up, conv_w, conv_b,
           w_down):
    b, s, _ = x.shape
    depth = w_in.shape[0]
    assert s % TM_IN == 0 and s % TM_FFN == 0 and KV_PAD % TM_IN == 0
    assert (KV_PAD - CA_PAD) % BLK == 0
    wp = _prepare(g_mix, w_in, w_uq, w_ukv, g_q_lora, g_kv_lora, g_mla_q, g_mla_k, g_ca_q,
                  g_ca_k, rel_bias, g_out_mla, g_out_ca, w_out, g_ffn, w_up, conv_w, conv_b,
                  w_down)

    inv = ROPE_BASE ** (-jnp.arange(0, MLA_ROPE, 2, dtype=F32) / MLA_ROPE)
    inv_full = jnp.zeros((1, LANES), F32)
    inv_full = inv_full.at[0, MLA_NOPE:MLA_NOPE + HALF_ROPE].set(inv)
    inv_full = inv_full.at[0, MLA_NOPE + HALF_ROPE:MLA_QK].set(inv)
    pos_f = positions.astype(F32).reshape(b * s, 1)
    cos, sin = [t.reshape(b, s, LANES) for t in _rope_tables(pos_f, inv_full)]
    bias = _bias_tiles(wp["base"], depth)

    kc = jnp.zeros((b, s + KV_PAD, CA_WIDTH), BF16)
    vct = jnp.zeros((b, CA_HEADS, (s + KV_PAD) // BLK, CA_HEAD_DIM, BLK), BF16)
    for l in range(depth):
        qt, km, vt, qc, kc, vct = _inproj(l, x, cos, sin, wp, kc, vct)
        oa = _mla_attention(qt, km, vt)
        ob = _chunk_attention(l, qc, kc, vct, bias)
        x = _outproj_ffn(l, x, oa, ob, wp)
    return x
```

```python
import jax
import jax.numpy as jnp
from jax import lax
from jax.experimental import pallas as pl
from jax.experimental.pallas import tpu as pltpu

F32 = jnp.float32
BF16 = jnp.bfloat16

D_MODEL = 1024
CHUNK = 64
EPS = 1e-6
NEG_INF = -1e30
LOG2_E = 1.4426950408889634

MLA_HEADS = 8
MLA_NOPE = 64
MLA_ROPE = 32
MLA_QK = MLA_NOPE + MLA_ROPE
MLA_V = 64
Q_LORA = 256
KV_LORA = 128
ROPE_BASE = 10000.0
MLA_WIDTH = MLA_HEADS * MLA_V

CA_HEADS = 8
CA_HEAD_DIM = 64
CA_WIDTH = CA_HEADS * CA_HEAD_DIM
LEFT_CHUNKS = 8
REL_CLIP = 128

OFF_CQ = 0
OFF_CKV = OFF_CQ + Q_LORA
OFF_KR = OFF_CKV + KV_LORA
OFF_CA = OFF_KR + MLA_ROPE

D_FF = 2816
CONV_W = 3

LANES = 128
HEAD_SLOT = LANES
HALF_ROPE = MLA_ROPE // 2
MLA_SLOTS = MLA_HEADS * HEAD_SLOT

P_CQ = 0
P_CKV = P_CQ + Q_LORA
P_KR = P_CKV + KV_LORA
P_KRS = P_KR + HEAD_SLOT
P_CAQ = P_KRS + HEAD_SLOT
P_CAK = P_CAQ + CA_WIDTH
P_CAV = P_CAK + CA_WIDTH
P_COLS = P_CAV + CA_WIDTH

SUM_ROWS = 16
BLK = 256
TM_IN = 1024
CA_PAD = LEFT_CHUNKS * CHUNK
KV_PAD = max(CA_PAD, TM_IN)
CA_TQ = BLK
CA_WIN = CA_PAD + CA_TQ
BASE_W = CA_WIN + CA_TQ
CA_VARIANTS = CA_PAD // CA_TQ + 1
CA_LOOKAHEAD = 3
CA_QBLOCKS = 4
MLA_TQ = BLK
MLA_HPS = 8
MLA_LOOKAHEAD = 4
MLA_PV_DELAY = 1
TM_FFN = 512
FF_CHUNK = 256
N_FF_CHUNKS = D_FF // FF_CHUNK
CARRY_ROWS = 8

V7X_VMEM_BYTES = 64 * 1024 * 1024
VMEM_LIMIT = V7X_VMEM_BYTES * 7 // 8


def _rms(x, g):
    return x * lax.rsqrt(jnp.mean(x * x, axis=-1, keepdims=True) + EPS) * g


def _dot(a, b):
    return jnp.dot(a, b, preferred_element_type=F32)


def _dot_nt(a, b):
    return lax.dot_general(a, b, (((1,), (1,)), ((), ())), preferred_element_type=F32)


def _rope_kernel(pos_ref, inv_ref, c_ref, s_ref):
    ang = pos_ref[...] * inv_ref[...]
    s = jnp.sin(ang)
    lane = lax.broadcasted_iota(jnp.int32, ang.shape, 1)
    first_half = lane < MLA_NOPE + HALF_ROPE
    c_ref[...] = jnp.cos(ang)
    s_ref[...] = jnp.where(first_half, -s, s)


def _rope_tables(pos_f, inv_full):
    t = pos_f.shape[0]
    tm = 1024
    out = jax.ShapeDtypeStruct((t, LANES), F32)
    return pl.pallas_call(
        _rope_kernel,
        grid=(t // tm,),
        in_specs=[pl.BlockSpec((tm, 1), lambda i: (i, 0)),
                  pl.BlockSpec((1, LANES), lambda i: (0, 0))],
        out_specs=[pl.BlockSpec((tm, LANES), lambda i: (i, 0))] * 2,
        out_shape=[out] * 2,
        compiler_params=pltpu.CompilerParams(dimension_semantics=("arbitrary",)),
    )(pos_f, inv_full)


def _bias_kernel(base_ref, out_ref, tile_ref):
    @pl.when(pl.program_id(1) == 0)
    def _():
        x = pltpu.roll(jnp.broadcast_to(base_ref[...], (CA_TQ, BASE_W)), 0, 1,
                       stride=1, stride_axis=0)
        t = x[:, CA_TQ:] * LOG2_E
        i = lax.broadcasted_iota(jnp.int32, (CA_TQ, CA_WIN), 0) // CHUNK
        j = lax.broadcasted_iota(jnp.int32, (CA_TQ, CA_WIN), 1) // CHUNK
        band = (j >= i) & (j <= i + LEFT_CHUNKS)
        tile_ref[...] = jnp.where(band, t, NEG_INF).T

    key = lax.broadcasted_iota(jnp.int32, (CA_WIN, CA_TQ), 0)
    real = key >= CA_PAD - pl.program_id(1) * CA_TQ
    out_ref[...] = jnp.where(real, tile_ref[...], NEG_INF)


def _bias_tiles(base, depth):
    return pl.pallas_call(
        _bias_kernel,
        grid=(depth * CA_HEADS, CA_VARIANTS),
        in_specs=[pl.BlockSpec((None, 1, BASE_W), lambda i, v: (i, 0, 0))],
        out_specs=pl.BlockSpec((None, None, None, CA_WIN, CA_TQ),
                               lambda i, v: (i // CA_HEADS, v, i % CA_HEADS, 0, 0)),
        out_shape=jax.ShapeDtypeStruct((depth, CA_VARIANTS, CA_HEADS, CA_WIN, CA_TQ), F32),
        scratch_shapes=[pltpu.VMEM((CA_WIN, CA_TQ), F32)],
        compiler_params=pltpu.CompilerParams(dimension_semantics=("arbitrary", "arbitrary")),
    )(base)


def _inproj_kernel(x_ref, c_ref, s_ref, gmix_ref, w1_ref, gqa_ref, gkva_ref, wuq_ref, wuk_ref,
                   wuv_ref, gq_ref, gqs_ref, gk_ref, gks_ref, gcaq_ref, gcak_ref, bd_ref,
                   kpad_ref, vpad_ref,
                   qt_ref, km_ref, vt_ref, qc_ref, kc_ref, vct_ref):
    del kpad_ref, vpad_ref
    n_sub = TM_IN // BLK

    def project(r):
        rows = slice(r * BLK, (r + 1) * BLK)
        h = _rms(x_ref[rows, :], gmix_ref[...]).astype(BF16)
        return _dot(h, w1_ref[...])

    def up_project(proj):
        cqn = _rms(proj[:, P_CQ:P_CKV], gqa_ref[...]).astype(BF16)
        ckvn = _rms(proj[:, P_CKV:P_KR], gkva_ref[...]).astype(BF16)
        caq = proj[:, P_CAQ:P_CAK]
        cak = proj[:, P_CAK:P_CAV]
        return {
            "q2": _dot(cqn, wuq_ref[...]),
            "kn": _dot(ckvn, wuk_ref[...]),
            "v": _dot(ckvn, wuv_ref[...]),
            "ssq": _dot((caq * caq).astype(BF16), bd_ref[...]),
            "ssk": _dot((cak * cak).astype(BF16), bd_ref[...]),
        }

    def transpose_heads(t, ref, r):
        for p in range(t.shape[1] // LANES):
            tt = t[:, p * LANES:(p + 1) * LANES].T.astype(BF16)
            ref[2 * p, r] = tt[:LANES // 2]
            ref[2 * p + 1, r] = tt[LANES // 2:]

    def finish(r, proj, up):
        rows = slice(r * BLK, (r + 1) * BLK)
        cos = c_ref[rows, :]
        sin = s_ref[rows, :]
        gq_c, gq_s = gq_ref[...] * cos, gqs_ref[...] * sin
        gk_c, gk_s = gk_ref[...] * cos, gks_ref[...] * sin
        kr = proj[:, P_KR:P_KRS]
        kr_s = proj[:, P_KRS:P_CAQ] * gk_s
        for hd in range(MLA_HEADS):
            sl = slice(hd * HEAD_SLOT, (hd + 1) * HEAD_SLOT)
            ssl = slice(MLA_SLOTS + hd * HEAD_SLOT, MLA_SLOTS + (hd + 1) * HEAD_SLOT)
            q = up["q2"][:, sl]
            rq = lax.rsqrt(jnp.sum(q * q, axis=-1, keepdims=True) * (1.0 / MLA_QK) + EPS)
            qr = (q * gq_c + up["q2"][:, ssl] * gq_s) * rq
            qt_ref[hd, :, rows] = qr.T.astype(BF16)
            k = up["kn"][:, sl] + kr
            rk = lax.rsqrt(jnp.sum(k * k, axis=-1, keepdims=True) * (1.0 / MLA_QK) + EPS)
            km_ref[rows, sl] = ((k * gk_c + kr_s) * rk).astype(BF16)
        transpose_heads(up["v"], vt_ref, r)
        caq = proj[:, P_CAQ:P_CAK]
        cak = proj[:, P_CAK:P_CAV]
        qc_ref[rows, :] = (caq * lax.rsqrt(up["ssq"] * (1.0 / CA_HEAD_DIM) + EPS)
                           * gcaq_ref[...]).astype(BF16)
        kc_ref[rows, :] = (cak * lax.rsqrt(up["ssk"] * (1.0 / CA_HEAD_DIM) + EPS)
                           * gcak_ref[...]).astype(BF16)
        transpose_heads(proj[:, P_CAV:P_COLS], vct_ref, r)

    proj = project(0)
    for r in range(n_sub):
        nxt = project(r + 1) if r + 1 < n_sub else None
        finish(r, proj, up_project(proj))
        proj = nxt


def _inproj(l, x, cos, sin, wp, kpad, vpad):
    b, s, _ = x.shape
    n_sub = TM_IN // BLK
    pad_tiles = KV_PAD // TM_IN

    def tok(width):
        return pl.BlockSpec((None, TM_IN, width), lambda bi, i: (bi, i, 0))

    def lw(shape):
        return pl.BlockSpec((None,) + shape, lambda bi, i: (l,) + (0,) * len(shape),
                            pipeline_mode=pl.Buffered(1))

    const = pl.BlockSpec((CA_WIDTH, CA_WIDTH), lambda bi, i: (0, 0),
                         pipeline_mode=pl.Buffered(1))
    any_spec = pl.BlockSpec(memory_space=pl.ANY)
    heads_t = lambda off: pl.BlockSpec((None, MLA_HEADS, n_sub, LANES // 2, BLK),
                                       lambda bi, i: (bi, 0, i + off, 0, 0))
    tok_shape = lambda w: jax.ShapeDtypeStruct((b, s, w), BF16)
    return pl.pallas_call(
        _inproj_kernel,
        grid=(b, s // TM_IN),
        in_specs=[tok(D_MODEL), tok(LANES), tok(LANES),
                  lw((1, D_MODEL)), lw((D_MODEL, P_COLS)), lw((1, Q_LORA)), lw((1, KV_LORA)),
                  lw((Q_LORA, 2 * MLA_SLOTS)), lw((KV_LORA, MLA_SLOTS)),
                  lw((KV_LORA, MLA_WIDTH)), lw((1, HEAD_SLOT)), lw((1, HEAD_SLOT)),
                  lw((1, HEAD_SLOT)), lw((1, HEAD_SLOT)),
                  lw((1, CA_WIDTH)), lw((1, CA_WIDTH)), const, any_spec, any_spec],
        out_specs=[pl.BlockSpec((None, MLA_HEADS, HEAD_SLOT, TM_IN), lambda bi, i: (bi, 0, 0, i)),
                   tok(MLA_SLOTS), heads_t(0), tok(CA_WIDTH),
                   pl.BlockSpec((None, TM_IN, CA_WIDTH), lambda bi, i: (bi, i + pad_tiles, 0)),
                   heads_t(pad_tiles)],
        out_shape=[jax.ShapeDtypeStruct((b, MLA_HEADS, HEAD_SLOT, s), BF16), tok_shape(MLA_SLOTS),
                   jax.ShapeDtypeStruct((b, MLA_HEADS, s // BLK, MLA_V, BLK), BF16),
                   tok_shape(CA_WIDTH), jax.ShapeDtypeStruct(kpad.shape, kpad.dtype),
                   jax.ShapeDtypeStruct(vpad.shape, vpad.dtype)],
        input_output_aliases={17: 4, 18: 5},
        compiler_params=pltpu.CompilerParams(
            dimension_semantics=("arbitrary", "arbitrary"), vmem_limit_bytes=VMEM_LIMIT),
    )(x, cos, sin, wp["g_mix"], wp["w1"], wp["g_qa"], wp["g_kva"], wp["w_uq"], wp["w_uk"],
      wp["w_uv"], wp["g_q"], wp["g_qs"], wp["g_k"], wp["g_ks"], wp["g_caq"], wp["g_cak"],
      wp["bd"], kpad, vpad)


def _mla_kernel(qt_ref, k_ref, vt_ref, o_ref, m_ref, acc_ref, s_ref):
    i = pl.program_id(2)
    t = MLA_TQ
    ones = jnp.ones((SUM_ROWS, t), BF16)
    m_ref[...] = jnp.full(m_ref.shape, NEG_INF, F32)
    acc_ref[...] = jnp.zeros(acc_ref.shape, F32)

    def scores(j, hh):
        rows = pl.ds(pl.multiple_of(j * t, t), t)
        sl = slice(hh * HEAD_SLOT, (hh + 1) * HEAD_SLOT)
        return _dot(k_ref[rows, sl], qt_ref[hh])

    def step(j, last):
        def accumulate(hh, p, alpha):
            v1 = jnp.concatenate([vt_ref[hh, j], ones], axis=0)
            acc_ref[hh] = acc_ref[hh] * alpha + _dot(v1, p)

        pending = {hh: s_ref[hh] for hh in range(MLA_LOOKAHEAD)}
        done = {}
        for hh in range(MLA_HPS):
            st = pending.pop(hh)
            ahead = hh + MLA_LOOKAHEAD
            if ahead < MLA_HPS:
                pending[ahead] = scores(j, ahead)
            elif not last:
                s_ref[ahead - MLA_HPS] = scores(j + 1, ahead - MLA_HPS)
            if last:
                kc = lax.broadcasted_iota(jnp.int32, st.shape, 0) // CHUNK
                qc = lax.broadcasted_iota(jnp.int32, st.shape, 1) // CHUNK
                st = jnp.where(kc <= qc, st, NEG_INF)
            m_prev = m_ref[hh]
            m_new = jnp.maximum(m_prev, jnp.max(st, axis=0, keepdims=True))
            m_ref[hh] = m_new
            done[hh] = (jnp.exp2(st - m_new).astype(BF16), jnp.exp2(m_prev - m_new))
            if hh >= MLA_PV_DELAY:
                accumulate(hh - MLA_PV_DELAY, *done.pop(hh - MLA_PV_DELAY))
        for hh in sorted(done):
            accumulate(hh, *done[hh])

    for hh in range(MLA_LOOKAHEAD):
        s_ref[hh] = scores(0, hh)

    def body(k, carry):
        step(2 * k, False)
        step(2 * k + 1, False)
        return carry

    lax.fori_loop(0, i // 2, body, 0)

    @pl.when(i % 2 == 1)
    def _():
        step(i - 1, False)

    step(i, True)
    for pr in range(MLA_HPS // 2):
        ot = jnp.concatenate([acc_ref[hh, :MLA_V, :] / acc_ref[hh, MLA_V:MLA_V + 1, :]
                              for hh in (2 * pr, 2 * pr + 1)], axis=0)
        o_ref[:, pr * LANES:(pr + 1) * LANES] = ot.T.astype(BF16)


def _mla_attention(qt, km, vt):
    b, s, _ = km.shape
    groups = MLA_HEADS // MLA_HPS
    nk = s // MLA_TQ
    return pl.pallas_call(
        _mla_kernel,
        grid=(b, groups, s // MLA_TQ),
        in_specs=[pl.BlockSpec((None, MLA_HPS, HEAD_SLOT, MLA_TQ), lambda bi, g, i: (bi, g, 0, i)),
                  pl.BlockSpec((None, s, MLA_HPS * HEAD_SLOT), lambda bi, g, i: (bi, 0, g)),
                  pl.BlockSpec((None, MLA_HPS, nk, MLA_V, MLA_TQ),
                               lambda bi, g, i: (bi, g, 0, 0, 0))],
        out_specs=pl.BlockSpec((None, MLA_TQ, MLA_HPS * MLA_V), lambda bi, g, i: (bi, i, g)),
        out_shape=jax.ShapeDtypeStruct((b, s, MLA_WIDTH), BF16),
        scratch_shapes=[pltpu.VMEM((MLA_HPS, 1, MLA_TQ), F32),
                        pltpu.VMEM((MLA_HPS, MLA_V + SUM_ROWS, MLA_TQ), F32),
                        pltpu.VMEM((MLA_LOOKAHEAD, MLA_TQ, MLA_TQ), F32)],
        compiler_params=pltpu.CompilerParams(
            dimension_semantics=("arbitrary", "arbitrary", "arbitrary"),
            vmem_limit_bytes=VMEM_LIMIT),
    )(qt, km, vt)


def _ca_kernel(q_ref, k_ref, vt_ref, bias_ref, o_ref):
    lead = KV_PAD - CA_PAD
    low = lax.broadcasted_iota(jnp.int32, (1, LANES), 1) < CA_HEAD_DIM
    ones = jnp.ones((SUM_ROWS, CA_WIN), BF16)
    items = [(sub, h) for sub in range(CA_QBLOCKS) for h in range(CA_HEADS)]

    def block(sub):
        return pl.program_id(1) * CA_QBLOCKS + sub

    def scores(sub, h):
        rows = pl.ds(pl.multiple_of(lead + block(sub) * CA_TQ, BLK), CA_WIN)
        sl = slice((h // 2) * LANES, (h // 2 + 1) * LANES)
        q = q_ref[sub * CA_TQ:(sub + 1) * CA_TQ, sl]
        qh = jnp.where(low if h % 2 == 0 else ~low, q, jnp.zeros_like(q))
        return _dot_nt(k_ref[rows, sl], qh)

    def attend(sub, h, p):
        blk0 = lead // BLK + block(sub)
        vt = jnp.concatenate([vt_ref[h, blk0 + kb] for kb in range(CA_WIN // BLK)], axis=1)
        acc = _dot(jnp.concatenate([vt, ones], axis=0), p)
        return acc[:CA_HEAD_DIM] / acc[CA_HEAD_DIM:CA_HEAD_DIM + 1]

    pending = {it: scores(*it) for it in items[:CA_LOOKAHEAD]}
    probs = {}
    outs = {}

    def flush(sub, h):
        outs[h] = attend(sub, h, probs.pop((sub, h)))
        if h % 2 == 1:
            ot = jnp.concatenate([outs.pop(h - 1), outs.pop(h)], axis=0)
            o_ref[sub * CA_TQ:(sub + 1) * CA_TQ, (h // 2) * LANES:(h // 2 + 1) * LANES] = (
                ot.T.astype(BF16))

    for n, (sub, h) in enumerate(items):
        st = pending.pop((sub, h))
        if n + CA_LOOKAHEAD < len(items):
            ahead = items[n + CA_LOOKAHEAD]
            pending[ahead] = scores(*ahead)
        variant = jnp.minimum(block(sub), CA_VARIANTS - 1)
        st = st + bias_ref[variant, h]
        probs[(sub, h)] = jnp.exp2(st - jnp.max(st, axis=0, keepdims=True)).astype(BF16)
        if n >= 1:
            flush(*items[n - 1])
    flush(*items[-1])


def _chunk_attention(l, qc, kc, vct, bias):
    b, s, _ = qc.shape
    nblk = vct.shape[2]
    tq = CA_QBLOCKS * CA_TQ
    return pl.pallas_call(
        _ca_kernel,
        grid=(b, s // tq),
        in_specs=[pl.BlockSpec((None, tq, CA_WIDTH), lambda bi, i: (bi, i, 0)),
                  pl.BlockSpec((None, s + KV_PAD, CA_WIDTH), lambda bi, i: (bi, 0, 0)),
                  pl.BlockSpec((None, CA_HEADS, nblk, CA_HEAD_DIM, BLK),
                               lambda bi, i: (bi, 0, 0, 0, 0)),
                  pl.BlockSpec((None, CA_VARIANTS, CA_HEADS, CA_WIN, CA_TQ),
                               lambda bi, i: (l, 0, 0, 0, 0), pipeline_mode=pl.Buffered(1))],
        out_specs=pl.BlockSpec((None, tq, CA_WIDTH), lambda bi, i: (bi, i, 0)),
        out_shape=jax.ShapeDtypeStruct((b, s, CA_WIDTH), BF16),
        compiler_params=pltpu.CompilerParams(
            dimension_semantics=("arbitrary", "arbitrary"), vmem_limit_bytes=VMEM_LIMIT),
    )(qc, kc, vct, bias)


def _ffn_kernel(x_ref, oa_ref, ob_ref, goa_ref, gob_ref, wout_ref, gffn_ref, wup_ref, cw_ref,
                cb_ref, wdown_ref, out_ref, carry_ref, u_ref, h_ref):
    tm = TM_FFN

    @pl.when(pl.program_id(1) == 0)
    def _():
        carry_ref[...] = jnp.zeros(carry_ref.shape, F32)

    na = _rms(oa_ref[...].astype(F32), goa_ref[...]).astype(BF16)
    nb = _rms(ob_ref[...].astype(F32), gob_ref[...]).astype(BF16)
    x1 = (x_ref[...] + _dot(na, wout_ref[0:MLA_WIDTH, :])
          + _dot(nb, wout_ref[MLA_WIDTH:MLA_WIDTH + CA_WIDTH, :]))
    out_ref[...] = x1
    h_ref[...] = _rms(x1, gffn_ref[...]).astype(BF16)

    def up(c, slot):
        u = _dot(h_ref[...], wup_ref[c])
        u_ref[slot, 0:CARRY_ROWS, :] = carry_ref[c]
        u_ref[slot, CARRY_ROWS:CARRY_ROWS + tm, :] = u
        carry_ref[c] = u[tm - CARRY_ROWS:, :]

    def act_down(c, slot):
        cw = cw_ref[c]
        y = cb_ref[c]
        for tap in range(CONV_W):
            off = CARRY_ROWS - (CONV_W - 1) + tap
            y = y + u_ref[slot, off:off + tm, :] * cw[tap:tap + 1, :]
        g = y[:, :FF_CHUNK]
        a = g * (1.0 / (1.0 + jnp.exp(-g))) * y[:, FF_CHUNK:]
        out_ref[...] += _dot(a.astype(BF16), wdown_ref[c])

    up(0, 0)

    def body(k, carry):
        c = 2 * k
        up(c + 1, 1)
        act_down(c, 0)
        up(c + 2, 0)
        act_down(c + 1, 1)
        return carry

    lax.fori_loop(0, (N_FF_CHUNKS - 1) // 2, body, 0)
    act_down(N_FF_CHUNKS - 1, 0)


def _outproj_ffn(l, x, oa, ob, wp):
    b, s, _ = x.shape

    def tok(width):
        return pl.BlockSpec((None, TM_FFN, width), lambda bi, i: (bi, i, 0))

    def lw(shape):
        return pl.BlockSpec((None,) + shape, lambda bi, i: (l,) + (0,) * len(shape),
                            pipeline_mode=pl.Buffered(1))

    return pl.pallas_call(
        _ffn_kernel,
        grid=(b, s // TM_FFN),
        in_specs=[tok(D_MODEL), tok(MLA_WIDTH), tok(CA_WIDTH),
                  lw((1, MLA_WIDTH)), lw((1, CA_WIDTH)), lw((D_MODEL, D_MODEL)),
                  lw((1, D_MODEL)), lw((N_FF_CHUNKS, D_MODEL, 2 * FF_CHUNK)),
                  lw((N_FF_CHUNKS, CARRY_ROWS, 2 * FF_CHUNK)),
                  lw((N_FF_CHUNKS, 1, 2 * FF_CHUNK)),
                  lw((N_FF_CHUNKS, FF_CHUNK, D_MODEL))],
        out_specs=tok(D_MODEL),
        out_shape=jax.ShapeDtypeStruct((b, s, D_MODEL), F32),
        scratch_shapes=[pltpu.VMEM((N_FF_CHUNKS, CARRY_ROWS, 2 * FF_CHUNK), F32),
                        pltpu.VMEM((2, CARRY_ROWS + TM_FFN, 2 * FF_CHUNK), F32),
                        pltpu.VMEM((TM_FFN, D_MODEL), BF16)],
        compiler_params=pltpu.CompilerParams(
            dimension_semantics=("arbitrary", "arbitrary"), vmem_limit_bytes=VMEM_LIMIT),
    )(x, oa, ob, wp["g_oa"], wp["g_ob"], wp["w_out"], wp["g_ffn"], wp["w_up"], wp["conv_w"],
      wp["conv_b"], wp["w_down"])


def _rope_slot(first, second, axis_pad):
    t = jnp.concatenate([first, second], axis=-1)
    return jnp.pad(t, axis_pad + ((MLA_NOPE, HEAD_SLOT - MLA_QK),))


def _prepare(g_mix, w_in, w_uq, w_ukv, g_q_lora, g_kv_lora, g_mla_q, g_mla_k, g_ca_q, g_ca_k,
             rel_bias, g_out_mla, g_out_ca, w_out, g_ffn, w_up, conv_w, conv_b, w_down):
    depth = w_in.shape[0]
    row = lambda g: g[:, None, :]
    zpad = HEAD_SLOT - MLA_QK
    r1 = slice(MLA_NOPE, MLA_NOPE + HALF_ROPE)
    r2 = slice(MLA_NOPE + HALF_ROPE, MLA_QK)

    w_kr = w_in[:, :, OFF_KR:OFF_CA]
    kr_slot = _rope_slot(w_kr[..., :HALF_ROPE], w_kr[..., HALF_ROPE:], ((0, 0), (0, 0)))
    kr_swap = _rope_slot(w_kr[..., HALF_ROPE:], w_kr[..., :HALF_ROPE], ((0, 0), (0, 0)))
    w1 = jnp.concatenate([w_in[:, :, OFF_CQ:OFF_KR], kr_slot, kr_swap, w_in[:, :, OFF_CA:]],
                         axis=-1)

    wq = w_uq.reshape(depth, Q_LORA, MLA_HEADS, MLA_QK)
    wq_slot = jnp.pad(wq, ((0, 0), (0, 0), (0, 0), (0, zpad))).reshape(depth, Q_LORA, -1)
    wq_swap = _rope_slot(wq[..., r2], wq[..., r1], ((0, 0), (0, 0), (0, 0)))
    wq2 = jnp.concatenate([wq_slot, wq_swap.reshape(depth, Q_LORA, -1)], axis=-1)
    wkv = w_ukv.reshape(depth, KV_LORA, MLA_HEADS, MLA_NOPE + MLA_V)
    wk = jnp.pad(wkv[..., :MLA_NOPE], ((0, 0), (0, 0), (0, 0), (0, HEAD_SLOT - MLA_NOPE)))
    wk = wk.reshape(depth, KV_LORA, -1)
    wv = wkv[..., MLA_NOPE:].reshape(depth, KV_LORA, MLA_WIDTH)

    gq = g_mla_q * (MLA_QK ** -0.5 * LOG2_E)
    g_q = jnp.pad(gq, ((0, 0), (0, zpad)))
    g_qs = _rope_slot(gq[:, r2], gq[:, r1], ((0, 0),))
    g_k = jnp.pad(g_mla_k, ((0, 0), (0, zpad)))
    g_ks = _rope_slot(g_mla_k[:, r2], g_mla_k[:, r1], ((0, 0),))
    g_caq = jnp.tile(g_ca_q * (CA_HEAD_DIM ** -0.5 * LOG2_E), (1, CA_HEADS))
    g_cak = jnp.tile(g_ca_k, (1, CA_HEADS))
    head = jnp.arange(CA_WIDTH) // CA_HEAD_DIM
    bd = (head[:, None] == head[None, :]).astype(BF16)

    ext = jnp.pad(rel_bias, ((0, 0), (0, 0), (CA_TQ - 1 - REL_CLIP, CA_WIN - REL_CLIP)),
                  mode="edge")
    base = ext[:, :, ::-1].reshape(depth * CA_HEADS, 1, BASE_W)

    def chunks(w):
        g = w[..., :D_FF].reshape(w.shape[:-1] + (N_FF_CHUNKS, FF_CHUNK))
        v = w[..., D_FF:].reshape(w.shape[:-1] + (N_FF_CHUNKS, FF_CHUNK))
        return jnp.concatenate([g, v], axis=-1)

    w_up_c = jnp.moveaxis(chunks(w_up), 2, 1)
    conv_c = jnp.moveaxis(chunks(conv_w), 2, 1)
    conv_c = jnp.pad(conv_c, ((0, 0), (0, 0), (0, CARRY_ROWS - CONV_W), (0, 0)))
    cb_c = chunks(conv_b)[:, :, None, :]
    return {
        "g_mix": row(g_mix), "w1": w1.astype(BF16), "g_qa": row(g_q_lora),
        "g_kva": row(g_kv_lora), "w_uq": wq2.astype(BF16), "w_uk": wk.astype(BF16),
        "w_uv": wv.astype(BF16), "g_q": row(g_q), "g_qs": row(g_qs), "g_k": row(g_k),
        "g_ks": row(g_ks), "g_caq": row(g_caq), "g_cak": row(g_cak), "bd": bd, "base": base,
        "g_oa": row(g_out_mla), "g_ob": row(g_out_ca), "w_out": w_out.astype(BF16),
        "g_ffn": row(g_ffn), "w_up": w_up_c.astype(BF16), "conv_w": conv_c, "conv_b": cb_c,
        "w_down": w_down.reshape(depth, N_FF_CHUNKS, FF_CHUNK, D_MODEL).astype(BF16),
    }


def kernel(x, positions, g_mix, w_in, w_uq, w_ukv, g_q_lora, g_kv_lora, g_mla_q, g_mla_k,
           g_ca_q, g_ca_k, rel_bias, g_out_mla, g_out_ca, w_out, g_ffn, w_up, conv_w, conv_b,
           w_down):
    b, s, _ = x.shape
    depth = w_in.shape[0]
    assert s % TM_IN == 0 and s % TM_FFN == 0 and KV_PAD % TM_IN == 0
    assert (KV_PAD - CA_PAD) % BLK == 0
    wp = _prepare(g_mix, w_in, w_uq, w_ukv, g_q_lora, g_kv_lora, g_mla_q, g_mla_k, g_ca_q,
                  g_ca_k, rel_bias, g_out_mla, g_out_ca, w_out, g_ffn, w_up, conv_w, conv_b,
                  w_down)

    inv = ROPE_BASE ** (-jnp.arange(0, MLA_ROPE, 2, dtype=F32) / MLA_ROPE)
    inv_full = jnp.zeros((1, LANES), F32)
    inv_full = inv_full.at[0, MLA_NOPE:MLA_NOPE + HALF_ROPE].set(inv)
    inv_full = inv_full.at[0, MLA_NOPE + HALF_ROPE:MLA_QK].set(inv)
    pos_f = positions.astype(F32).reshape(b * s, 1)
    cos, sin = [t.reshape(b, s, LANES) for t in _rope_tables(pos_f, inv_full)]
    bias = _bias_tiles(wp["base"], depth)

    kc = jnp.zeros((b, s + KV_PAD, CA_WIDTH), BF16)
    vct = jnp.zeros((b, CA_HEADS, (s + KV_PAD) // BLK, CA_HEAD_DIM, BLK), BF16)
    for l in range(depth):
        qt, km, vt, qc, kc, vct = _inproj(l, x, cos, sin, wp, kc, vct)
        oa = _mla_attention(qt, km, vt)
        ob = _chunk_attention(l, qc, kc, vct, bias)
        x = _outproj_ffn(l, x, oa, ob, wp)
    return x
```

```python
import functools

import jax
import jax.numpy as jnp
from jax import lax
from jax.experimental import pallas as pl
from jax.experimental.pallas import tpu as pltpu

F32 = jnp.float32
BF16 = jnp.bfloat16

D_MODEL = 1024
CHUNK = 64
EPS = 1e-6
NEG_INF = -1e30
LOG2_E = 1.4426950408889634

MLA_HEADS = 8
MLA_NOPE = 64
MLA_ROPE = 32
MLA_QK = MLA_NOPE + MLA_ROPE
MLA_V = 64
Q_LORA = 256
KV_LORA = 128
ROPE_BASE = 10000.0
MLA_WIDTH = MLA_HEADS * MLA_V

CA_HEADS = 8
CA_HEAD_DIM = 64
CA_WIDTH = CA_HEADS * CA_HEAD_DIM
LEFT_CHUNKS = 8
REL_CLIP = 128

OFF_CQ = 0
OFF_CKV = OFF_CQ + Q_LORA
OFF_KR = OFF_CKV + KV_LORA
OFF_CA = OFF_KR + MLA_ROPE

D_FF = 2816
CONV_W = 3

LANES = 128
HEAD_SLOT = LANES
HALF_ROPE = MLA_ROPE // 2
MLA_SLOTS = MLA_HEADS * HEAD_SLOT

P_CQ = 0
P_CKV = P_CQ + Q_LORA
P_KR = P_CKV + KV_LORA
P_KRS = P_KR + HEAD_SLOT
P_CAQ = P_KRS + HEAD_SLOT
P_CAK = P_CAQ + CA_WIDTH
P_CAV = P_CAK + CA_WIDTH
P_COLS = P_CAV + CA_WIDTH

SUM_ROWS = 16
BLK = 256
TM_IN = 1024
CA_PAD = LEFT_CHUNKS * CHUNK
KV_PAD = max(CA_PAD, TM_IN)
CA_TQ = BLK
CA_WIN = CA_PAD + CA_TQ
BASE_W = CA_WIN + CA_TQ
CA_VARIANTS = CA_PAD // CA_TQ + 1
CA_LOOKAHEAD = 3
CA_QBLOCKS = 4
MLA_TQ = BLK
MLA_HPS = 8
MLA_LOOKAHEAD = 4
MLA_PV_DELAY = 1
TM_FFN = 512
FF_CHUNK = 256
N_FF_CHUNKS = D_FF // FF_CHUNK
CARRY_ROWS = 8

V7X_VMEM_BYTES = 64 * 1024 * 1024
VMEM_LIMIT = V7X_VMEM_BYTES * 7 // 8


def _rms(x, g):
    return x * lax.rsqrt(jnp.mean(x * x, axis=-1, keepdims=True) + EPS) * g


def _dot(a, b):
    return jnp.dot(a, b, preferred_element_type=F32)


def _dot_nt(a, b):
    return lax.dot_general(a, b, (((1,), (1,)), ((), ())), preferred_element_type=F32)


def _rope_kernel(pos_ref, inv_ref, c_ref, s_ref):
    ang = pos_ref[...] * inv_ref[...]
    s = jnp.sin(ang)
    lane = lax.broadcasted_iota(jnp.int32, ang.shape, 1)
    first_half = lane < MLA_NOPE + HALF_ROPE
    c_ref[...] = jnp.cos(ang)
    s_ref[...] = jnp.where(first_half, -s, s)


def _rope_tables(pos_f, inv_full):
    t = pos_f.shape[0]
    tm = 1024
    out = jax.ShapeDtypeStruct((t, LANES), F32)
    return pl.pallas_call(
        _rope_kernel,
        grid=(t // tm,),
        in_specs=[pl.BlockSpec((tm, 1), lambda i: (i, 0)),
                  pl.BlockSpec((1, LANES), lambda i: (0, 0))],
        out_specs=[pl.BlockSpec((tm, LANES), lambda i: (i, 0))] * 2,
        out_shape=[out] * 2,
        compiler_params=pltpu.CompilerParams(dimension_semantics=("arbitrary",)),
    )(pos_f, inv_full)


def _bias_kernel(base_ref, out_ref, tile_ref):
    @pl.when(pl.program_id(1) == 0)
    def _():
        x = pltpu.roll(jnp.broadcast_to(base_ref[...], (CA_TQ, BASE_W)), 0, 1,
                       stride=1, stride_axis=0)
        t = x[:, CA_TQ:] * LOG2_E
        i = lax.broadcasted_iota(jnp.int32, (CA_TQ, CA_WIN), 0) // CHUNK
        j = lax.broadcasted_iota(jnp.int32, (CA_TQ, CA_WIN), 1) // CHUNK
        band = (j >= i) & (j <= i + LEFT_CHUNKS)
        tile_ref[...] = jnp.where(band, t, NEG_INF).T

    key = lax.broadcasted_iota(jnp.int32, (CA_WIN, CA_TQ), 0)
    real = key >= CA_PAD - pl.program_id(1) * CA_TQ
    out_ref[...] = jnp.where(real, tile_ref[...], NEG_INF)


def _bias_tiles(base, depth):
    return pl.pallas_call(
        _bias_kernel,
        grid=(depth * CA_HEADS, CA_VARIANTS),
        in_specs=[pl.BlockSpec((None, 1, BASE_W), lambda i, v: (i, 0, 0))],
        out_specs=pl.BlockSpec((None, None, None, CA_WIN, CA_TQ),
                               lambda i, v: (i // CA_HEADS, v, i % CA_HEADS, 0, 0)),
        out_shape=jax.ShapeDtypeStruct((depth, CA_VARIANTS, CA_HEADS, CA_WIN, CA_TQ), F32),
        scratch_shapes=[pltpu.VMEM((CA_WIN, CA_TQ), F32)],
        compiler_params=pltpu.CompilerParams(dimension_semantics=("arbitrary", "arbitrary")),
    )(base)


def _inproj_kernel(x_ref, c_ref, s_ref, gmix_ref, w1_ref, gqa_ref, gkva_ref, wuq_ref, wuk_ref,
                   wuv_ref, gq_ref, gqs_ref, gk_ref, gks_ref, gcaq_ref, gcak_ref, bd_ref,
                   kpad_ref, vpad_ref,
                   qt_ref, km_ref, vt_ref, qc_ref, kc_ref, vct_ref):
    del kpad_ref, vpad_ref
    n_sub = TM_IN // BLK

    def project(r):
        rows = slice(r * BLK, (r + 1) * BLK)
        h = _rms(x_ref[rows, :], gmix_ref[...]).astype(BF16)
        return _dot(h, w1_ref[...])

    def up_project(proj):
        cqn = _rms(proj[:, P_CQ:P_CKV], gqa_ref[...]).astype(BF16)
        ckvn = _rms(proj[:, P_CKV:P_KR], gkva_ref[...]).astype(BF16)
        caq = proj[:, P_CAQ:P_CAK]
        cak = proj[:, P_CAK:P_CAV]
        return {
            "q2": _dot(cqn, wuq_ref[...]),
            "kn": _dot(ckvn, wuk_ref[...]),
            "v": _dot(ckvn, wuv_ref[...]),
            "ssq": _dot((caq * caq).astype(BF16), bd_ref[...]),
            "ssk": _dot((cak * cak).astype(BF16), bd_ref[...]),
        }

    def transpose_heads(t, ref, r):
        for p in range(t.shape[1] // LANES):
            tt = t[:, p * LANES:(p + 1) * LANES].T.astype(BF16)
            ref[2 * p, r] = tt[:LANES // 2]
            ref[2 * p + 1, r] = tt[LANES // 2:]

    def finish(r, proj, up):
        rows = slice(r * BLK, (r + 1) * BLK)
        cos = c_ref[rows, :]
        sin = s_ref[rows, :]
        gq_c, gq_s = gq_ref[...] * cos, gqs_ref[...] * sin
        gk_c, gk_s = gk_ref[...] * cos, gks_ref[...] * sin
        kr = proj[:, P_KR:P_KRS]
        kr_s = proj[:, P_KRS:P_CAQ] * gk_s
        for hd in range(MLA_HEADS):
            sl = slice(hd * HEAD_SLOT, (hd + 1) * HEAD_SLOT)
            ssl = slice(MLA_SLOTS + hd * HEAD_SLOT, MLA_SLOTS + (hd + 1) * HEAD_SLOT)
            q = up["q2"][:, sl]
            rq = lax.rsqrt(jnp.sum(q * q, axis=-1, keepdims=True) * (1.0 / MLA_QK) + EPS)
            qr = (q * gq_c + up["q2"][:, ssl] * gq_s) * rq
            qt_ref[hd, :, rows] = qr.T.astype(BF16)
            k = up["kn"][:, sl] + kr
            rk = lax.rsqrt(jnp.sum(k * k, axis=-1, keepdims=True) * (1.0 / MLA_QK) + EPS)
            km_ref[rows, sl] = ((k * gk_c + kr_s) * rk).astype(BF16)
        transpose_heads(up["v"], vt_ref, r)
        caq = proj[:, P_CAQ:P_CAK]
        cak = proj[:, P_CAK:P_CAV]
        qc_ref[rows, :] = (caq * lax.rsqrt(up["ssq"] * (1.0 / CA_HEAD_DIM) + EPS)
                           * gcaq_ref[...]).astype(BF16)
        kc_ref[rows, :] = (cak * lax.rsqrt(up["ssk"] * (1.0 / CA_HEAD_DIM) + EPS)
                           * gcak_ref[...]).astype(BF16)
        transpose_heads(proj[:, P_CAV:P_COLS], vct_ref, r)

    proj = project(0)
    for r in range(n_sub):
        nxt = project(r + 1) if r + 1 < n_sub else None
        finish(r, proj, up_project(proj))
        proj = nxt


def _inproj(l, x, cos, sin, wp, kpad, vpad):
    b, s, _ = x.shape
    n_sub = TM_IN // BLK
    pad_tiles = KV_PAD // TM_IN

    def tok(width):
        return pl.BlockSpec((None, TM_IN, width), lambda bi, i: (bi, i, 0))

    def lw(shape):
        return pl.BlockSpec((None,) + shape, lambda bi, i: (l,) + (0,) * len(shape),
                            pipeline_mode=pl.Buffered(1))

    const = pl.BlockSpec((CA_WIDTH, CA_WIDTH), lambda bi, i: (0, 0),
                         pipeline_mode=pl.Buffered(1))
    any_spec = pl.BlockSpec(memory_space=pl.ANY)
    heads_t = lambda off: pl.BlockSpec((None, MLA_HEADS, n_sub, LANES // 2, BLK),
                                       lambda bi, i: (bi, 0, i + off, 0, 0))
    tok_shape = lambda w: jax.ShapeDtypeStruct((b, s, w), BF16)
    return pl.pallas_call(
        _inproj_kernel,
        grid=(b, s // TM_IN),
        in_specs=[tok(D_MODEL), tok(LANES), tok(LANES),
                  lw((1, D_MODEL)), lw((D_MODEL, P_COLS)), lw((1, Q_LORA)), lw((1, KV_LORA)),
                  lw((Q_LORA, 2 * MLA_SLOTS)), lw((KV_LORA, MLA_SLOTS)),
                  lw((KV_LORA, MLA_WIDTH)), lw((1, HEAD_SLOT)), lw((1, HEAD_SLOT)),
                  lw((1, HEAD_SLOT)), lw((1, HEAD_SLOT)),
                  lw((1, CA_WIDTH)), lw((1, CA_WIDTH)), const, any_spec, any_spec],
        out_specs=[pl.BlockSpec((None, MLA_HEADS, HEAD_SLOT, TM_IN), lambda bi, i: (bi, 0, 0, i)),
                   tok(MLA_SLOTS), heads_t(0), tok(CA_WIDTH),
                   pl.BlockSpec((None, TM_IN, CA_WIDTH), lambda bi, i: (bi, i + pad_tiles, 0)),
                   heads_t(pad_tiles)],
        out_shape=[jax.ShapeDtypeStruct((b, MLA_HEADS, HEAD_SLOT, s), BF16), tok_shape(MLA_SLOTS),
                   jax.ShapeDtypeStruct((b, MLA_HEADS, s // BLK, MLA_V, BLK), BF16),
                   tok_shape(CA_WIDTH), jax.ShapeDtypeStruct(kpad.shape, kpad.dtype),
                   jax.ShapeDtypeStruct(vpad.shape, vpad.dtype)],
        input_output_aliases={17: 4, 18: 5},
        compiler_params=pltpu.CompilerParams(
            dimension_semantics=("arbitrary", "arbitrary"), vmem_limit_bytes=VMEM_LIMIT),
    )(x, cos, sin, wp["g_mix"], wp["w1"], wp["g_qa"], wp["g_kva"], wp["w_uq"], wp["w_uk"],
      wp["w_uv"], wp["g_q"], wp["g_qs"], wp["g_k"], wp["g_ks"], wp["g_caq"], wp["g_cak"],
      wp["bd"], kpad, vpad)


def _mla_kernel(qt_ref, k_ref, vt_ref, o_ref, m_ref, acc_ref, s_ref):
    i = pl.program_id(2)
    t = MLA_TQ
    ones = jnp.ones((SUM_ROWS, t), BF16)
    m_ref[...] = jnp.full(m_ref.shape, NEG_INF, F32)
    acc_ref[...] = jnp.zeros(acc_ref.shape, F32)

    def scores(j, hh):
        rows = pl.ds(pl.multiple_of(j * t, t), t)
        sl = slice(hh * HEAD_SLOT, (hh + 1) * HEAD_SLOT)
        return _dot(k_ref[rows, sl], qt_ref[hh])

    def step(j, last):
        def accumulate(hh, p, alpha):
            v1 = jnp.concatenate([vt_ref[hh, j], ones], axis=0)
            acc_ref[hh] = acc_ref[hh] * alpha + _dot(v1, p)

        pending = {hh: s_ref[hh] for hh in range(MLA_LOOKAHEAD)}
        done = {}
        for hh in range(MLA_HPS):
            st = pending.pop(hh)
            ahead = hh + MLA_LOOKAHEAD
            if ahead < MLA_HPS:
                pending[ahead] = scores(j, ahead)
            elif not last:
                s_ref[ahead - MLA_HPS] = scores(j + 1, ahead - MLA_HPS)
            if last:
                kc = lax.broadcasted_iota(jnp.int32, st.shape, 0) // CHUNK
                qc = lax.broadcasted_iota(jnp.int32, st.shape, 1) // CHUNK
                st = jnp.where(kc <= qc, st, NEG_INF)
            m_prev = m_ref[hh]
            m_new = jnp.maximum(m_prev, jnp.max(st, axis=0, keepdims=True))
            m_ref[hh] = m_new
            done[hh] = (jnp.exp2(st - m_new).astype(BF16), jnp.exp2(m_prev - m_new))
            if hh >= MLA_PV_DELAY:
                accumulate(hh - MLA_PV_DELAY, *done.pop(hh - MLA_PV_DELAY))
        for hh in sorted(done):
            accumulate(hh, *done[hh])

    for hh in range(MLA_LOOKAHEAD):
        s_ref[hh] = scores(0, hh)

    def body(k, carry):
        step(2 * k, False)
        step(2 * k + 1, False)
        return carry

    lax.fori_loop(0, i // 2, body, 0)

    @pl.when(i % 2 == 1)
    def _():
        step(i - 1, False)

    step(i, True)
    for pr in range(MLA_HPS // 2):
        ot = jnp.concatenate([acc_ref[hh, :MLA_V, :] / acc_ref[hh, MLA_V:MLA_V + 1, :]
                              for hh in (2 * pr, 2 * pr + 1)], axis=0)
        o_ref[:, pr * LANES:(pr + 1) * LANES] = ot.T.astype(BF16)


def _mla_attention(qt, km, vt):
    b, s, _ = km.shape
    groups = MLA_HEADS // MLA_HPS
    nk = s // MLA_TQ
    return pl.pallas_call(
        _mla_kernel,
        grid=(b, groups, s // MLA_TQ),
        in_specs=[pl.BlockSpec((None, MLA_HPS, HEAD_SLOT, MLA_TQ), lambda bi, g, i: (bi, g, 0, i)),
                  pl.BlockSpec((None, s, MLA_HPS * HEAD_SLOT), lambda bi, g, i: (bi, 0, g)),
                  pl.BlockSpec((None, MLA_HPS, nk, MLA_V, MLA_TQ),
                               lambda bi, g, i: (bi, g, 0, 0, 0))],
        out_specs=pl.BlockSpec((None, MLA_TQ, MLA_HPS * MLA_V), lambda bi, g, i: (bi, i, g)),
        out_shape=jax.ShapeDtypeStruct((b, s, MLA_WIDTH), BF16),
        scratch_shapes=[pltpu.VMEM((MLA_HPS, 1, MLA_TQ), F32),
                        pltpu.VMEM((MLA_HPS, MLA_V + SUM_ROWS, MLA_TQ), F32),
                        pltpu.VMEM((MLA_LOOKAHEAD, MLA_TQ, MLA_TQ), F32)],
        compiler_params=pltpu.CompilerParams(
            dimension_semantics=("arbitrary", "arbitrary", "arbitrary"),
            vmem_limit_bytes=VMEM_LIMIT),
    )(qt, km, vt)


def _ca_kernel(q_ref, k_ref, vt_ref, bias_ref, o_ref):
    lead = KV_PAD - CA_PAD
    low = lax.broadcasted_iota(jnp.int32, (1, LANES), 1) < CA_HEAD_DIM
    ones = jnp.ones((SUM_ROWS, BLK), BF16)
    items = [(sub, h) for sub in range(CA_QBLOCKS) for h in range(CA_HEADS)]
    pieces = range(CA_WIN // BLK)

    def block(sub):
        return pl.program_id(1) * CA_QBLOCKS + sub

    def scores(sub, h):
        sl = slice((h // 2) * LANES, (h // 2 + 1) * LANES)
        q = q_ref[sub * CA_TQ:(sub + 1) * CA_TQ, sl]
        qh = jnp.where(low if h % 2 == 0 else ~low, q, jnp.zeros_like(q))
        row0 = lead + block(sub) * CA_TQ
        return [_dot_nt(k_ref[pl.ds(pl.multiple_of(row0 + kb * BLK, BLK), BLK), sl], qh)
                for kb in pieces]

    def attend(sub, h, probs):
        blk0 = lead // BLK + block(sub)
        acc = None
        for kb in pieces:
            v1 = jnp.concatenate([vt_ref[h, blk0 + kb], ones], axis=0)
            part = _dot(v1, probs[kb])
            acc = part if acc is None else acc + part
        return acc[:CA_HEAD_DIM] / acc[CA_HEAD_DIM:CA_HEAD_DIM + 1]

    pending = {it: scores(*it) for it in items[:CA_LOOKAHEAD]}
    probs = {}
    outs = {}

    def flush(sub, h):
        outs[h] = attend(sub, h, probs.pop((sub, h)))
        if h % 2 == 1:
            ot = jnp.concatenate([outs.pop(h - 1), outs.pop(h)], axis=0)
            o_ref[sub * CA_TQ:(sub + 1) * CA_TQ, (h // 2) * LANES:(h // 2 + 1) * LANES] = (
                ot.T.astype(BF16))

    for n, (sub, h) in enumerate(items):
        st = pending.pop((sub, h))
        if n + CA_LOOKAHEAD < len(items):
            ahead = items[n + CA_LOOKAHEAD]
            pending[ahead] = scores(*ahead)
        variant = jnp.minimum(block(sub), CA_VARIANTS - 1)
        st = [st[kb] + bias_ref[variant, h, kb * BLK:(kb + 1) * BLK, :] for kb in pieces]
        m = functools.reduce(jnp.maximum, [jnp.max(t, axis=0, keepdims=True) for t in st])
        probs[(sub, h)] = [jnp.exp2(t - m).astype(BF16) for t in st]
        if n >= 1:
            flush(*items[n - 1])
    flush(*items[-1])


def _chunk_attention(l, qc, kc, vct, bias):
    b, s, _ = qc.shape
    nblk = vct.shape[2]
    tq = CA_QBLOCKS * CA_TQ
    return pl.pallas_call(
        _ca_kernel,
        grid=(b, s // tq),
        in_specs=[pl.BlockSpec((None, tq, CA_WIDTH), lambda bi, i: (bi, i, 0)),
                  pl.BlockSpec((None, s + KV_PAD, CA_WIDTH), lambda bi, i: (bi, 0, 0)),
                  pl.BlockSpec((None, CA_HEADS, nblk, CA_HEAD_DIM, BLK),
                               lambda bi, i: (bi, 0, 0, 0, 0)),
                  pl.BlockSpec((None, CA_VARIANTS, CA_HEADS, CA_WIN, CA_TQ),
                               lambda bi, i: (l, 0, 0, 0, 0), pipeline_mode=pl.Buffered(1))],
        out_specs=pl.BlockSpec((None, tq, CA_WIDTH), lambda bi, i: (bi, i, 0)),
        out_shape=jax.ShapeDtypeStruct((b, s, CA_WIDTH), BF16),
        compiler_params=pltpu.CompilerParams(
            dimension_semantics=("arbitrary", "arbitrary"), vmem_limit_bytes=VMEM_LIMIT),
    )(qc, kc, vct, bias)


def _ffn_kernel(x_ref, oa_ref, ob_ref, goa_ref, gob_ref, wout_ref, gffn_ref, wup_ref, cw_ref,
                cb_ref, wdown_ref, out_ref, carry_ref, u_ref, h_ref):
    tm = TM_FFN

    @pl.when(pl.program_id(1) == 0)
    def _():
        carry_ref[...] = jnp.zeros(carry_ref.shape, F32)

    na = _rms(oa_ref[...].astype(F32), goa_ref[...]).astype(BF16)
    nb = _rms(ob_ref[...].astype(F32), gob_ref[...]).astype(BF16)
    x1 = (x_ref[...] + _dot(na, wout_ref[0:MLA_WIDTH, :])
          + _dot(nb, wout_ref[MLA_WIDTH:MLA_WIDTH + CA_WIDTH, :]))
    out_ref[...] = x1
    h_ref[...] = _rms(x1, gffn_ref[...]).astype(BF16)

    def up(c, slot):
        u = _dot(h_ref[...], wup_ref[c])
        u_ref[slot, 0:CARRY_ROWS, :] = carry_ref[c]
        u_ref[slot, CARRY_ROWS:CARRY_ROWS + tm, :] = u
        carry_ref[c] = u[tm - CARRY_ROWS:, :]

    def act_down(c, slot):
        cw = cw_ref[c]
        y = cb_ref[c]
        for tap in range(CONV_W):
            off = CARRY_ROWS - (CONV_W - 1) + tap
            y = y + u_ref[slot, off:off + tm, :] * cw[tap:tap + 1, :]
        g = y[:, :FF_CHUNK]
        a = g * (1.0 / (1.0 + jnp.exp(-g))) * y[:, FF_CHUNK:]
        out_ref[...] += _dot(a.astype(BF16), wdown_ref[c])

    up(0, 0)

    def body(k, carry):
        c = 2 * k
        up(c + 1, 1)
        act_down(c, 0)
        up(c + 2, 0)
        act_down(c + 1, 1)
        return carry

    lax.fori_loop(0, (N_FF_CHUNKS - 1) // 2, body, 0)
    act_down(N_FF_CHUNKS - 1, 0)


def _outproj_ffn(l, x, oa, ob, wp):
    b, s, _ = x.shape

    def tok(width):
        return pl.BlockSpec((None, TM_FFN, width), lambda bi, i: (bi, i, 0))

    def lw(shape):
        return pl.BlockSpec((None,) + shape, lambda bi, i: (l,) + (0,) * len(shape),
                            pipeline_mode=pl.Buffered(1))

    return pl.pallas_call(
        _ffn_kernel,
        grid=(b, s // TM_FFN),
        in_specs=[tok(D_MODEL), tok(MLA_WIDTH), tok(CA_WIDTH),
                  lw((1, MLA_WIDTH)), lw((1, CA_WIDTH)), lw((D_MODEL, D_MODEL)),
                  lw((1, D_MODEL)), lw((N_FF_CHUNKS, D_MODEL, 2 * FF_CHUNK)),
                  lw((N_FF_CHUNKS, CARRY_ROWS, 2 * FF_CHUNK)),
                  lw((N_FF_CHUNKS, 1, 2 * FF_CHUNK)),
                  lw((N_FF_CHUNKS, FF_CHUNK, D_MODEL))],
        out_specs=tok(D_MODEL),
        out_shape=jax.ShapeDtypeStruct((b, s, D_MODEL), F32),
        scratch_shapes=[pltpu.VMEM((N_FF_CHUNKS, CARRY_ROWS, 2 * FF_CHUNK), F32),
                        pltpu.VMEM((2, CARRY_ROWS + TM_FFN, 2 * FF_CHUNK), F32),
                        pltpu.VMEM((TM_FFN, D_MODEL), BF16)],
        compiler_params=pltpu.CompilerParams(
            dimension_semantics=("arbitrary", "arbitrary"), vmem_limit_bytes=VMEM_LIMIT),
    )(x, oa, ob, wp["g_oa"], wp["g_ob"], wp["w_out"], wp["g_ffn"], wp["w_up"], wp["conv_w"],
      wp["conv_b"], wp["w_down"])


def _rope_slot(first, second, axis_pad):
    t = jnp.concatenate([first, second], axis=-1)
    return jnp.pad(t, axis_pad + ((MLA_NOPE, HEAD_SLOT - MLA_QK),))


def _prepare(g_mix, w_in, w_uq, w_ukv, g_q_lora, g_kv_lora, g_mla_q, g_mla_k, g_ca_q, g_ca_k,
             rel_bias, g_out_mla, g_out_ca, w_out, g_ffn, w_up, conv_w, conv_b, w_down):
    depth = w_in.shape[0]
    row = lambda g: g[:, None, :]
    zpad = HEAD_SLOT - MLA_QK
    r1 = slice(MLA_NOPE, MLA_NOPE + HALF_ROPE)
    r2 = slice(MLA_NOPE + HALF_ROPE, MLA_QK)

    w_kr = w_in[:, :, OFF_KR:OFF_CA]
    kr_slot = _rope_slot(w_kr[..., :HALF_ROPE], w_kr[..., HALF_ROPE:], ((0, 0), (0, 0)))
    kr_swap = _rope_slot(w_kr[..., HALF_ROPE:], w_kr[..., :HALF_ROPE], ((0, 0), (0, 0)))
    w1 = jnp.concatenate([w_in[:, :, OFF_CQ:OFF_KR], kr_slot, kr_swap, w_in[:, :, OFF_CA:]],
                         axis=-1)

    wq = w_uq.reshape(depth, Q_LORA, MLA_HEADS, MLA_QK)
    wq_slot = jnp.pad(wq, ((0, 0), (0, 0), (0, 0), (0, zpad))).reshape(depth, Q_LORA, -1)
    wq_swap = _rope_slot(wq[..., r2], wq[..., r1], ((0, 0), (0, 0), (0, 0)))
    wq2 = jnp.concatenate([wq_slot, wq_swap.reshape(depth, Q_LORA, -1)], axis=-1)
    wkv = w_ukv.reshape(depth, KV_LORA, MLA_HEADS, MLA_NOPE + MLA_V)
    wk = jnp.pad(wkv[..., :MLA_NOPE], ((0, 0), (0, 0), (0, 0), (0, HEAD_SLOT - MLA_NOPE)))
    wk = wk.reshape(depth, KV_LORA, -1)
    wv = wkv[..., MLA_NOPE:].reshape(depth, KV_LORA, MLA_WIDTH)

    gq = g_mla_q * (MLA_QK ** -0.5 * LOG2_E)
    g_q = jnp.pad(gq, ((0, 0), (0, zpad)))
    g_qs = _rope_slot(gq[:, r2], gq[:, r1], ((0, 0),))
    g_k = jnp.pad(g_mla_k, ((0, 0), (0, zpad)))
    g_ks = _rope_slot(g_mla_k[:, r2], g_mla_k[:, r1], ((0, 0),))
    g_caq = jnp.tile(g_ca_q * (CA_HEAD_DIM ** -0.5 * LOG2_E), (1, CA_HEADS))
    g_cak = jnp.tile(g_ca_k, (1, CA_HEADS))
    head = jnp.arange(CA_WIDTH) // CA_HEAD_DIM
    bd = (head[:, None] == head[None, :]).astype(BF16)

    ext = jnp.pad(rel_bias, ((0, 0), (0, 0), (CA_TQ - 1 - REL_CLIP, CA_WIN - REL_CLIP)),
                  mode="edge")
    base = ext[:, :, ::-1].reshape(depth * CA_HEADS, 1, BASE_W)

    def chunks(w):
        g = w[..., :D_FF].reshape(w.shape[:-1] + (N_FF_CHUNKS, FF_CHUNK))
        v = w[..., D_FF:].reshape(w.shape[:-1] + (N_FF_CHUNKS, FF_CHUNK))
        return jnp.concatenate([g, v], axis=-1)

    w_up_c = jnp.moveaxis(chunks(w_up), 2, 1)
    conv_c = jnp.moveaxis(chunks(conv_w), 2, 1)
    conv_c = jnp.pad(conv_c, ((0, 0), (0, 0), (0, CARRY_ROWS - CONV_W), (0, 0)))
    cb_c = chunks(conv_b)[:, :, None, :]
    return {
        "g_mix": row(g_mix), "w1": w1.astype(BF16), "g_qa": row(g_q_lora),
        "g_kva": row(g_kv_lora), "w_uq": wq2.astype(BF16), "w_uk": wk.astype(BF16),
        "w_uv": wv.astype(BF16), "g_q": row(g_q), "g_qs": row(g_qs), "g_k": row(g_k),
        "g_ks": row(g_ks), "g_caq": row(g_caq), "g_cak": row(g_cak), "bd": bd, "base": base,
        "g_oa": row(g_out_mla), "g_ob": row(g_out_ca), "w_out": w_out.astype(BF16),
        "g_ffn": row(g_ffn), "w_up": w_up_c.astype(BF16), "conv_w": conv_c, "conv_b": cb_c,
        "w_down": w_down.reshape(depth, N_FF_CHUNKS, FF_CHUNK, D_MODEL).astype(BF16),
    }


def kernel(x, positions, g_mix, w_in, w_uq, w_ukv, g_q_lora, g_kv_lora, g_mla_q, g_mla_k,
           g_ca_q, g_ca_k, rel_bias, g_out_mla, g_out_ca, w_out, g_ffn, w_up, conv_w, conv_b,
           w_down):
    b, s, _ = x.shape
    depth = w_in.shape[0]
    assert s % TM_IN == 0 and s % TM_FFN == 0 and KV_PAD % TM_IN == 0
    assert (KV_PAD - CA_PAD) % BLK == 0
    wp = _prepare(g_mix, w_in, w_uq, w_ukv, g_q_lora, g_kv_lora, g_mla_q, g_mla_k, g_ca_q,
                  g_ca_k, rel_bias, g_out_mla, g_out_ca, w_out, g_ffn, w_up, conv_w, conv_b,
                  w_down)

    inv = ROPE_BASE ** (-jnp.arange(0, MLA_ROPE, 2, dtype=F32) / MLA_ROPE)
    inv_full = jnp.zeros((1, LANES), F32)
    inv_full = inv_full.at[0, MLA_NOPE:MLA_NOPE + HALF_ROPE].set(inv)
    inv_full = inv_full.at[0, MLA_NOPE + HALF_ROPE:MLA_QK].set(inv)
    pos_f = positions.astype(F32).reshape(b * s, 1)
    cos, sin = [t.reshape(b, s, LANES) for t in _rope_tables(pos_f, inv_full)]
    bias = _bias_tiles(wp["base"], depth)

    kc = jnp.zeros((b, s + KV_PAD, CA_WIDTH), BF16)
    vct = jnp.zeros((b, CA_HEADS, (s + KV_PAD) // BLK, CA_HEAD_DIM, BLK), BF16)
    for l in range(depth):
        qt, km, vt, qc, kc, vct = _inproj(l, x, cos, sin, wp, kc, vct)
        oa = _mla_attention(qt, km, vt)
        ob = _chunk_attention(l, qc, kc, vct, bias)
        x = _outproj_ffn(l, x, oa, ob, wp)
    return x
```

```python
import functools

import jax
import jax.numpy as jnp
from jax import lax
from jax.experimental import pallas as pl
from jax.experimental.pallas import tpu as pltpu

F32 = jnp.float32
BF16 = jnp.bfloat16

D_MODEL = 1024
CHUNK = 64
EPS = 1e-6
NEG_INF = -1e30
LOG2_E = 1.4426950408889634

MLA_HEADS = 8
MLA_NOPE = 64
MLA_ROPE = 32
MLA_QK = MLA_NOPE + MLA_ROPE
MLA_V = 64
Q_LORA = 256
KV_LORA = 128
ROPE_BASE = 10000.0
MLA_WIDTH = MLA_HEADS * MLA_V

CA_HEADS = 8
CA_HEAD_DIM = 64
CA_WIDTH = CA_HEADS * CA_HEAD_DIM
LEFT_CHUNKS = 8
REL_CLIP = 128

OFF_CQ = 0
OFF_CKV = OFF_CQ + Q_LORA
OFF_KR = OFF_CKV + KV_LORA
OFF_CA = OFF_KR + MLA_ROPE

D_FF = 2816
CONV_W = 3

LANES = 128
HEAD_SLOT = LANES
HALF_ROPE = MLA_ROPE // 2
BD_WIDTH = 256
V_DIM = MLA_V
assert V_DIM == CA_HEAD_DIM and MLA_HEADS == CA_HEADS and 2 * V_DIM == LANES
MLA_SLOTS = MLA_HEADS * HEAD_SLOT

P_CQ = 0
P_CKV = P_CQ + Q_LORA
P_KR = P_CKV + KV_LORA
P_KRS = P_KR + HEAD_SLOT
P_CAQ = P_KRS + HEAD_SLOT
P_CAK = P_CAQ + CA_WIDTH
P_CAV = P_CAK + CA_WIDTH
P_COLS = P_CAV + CA_WIDTH

SUM_ROWS = 16
BLK = 256
TM_IN = 1024
CA_PAD = LEFT_CHUNKS * CHUNK
KV_PAD = max(CA_PAD, TM_IN)
CA_TQ = BLK
CA_WIN = CA_PAD + CA_TQ
BASE_W = CA_WIN + CA_TQ
CA_VARIANTS = CA_PAD // CA_TQ + 1
CA_LOOKAHEAD = 3
CA_QBLOCKS = 4
MLA_TQ = BLK
MLA_HPS = 8
MLA_LOOKAHEAD = 4
MLA_PV_DELAY = 1
TM_FFN = 512
FF_CHUNK = 256
N_FF_CHUNKS = D_FF // FF_CHUNK
CARRY_ROWS = 8

V7X_VMEM_BYTES = 64 * 1024 * 1024
VMEM_LIMIT = V7X_VMEM_BYTES * 7 // 8


def _rms(x, g):
    return x * lax.rsqrt(jnp.mean(x * x, axis=-1, keepdims=True) + EPS) * g


def _dot(a, b):
    return jnp.dot(a, b, preferred_element_type=F32)


def _dot_nt(a, b):
    return lax.dot_general(a, b, (((1,), (1,)), ((), ())), preferred_element_type=F32)


def _rope_kernel(pos_ref, inv_ref, c_ref, s_ref):
    ang = pos_ref[...] * inv_ref[...]
    s = jnp.sin(ang)
    lane = lax.broadcasted_iota(jnp.int32, ang.shape, 1)
    first_half = lane < MLA_NOPE + HALF_ROPE
    c_ref[...] = jnp.cos(ang)
    s_ref[...] = jnp.where(first_half, -s, s)


def _rope_tables(pos_f, inv_full):
    t = pos_f.shape[0]
    tm = 1024
    out = jax.ShapeDtypeStruct((t, LANES), F32)
    return pl.pallas_call(
        _rope_kernel,
        grid=(t // tm,),
        in_specs=[pl.BlockSpec((tm, 1), lambda i: (i, 0)),
                  pl.BlockSpec((1, LANES), lambda i: (0, 0))],
        out_specs=[pl.BlockSpec((tm, LANES), lambda i: (i, 0))] * 2,
        out_shape=[out] * 2,
        compiler_params=pltpu.CompilerParams(dimension_semantics=("arbitrary",)),
    )(pos_f, inv_full)


def _bias_kernel(base_ref, out_ref, tile_ref):
    @pl.when(pl.program_id(1) == 0)
    def _():
        x = pltpu.roll(jnp.broadcast_to(base_ref[...], (CA_TQ, BASE_W)), 0, 1,
                       stride=1, stride_axis=0)
        t = x[:, CA_TQ:] * LOG2_E
        i = lax.broadcasted_iota(jnp.int32, (CA_TQ, CA_WIN), 0) // CHUNK
        j = lax.broadcasted_iota(jnp.int32, (CA_TQ, CA_WIN), 1) // CHUNK
        band = (j >= i) & (j <= i + LEFT_CHUNKS)
        tile_ref[...] = jnp.where(band, t, NEG_INF).T

    key = lax.broadcasted_iota(jnp.int32, (CA_WIN, CA_TQ), 0)
    real = key >= CA_PAD - pl.program_id(1) * CA_TQ
    out_ref[...] = jnp.where(real, tile_ref[...], NEG_INF)


def _bias_tiles(base, depth):
    return pl.pallas_call(
        _bias_kernel,
        grid=(depth * CA_HEADS, CA_VARIANTS),
        in_specs=[pl.BlockSpec((None, 1, BASE_W), lambda i, v: (i, 0, 0))],
        out_specs=pl.BlockSpec((None, None, None, CA_WIN, CA_TQ),
                               lambda i, v: (i // CA_HEADS, v, i % CA_HEADS, 0, 0)),
        out_shape=jax.ShapeDtypeStruct((depth, CA_VARIANTS, CA_HEADS, CA_WIN, CA_TQ), F32),
        scratch_shapes=[pltpu.VMEM((CA_WIN, CA_TQ), F32)],
        compiler_params=pltpu.CompilerParams(dimension_semantics=("arbitrary", "arbitrary")),
    )(base)


def _inproj_kernel(x_ref, c_ref, s_ref, gmix_ref, w1_ref, gqa_ref, gkva_ref, wuq_ref, wuk_ref,
                   wuv_ref, gq_ref, gqs_ref, gk_ref, gks_ref, gcaq_ref, gcak_ref, bd_ref,
                   kpad_ref, vpad_ref,
                   qt_ref, km_ref, vt_ref, qc_ref, kc_ref, vct_ref):
    del kpad_ref, vpad_ref
    n_sub = TM_IN // BLK

    def project(r):
        rows = slice(r * BLK, (r + 1) * BLK)
        h = _rms(x_ref[rows, :], gmix_ref[...]).astype(BF16)
        return _dot(h, w1_ref[...])

    def head_sums(t):
        sq = (t * t).astype(BF16)
        return jnp.concatenate([_dot(sq[:, c:c + BD_WIDTH], bd_ref[...])
                                for c in range(0, CA_WIDTH, BD_WIDTH)], axis=1)

    def up_project(proj):
        cqn = _rms(proj[:, P_CQ:P_CKV], gqa_ref[...]).astype(BF16)
        ckvn = _rms(proj[:, P_CKV:P_KR], gkva_ref[...]).astype(BF16)
        caq = proj[:, P_CAQ:P_CAK]
        cak = proj[:, P_CAK:P_CAV]
        return {
            "q2": _dot(cqn, wuq_ref[...]),
            "kn": _dot(ckvn, wuk_ref[...]),
            "v": _dot(ckvn, wuv_ref[...]),
            "ssq": head_sums(caq),
            "ssk": head_sums(cak),
        }

    def transpose_heads(t, ref, r):
        for p in range(t.shape[1] // LANES):
            tt = t[:, p * LANES:(p + 1) * LANES].T.astype(BF16)
            ref[2 * p, r] = tt[:V_DIM]
            ref[2 * p + 1, r] = tt[V_DIM:]

    def finish(r, proj, up):
        rows = slice(r * BLK, (r + 1) * BLK)
        cos = c_ref[rows, :]
        sin = s_ref[rows, :]
        gq_c, gq_s = gq_ref[...] * cos, gqs_ref[...] * sin
        gk_c, gk_s = gk_ref[...] * cos, gks_ref[...] * sin
        kr = proj[:, P_KR:P_KRS]
        kr_s = proj[:, P_KRS:P_CAQ] * gk_s
        for hd in range(MLA_HEADS):
            sl = slice(hd * HEAD_SLOT, (hd + 1) * HEAD_SLOT)
            ssl = slice(MLA_SLOTS + hd * HEAD_SLOT, MLA_SLOTS + (hd + 1) * HEAD_SLOT)
            q = up["q2"][:, sl]
            rq = lax.rsqrt(jnp.sum(q * q, axis=-1, keepdims=True) * (1.0 / MLA_QK) + EPS)
            qr = (q * gq_c + up["q2"][:, ssl] * gq_s) * rq
            qt_ref[hd, :, rows] = qr.T.astype(BF16)
            k = up["kn"][:, sl] + kr
            rk = lax.rsqrt(jnp.sum(k * k, axis=-1, keepdims=True) * (1.0 / MLA_QK) + EPS)
            km_ref[rows, sl] = ((k * gk_c + kr_s) * rk).astype(BF16)
        transpose_heads(up["v"], vt_ref, r)
        caq = proj[:, P_CAQ:P_CAK]
        cak = proj[:, P_CAK:P_CAV]
        qc_ref[rows, :] = (caq * lax.rsqrt(up["ssq"] * (1.0 / CA_HEAD_DIM) + EPS)
                           * gcaq_ref[...]).astype(BF16)
        kc_ref[rows, :] = (cak * lax.rsqrt(up["ssk"] * (1.0 / CA_HEAD_DIM) + EPS)
                           * gcak_ref[...]).astype(BF16)
        transpose_heads(proj[:, P_CAV:P_COLS], vct_ref, r)

    proj = project(0)
    for r in range(n_sub):
        nxt = project(r + 1) if r + 1 < n_sub else None
        finish(r, proj, up_project(proj))
        proj = nxt


def _inproj(l, x, cos, sin, wp, kpad, vpad):
    b, s, _ = x.shape
    n_sub = TM_IN // BLK
    pad_tiles = KV_PAD // TM_IN

    def tok(width):
        return pl.BlockSpec((None, TM_IN, width), lambda bi, i: (bi, i, 0))

    def lw(shape):
        return pl.BlockSpec((None,) + shape, lambda bi, i: (l,) + (0,) * len(shape),
                            pipeline_mode=pl.Buffered(1))

    const = pl.BlockSpec((BD_WIDTH, BD_WIDTH), lambda bi, i: (0, 0),
                         pipeline_mode=pl.Buffered(1))
    any_spec = pl.BlockSpec(memory_space=pl.ANY)
    heads_t = lambda off: pl.BlockSpec((None, MLA_HEADS, n_sub, V_DIM, BLK),
                                       lambda bi, i: (bi, 0, i + off, 0, 0))
    tok_shape = lambda w: jax.ShapeDtypeStruct((b, s, w), BF16)
    return pl.pallas_call(
        _inproj_kernel,
        grid=(b, s // TM_IN),
        in_specs=[tok(D_MODEL), tok(LANES), tok(LANES),
                  lw((1, D_MODEL)), lw((D_MODEL, P_COLS)), lw((1, Q_LORA)), lw((1, KV_LORA)),
                  lw((Q_LORA, 2 * MLA_SLOTS)), lw((KV_LORA, MLA_SLOTS)),
                  lw((KV_LORA, MLA_WIDTH)), lw((1, HEAD_SLOT)), lw((1, HEAD_SLOT)),
                  lw((1, HEAD_SLOT)), lw((1, HEAD_SLOT)),
                  lw((1, CA_WIDTH)), lw((1, CA_WIDTH)), const, any_spec, any_spec],
        out_specs=[pl.BlockSpec((None, MLA_HEADS, HEAD_SLOT, TM_IN), lambda bi, i: (bi, 0, 0, i)),
                   tok(MLA_SLOTS), heads_t(0), tok(CA_WIDTH),
                   pl.BlockSpec((None, TM_IN, CA_WIDTH), lambda bi, i: (bi, i + pad_tiles, 0)),
                   heads_t(pad_tiles)],
        out_shape=[jax.ShapeDtypeStruct((b, MLA_HEADS, HEAD_SLOT, s), BF16), tok_shape(MLA_SLOTS),
                   jax.ShapeDtypeStruct((b, MLA_HEADS, s // BLK, MLA_V, BLK), BF16),
                   tok_shape(CA_WIDTH), jax.ShapeDtypeStruct(kpad.shape, kpad.dtype),
                   jax.ShapeDtypeStruct(vpad.shape, vpad.dtype)],
        input_output_aliases={17: 4, 18: 5},
        compiler_params=pltpu.CompilerParams(
            dimension_semantics=("arbitrary", "arbitrary"), vmem_limit_bytes=VMEM_LIMIT),
    )(x, cos, sin, wp["g_mix"], wp["w1"], wp["g_qa"], wp["g_kva"], wp["w_uq"], wp["w_uk"],
      wp["w_uv"], wp["g_q"], wp["g_qs"], wp["g_k"], wp["g_ks"], wp["g_caq"], wp["g_cak"],
      wp["bd"], kpad, vpad)


def _mla_kernel(qt_ref, k_ref, vt_ref, o_ref, m_ref, acc_ref, s_ref):
    i = pl.program_id(2)
    t = MLA_TQ
    ones = jnp.ones((SUM_ROWS, t), BF16)
    m_ref[...] = jnp.full(m_ref.shape, NEG_INF, F32)
    acc_ref[...] = jnp.zeros(acc_ref.shape, F32)

    def scores(j, hh):
        rows = pl.ds(pl.multiple_of(j * t, t), t)
        sl = slice(hh * HEAD_SLOT, (hh + 1) * HEAD_SLOT)
        return _dot(k_ref[rows, sl], qt_ref[hh])

    def step(j, last):
        def accumulate(hh, p, alpha):
            v1 = jnp.concatenate([vt_ref[hh, j], ones], axis=0)
            acc_ref[hh] = acc_ref[hh] * alpha + _dot(v1, p)

        pending = {hh: s_ref[hh] for hh in range(MLA_LOOKAHEAD)}
        done = {}
        for hh in range(MLA_HPS):
            st = pending.pop(hh)
            ahead = hh + MLA_LOOKAHEAD
            if ahead < MLA_HPS:
                pending[ahead] = scores(j, ahead)
            elif not last:
                s_ref[ahead - MLA_HPS] = scores(j + 1, ahead - MLA_HPS)
            if last:
                kc = lax.broadcasted_iota(jnp.int32, st.shape, 0) // CHUNK
                qc = lax.broadcasted_iota(jnp.int32, st.shape, 1) // CHUNK
                st = jnp.where(kc <= qc, st, NEG_INF)
            m_prev = m_ref[hh]
            m_new = jnp.maximum(m_prev, jnp.max(st, axis=0, keepdims=True))
            m_ref[hh] = m_new
            done[hh] = (jnp.exp2(st - m_new).astype(BF16), jnp.exp2(m_prev - m_new))
            if hh >= MLA_PV_DELAY:
                accumulate(hh - MLA_PV_DELAY, *done.pop(hh - MLA_PV_DELAY))
        for hh in sorted(done):
            accumulate(hh, *done[hh])

    for hh in range(MLA_LOOKAHEAD):
        s_ref[hh] = scores(0, hh)

    def body(k, carry):
        step(2 * k, False)
        step(2 * k + 1, False)
        return carry

    lax.fori_loop(0, i // 2, body, 0)

    @pl.when(i % 2 == 1)
    def _():
        step(i - 1, False)

    step(i, True)
    for pr in range(MLA_HPS // 2):
        ot = jnp.concatenate([acc_ref[hh, :MLA_V, :] / acc_ref[hh, MLA_V:MLA_V + 1, :]
                              for hh in (2 * pr, 2 * pr + 1)], axis=0)
        o_ref[:, pr * LANES:(pr + 1) * LANES] = ot.T.astype(BF16)


def _mla_attention(qt, km, vt):
    b, s, _ = km.shape
    groups = MLA_HEADS // MLA_HPS
    nk = s // MLA_TQ
    return pl.pallas_call(
        _mla_kernel,
        grid=(b, groups, s // MLA_TQ),
        in_specs=[pl.BlockSpec((None, MLA_HPS, HEAD_SLOT, MLA_TQ), lambda bi, g, i: (bi, g, 0, i)),
                  pl.BlockSpec((None, s, MLA_HPS * HEAD_SLOT), lambda bi, g, i: (bi, 0, g)),
                  pl.BlockSpec((None, MLA_HPS, nk, MLA_V, MLA_TQ),
                               lambda bi, g, i: (bi, g, 0, 0, 0))],
        out_specs=pl.BlockSpec((None, MLA_TQ, MLA_HPS * MLA_V), lambda bi, g, i: (bi, i, g)),
        out_shape=jax.ShapeDtypeStruct((b, s, MLA_WIDTH), BF16),
        scratch_shapes=[pltpu.VMEM((MLA_HPS, 1, MLA_TQ), F32),
                        pltpu.VMEM((MLA_HPS, MLA_V + SUM_ROWS, MLA_TQ), F32),
                        pltpu.VMEM((MLA_LOOKAHEAD, MLA_TQ, MLA_TQ), F32)],
        compiler_params=pltpu.CompilerParams(
            dimension_semantics=("arbitrary", "arbitrary", "arbitrary"),
            vmem_limit_bytes=VMEM_LIMIT),
    )(qt, km, vt)


def _ca_kernel(q_ref, k_ref, vt_ref, bias_ref, o_ref):
    lead = KV_PAD - CA_PAD
    low = lax.broadcasted_iota(jnp.int32, (1, LANES), 1) < CA_HEAD_DIM
    ones = jnp.ones((SUM_ROWS, BLK), BF16)
    items = [(sub, h) for sub in range(CA_QBLOCKS) for h in range(CA_HEADS)]
    pieces = range(CA_WIN // BLK)

    def block(sub):
        return pl.program_id(1) * CA_QBLOCKS + sub

    def scores(sub, h):
        sl = slice((h // 2) * LANES, (h // 2 + 1) * LANES)
        q = q_ref[sub * CA_TQ:(sub + 1) * CA_TQ, sl]
        qh = jnp.where(low if h % 2 == 0 else ~low, q, jnp.zeros_like(q))
        row0 = lead + block(sub) * CA_TQ
        return [_dot_nt(k_ref[pl.ds(pl.multiple_of(row0 + kb * BLK, BLK), BLK), sl], qh)
                for kb in pieces]

    def attend(sub, h, probs):
        blk0 = lead // BLK + block(sub)
        acc = None
        for kb in pieces:
            v1 = jnp.concatenate([vt_ref[h, blk0 + kb], ones], axis=0)
            part = _dot(v1, probs[kb])
            acc = part if acc is None else acc + part
        return acc[:CA_HEAD_DIM] / acc[CA_HEAD_DIM:CA_HEAD_DIM + 1]

    pending = {it: scores(*it) for it in items[:CA_LOOKAHEAD]}
    probs = {}
    outs = {}

    def flush(sub, h):
        outs[h] = attend(sub, h, probs.pop((sub, h)))
        if h % 2 == 1:
            ot = jnp.concatenate([outs.pop(h - 1), outs.pop(h)], axis=0)
            o_ref[sub * CA_TQ:(sub + 1) * CA_TQ, (h // 2) * LANES:(h // 2 + 1) * LANES] = (
                ot.T.astype(BF16))

    for n, (sub, h) in enumerate(items):
        st = pending.pop((sub, h))
        if n + CA_LOOKAHEAD < len(items):
            ahead = items[n + CA_LOOKAHEAD]
            pending[ahead] = scores(*ahead)
        variant = jnp.minimum(block(sub), CA_VARIANTS - 1)
        st = [st[kb] + bias_ref[variant, h, kb * BLK:(kb + 1) * BLK, :] for kb in pieces]
        m = functools.reduce(jnp.maximum, [jnp.max(t, axis=0, keepdims=True) for t in st])
        probs[(sub, h)] = [jnp.exp2(t - m).astype(BF16) for t in st]
        if n >= 1:
            flush(*items[n - 1])
    flush(*items[-1])


def _chunk_attention(l, qc, kc, vct, bias):
    b, s, _ = qc.shape
    nblk = vct.shape[2]
    tq = CA_QBLOCKS * CA_TQ
    return pl.pallas_call(
        _ca_kernel,
        grid=(b, s // tq),
        in_specs=[pl.BlockSpec((None, tq, CA_WIDTH), lambda bi, i: (bi, i, 0)),
                  pl.BlockSpec((None, s + KV_PAD, CA_WIDTH), lambda bi, i: (bi, 0, 0)),
                  pl.BlockSpec((None, CA_HEADS, nblk, CA_HEAD_DIM, BLK),
                               lambda bi, i: (bi, 0, 0, 0, 0)),
                  pl.BlockSpec((None, CA_VARIANTS, CA_HEADS, CA_WIN, CA_TQ),
                               lambda bi, i: (l, 0, 0, 0, 0), pipeline_mode=pl.Buffered(1))],
        out_specs=pl.BlockSpec((None, tq, CA_WIDTH), lambda bi, i: (bi, i, 0)),
        out_shape=jax.ShapeDtypeStruct((b, s, CA_WIDTH), BF16),
        compiler_params=pltpu.CompilerParams(
            dimension_semantics=("arbitrary", "arbitrary"), vmem_limit_bytes=VMEM_LIMIT),
    )(qc, kc, vct, bias)


def _ffn_kernel(x_ref, oa_ref, ob_ref, goa_ref, gob_ref, wout_ref, gffn_ref, wup_ref, cw_ref,
                cb_ref, wdown_ref, out_ref, carry_ref, u_ref, h_ref):
    tm = TM_FFN

    @pl.when(pl.program_id(1) == 0)
    def _():
        carry_ref[...] = jnp.zeros(carry_ref.shape, F32)

    na = _rms(oa_ref[...].astype(F32), goa_ref[...]).astype(BF16)
    nb = _rms(ob_ref[...].astype(F32), gob_ref[...]).astype(BF16)
    x1 = (x_ref[...] + _dot(na, wout_ref[0:MLA_WIDTH, :])
          + _dot(nb, wout_ref[MLA_WIDTH:MLA_WIDTH + CA_WIDTH, :]))
    out_ref[...] = x1
    h_ref[...] = _rms(x1, gffn_ref[...]).astype(BF16)

    def up(c, slot):
        u = _dot(h_ref[...], wup_ref[c])
        u_ref[slot, 0:CARRY_ROWS, :] = carry_ref[c]
        u_ref[slot, CARRY_ROWS:CARRY_ROWS + tm, :] = u
        carry_ref[c] = u[tm - CARRY_ROWS:, :]

    def act_down(c, slot):
        cw = cw_ref[c]
        y = cb_ref[c]
        for tap in range(CONV_W):
            off = CARRY_ROWS - (CONV_W - 1) + tap
            y = y + u_ref[slot, off:off + tm, :] * cw[tap:tap + 1, :]
        g = y[:, :FF_CHUNK]
        a = g * (1.0 / (1.0 + jnp.exp(-g))) * y[:, FF_CHUNK:]
        out_ref[...] += _dot(a.astype(BF16), wdown_ref[c])

    up(0, 0)

    def body(k, carry):
        c = 2 * k
        up(c + 1, 1)
        act_down(c, 0)
        up(c + 2, 0)
        act_down(c + 1, 1)
        return carry

    lax.fori_loop(0, (N_FF_CHUNKS - 1) // 2, body, 0)
    act_down(N_FF_CHUNKS - 1, 0)


def _outproj_ffn(l, x, oa, ob, wp):
    b, s, _ = x.shape

    def tok(width):
        return pl.BlockSpec((None, TM_FFN, width), lambda bi, i: (bi, i, 0))

    def lw(shape):
        return pl.BlockSpec((None,) + shape, lambda bi, i: (l,) + (0,) * len(shape),
                            pipeline_mode=pl.Buffered(1))

    return pl.pallas_call(
        _ffn_kernel,
        grid=(b, s // TM_FFN),
        in_specs=[tok(D_MODEL), tok(MLA_WIDTH), tok(CA_WIDTH),
                  lw((1, MLA_WIDTH)), lw((1, CA_WIDTH)), lw((D_MODEL, D_MODEL)),
                  lw((1, D_MODEL)), lw((N_FF_CHUNKS, D_MODEL, 2 * FF_CHUNK)),
                  lw((N_FF_CHUNKS, CARRY_ROWS, 2 * FF_CHUNK)),
                  lw((N_FF_CHUNKS, 1, 2 * FF_CHUNK)),
                  lw((N_FF_CHUNKS, FF_CHUNK, D_MODEL))],
        out_specs=tok(D_MODEL),
        out_shape=jax.ShapeDtypeStruct((b, s, D_MODEL), F32),
        scratch_shapes=[pltpu.VMEM((N_FF_CHUNKS, CARRY_ROWS, 2 * FF_CHUNK), F32),
                        pltpu.VMEM((2, CARRY_ROWS + TM_FFN, 2 * FF_CHUNK), F32),
                        pltpu.VMEM((TM_FFN, D_MODEL), BF16)],
        compiler_params=pltpu.CompilerParams(
            dimension_semantics=("arbitrary", "arbitrary"), vmem_limit_bytes=VMEM_LIMIT),
    )(x, oa, ob, wp["g_oa"], wp["g_ob"], wp["w_out"], wp["g_ffn"], wp["w_up"], wp["conv_w"],
      wp["conv_b"], wp["w_down"])


def _rope_slot(first, second, axis_pad):
    t = jnp.concatenate([first, second], axis=-1)
    return jnp.pad(t, axis_pad + ((MLA_NOPE, HEAD_SLOT - MLA_QK),))


def _prepare(g_mix, w_in, w_uq, w_ukv, g_q_lora, g_kv_lora, g_mla_q, g_mla_k, g_ca_q, g_ca_k,
             rel_bias, g_out_mla, g_out_ca, w_out, g_ffn, w_up, conv_w, conv_b, w_down):
    depth = w_in.shape[0]
    row = lambda g: g[:, None, :]
    zpad = HEAD_SLOT - MLA_QK
    r1 = slice(MLA_NOPE, MLA_NOPE + HALF_ROPE)
    r2 = slice(MLA_NOPE + HALF_ROPE, MLA_QK)

    w_kr = w_in[:, :, OFF_KR:OFF_CA]
    kr_slot = _rope_slot(w_kr[..., :HALF_ROPE], w_kr[..., HALF_ROPE:], ((0, 0), (0, 0)))
    kr_swap = _rope_slot(w_kr[..., HALF_ROPE:], w_kr[..., :HALF_ROPE], ((0, 0), (0, 0)))
    w1 = jnp.concatenate([w_in[:, :, OFF_CQ:OFF_KR], kr_slot, kr_swap, w_in[:, :, OFF_CA:]],
                         axis=-1)

    wq = w_uq.reshape(depth, Q_LORA, MLA_HEADS, MLA_QK)
    wq_slot = jnp.pad(wq, ((0, 0), (0, 0), (0, 0), (0, zpad))).reshape(depth, Q_LORA, -1)
    wq_swap = _rope_slot(wq[..., r2], wq[..., r1], ((0, 0), (0, 0), (0, 0)))
    wq2 = jnp.concatenate([wq_slot, wq_swap.reshape(depth, Q_LORA, -1)], axis=-1)
    wkv = w_ukv.reshape(depth, KV_LORA, MLA_HEADS, MLA_NOPE + MLA_V)
    wk = jnp.pad(wkv[..., :MLA_NOPE], ((0, 0), (0, 0), (0, 0), (0, HEAD_SLOT - MLA_NOPE)))
    wk = wk.reshape(depth, KV_LORA, -1)
    wv = wkv[..., MLA_NOPE:].reshape(depth, KV_LORA, MLA_WIDTH)

    gq = g_mla_q * (MLA_QK ** -0.5 * LOG2_E)
    g_q = jnp.pad(gq, ((0, 0), (0, zpad)))
    g_qs = _rope_slot(gq[:, r2], gq[:, r1], ((0, 0),))
    g_k = jnp.pad(g_mla_k, ((0, 0), (0, zpad)))
    g_ks = _rope_slot(g_mla_k[:, r2], g_mla_k[:, r1], ((0, 0),))
    g_caq = jnp.tile(g_ca_q * (CA_HEAD_DIM ** -0.5 * LOG2_E), (1, CA_HEADS))
    g_cak = jnp.tile(g_ca_k, (1, CA_HEADS))
    head = jnp.arange(BD_WIDTH) // CA_HEAD_DIM
    bd = (head[:, None] == head[None, :]).astype(BF16)

    ext = jnp.pad(rel_bias, ((0, 0), (0, 0), (CA_TQ - 1 - REL_CLIP, CA_WIN - REL_CLIP)),
                  mode="edge")
    base = ext[:, :, ::-1].reshape(depth * CA_HEADS, 1, BASE_W)

    def chunks(w):
        g = w[..., :D_FF].reshape(w.shape[:-1] + (N_FF_CHUNKS, FF_CHUNK))
        v = w[..., D_FF:].reshape(w.shape[:-1] + (N_FF_CHUNKS, FF_CHUNK))
        return jnp.concatenate([g, v], axis=-1)

    w_up_c = jnp.moveaxis(chunks(w_up), 2, 1)
    conv_c = jnp.moveaxis(chunks(conv_w), 2, 1)
    conv_c = jnp.pad(conv_c, ((0, 0), (0, 0), (0, CARRY_ROWS - CONV_W), (0, 0)))
    cb_c = chunks(conv_b)[:, :, None, :]
    return {
        "g_mix": row(g_mix), "w1": w1.astype(BF16), "g_qa": row(g_q_lora),
        "g_kva": row(g_kv_lora), "w_uq": wq2.astype(BF16), "w_uk": wk.astype(BF16),
        "w_uv": wv.astype(BF16), "g_q": row(g_q), "g_qs": row(g_qs), "g_k": row(g_k),
        "g_ks": row(g_ks), "g_caq": row(g_caq), "g_cak": row(g_cak), "bd": bd, "base": base,
        "g_oa": row(g_out_mla), "g_ob": row(g_out_ca), "w_out": w_out.astype(BF16),
        "g_ffn": row(g_ffn), "w_up": w_up_c.astype(BF16), "conv_w": conv_c, "conv_b": cb_c,
        "w_down": w_down.reshape(depth, N_FF_CHUNKS, FF_CHUNK, D_MODEL).astype(BF16),
    }


def kernel(x, positions, g_mix, w_in, w_uq, w_ukv, g_q_lora, g_kv_lora, g_mla_q, g_mla_k,
           g_ca_q, g_ca_k, rel_bias, g_out_mla, g_out_ca, w_out, g_ffn, w_up, conv_w, conv_b,
           w_down):
    b, s, _ = x.shape
    depth = w_in.shape[0]
    assert s % TM_IN == 0 and s % TM_FFN == 0 and KV_PAD % TM_IN == 0
    assert (KV_PAD - CA_PAD) % BLK == 0
    wp = _prepare(g_mix, w_in, w_uq, w_ukv, g_q_lora, g_kv_lora, g_mla_q, g_mla_k, g_ca_q,
                  g_ca_k, rel_bias, g_out_mla, g_out_ca, w_out, g_ffn, w_up, conv_w, conv_b,
                  w_down)

    inv = ROPE_BASE ** (-jnp.arange(0, MLA_ROPE, 2, dtype=F32) / MLA_ROPE)
    inv_full = jnp.zeros((1, LANES), F32)
    inv_full = inv_full.at[0, MLA_NOPE:MLA_NOPE + HALF_ROPE].set(inv)
    inv_full = inv_full.at[0, MLA_NOPE + HALF_ROPE:MLA_QK].set(inv)
    pos_f = positions.astype(F32).reshape(b * s, 1)
    cos, sin = [t.reshape(b, s, LANES) for t in _rope_tables(pos_f, inv_full)]
    bias = _bias_tiles(wp["base"], depth)

    kc = jnp.zeros((b, s + KV_PAD, CA_WIDTH), BF16)
    vct = jnp.zeros((b, CA_HEADS, (s + KV_PAD) // BLK, CA_HEAD_DIM, BLK), BF16)
    for l in range(depth):
        qt, km, vt, qc, kc, vct = _inproj(l, x, cos, sin, wp, kc, vct)
        oa = _mla_attention(qt, km, vt)
        ob = _chunk_attention(l, qc, kc, vct, bias)
        x = _outproj_ffn(l, x, oa, ob, wp)
    return x
```

```python
import functools

import jax
import jax.numpy as jnp
from jax import lax
from jax.experimental import pallas as pl
from jax.experimental.pallas import tpu as pltpu

F32 = jnp.float32
BF16 = jnp.bfloat16

D_MODEL = 1024
CHUNK = 64
EPS = 1e-6
NEG_INF = -1e30
LOG2_E = 1.4426950408889634

MLA_HEADS = 8
MLA_NOPE = 64
MLA_ROPE = 32
MLA_QK = MLA_NOPE + MLA_ROPE
MLA_V = 64
Q_LORA = 256
KV_LORA = 128
ROPE_BASE = 10000.0
MLA_WIDTH = MLA_HEADS * MLA_V

CA_HEADS = 8
CA_HEAD_DIM = 64
CA_WIDTH = CA_HEADS * CA_HEAD_DIM
LEFT_CHUNKS = 8
REL_CLIP = 128

OFF_CQ = 0
OFF_CKV = OFF_CQ + Q_LORA
OFF_KR = OFF_CKV + KV_LORA
OFF_CA = OFF_KR + MLA_ROPE

D_FF = 2816
CONV_W = 3

LANES = 128
HEAD_SLOT = LANES
HALF_ROPE = MLA_ROPE // 2
BD_WIDTH = 256
V_DIM = MLA_V
assert V_DIM == CA_HEAD_DIM and MLA_HEADS == CA_HEADS and 2 * V_DIM == LANES
MLA_SLOTS = MLA_HEADS * HEAD_SLOT

P_CQ = 0
P_CKV = P_CQ + Q_LORA
P_KR = P_CKV + KV_LORA
P_KRS = P_KR + HEAD_SLOT
P_CAQ = P_KRS + HEAD_SLOT
P_CAK = P_CAQ + CA_WIDTH
P_CAV = P_CAK + CA_WIDTH
P_COLS = P_CAV + CA_WIDTH

SUM_ROWS = 16
BLK = 256
TM_IN = 1024
CA_PAD = LEFT_CHUNKS * CHUNK
KV_PAD = max(CA_PAD, TM_IN)
CA_TQ = BLK
CA_WIN = CA_PAD + CA_TQ
BASE_W = CA_WIN + CA_TQ
CA_VARIANTS = CA_PAD // CA_TQ + 1
CA_LOOKAHEAD = 3
CA_PV_DELAY = 3
CA_QBLOCKS = 4
MLA_TQ = BLK
MLA_HPS = 8
MLA_LOOKAHEAD = 4
MLA_PV_DELAY = 1
TM_FFN = 512
FF_CHUNK = 256
N_FF_CHUNKS = D_FF // FF_CHUNK
CARRY_ROWS = 8

V7X_VMEM_BYTES = 64 * 1024 * 1024
VMEM_LIMIT = V7X_VMEM_BYTES * 7 // 8


def _rms(x, g):
    return x * lax.rsqrt(jnp.mean(x * x, axis=-1, keepdims=True) + EPS) * g


def _dot(a, b):
    return jnp.dot(a, b, preferred_element_type=F32)


def _rope_kernel(pos_ref, inv_ref, c_ref, s_ref):
    ang = pos_ref[...] * inv_ref[...]
    s = jnp.sin(ang)
    lane = lax.broadcasted_iota(jnp.int32, ang.shape, 1)
    first_half = lane < MLA_NOPE + HALF_ROPE
    c_ref[...] = jnp.cos(ang)
    s_ref[...] = jnp.where(first_half, -s, s)


def _rope_tables(pos_f, inv_full):
    t = pos_f.shape[0]
    tm = 1024
    out = jax.ShapeDtypeStruct((t, LANES), F32)
    return pl.pallas_call(
        _rope_kernel,
        grid=(t // tm,),
        in_specs=[pl.BlockSpec((tm, 1), lambda i: (i, 0)),
                  pl.BlockSpec((1, LANES), lambda i: (0, 0))],
        out_specs=[pl.BlockSpec((tm, LANES), lambda i: (i, 0))] * 2,
        out_shape=[out] * 2,
        compiler_params=pltpu.CompilerParams(dimension_semantics=("arbitrary",)),
    )(pos_f, inv_full)


def _bias_kernel(base_ref, out_ref, tile_ref):
    @pl.when(pl.program_id(1) == 0)
    def _():
        x = pltpu.roll(jnp.broadcast_to(base_ref[...], (CA_TQ, BASE_W)), 0, 1,
                       stride=1, stride_axis=0)
        t = x[:, CA_TQ:] * LOG2_E
        i = lax.broadcasted_iota(jnp.int32, (CA_TQ, CA_WIN), 0) // CHUNK
        j = lax.broadcasted_iota(jnp.int32, (CA_TQ, CA_WIN), 1) // CHUNK
        band = (j >= i) & (j <= i + LEFT_CHUNKS)
        tile_ref[...] = jnp.where(band, t, NEG_INF).T

    key = lax.broadcasted_iota(jnp.int32, (CA_WIN, CA_TQ), 0)
    real = key >= CA_PAD - pl.program_id(1) * CA_TQ
    out_ref[...] = jnp.where(real, tile_ref[...], NEG_INF)


def _bias_tiles(base, depth):
    return pl.pallas_call(
        _bias_kernel,
        grid=(depth * CA_HEADS, CA_VARIANTS),
        in_specs=[pl.BlockSpec((None, 1, BASE_W), lambda i, v: (i, 0, 0))],
        out_specs=pl.BlockSpec((None, None, None, CA_WIN, CA_TQ),
                               lambda i, v: (i // CA_HEADS, v, i % CA_HEADS, 0, 0)),
        out_shape=jax.ShapeDtypeStruct((depth, CA_VARIANTS, CA_HEADS, CA_WIN, CA_TQ), F32),
        scratch_shapes=[pltpu.VMEM((CA_WIN, CA_TQ), F32)],
        compiler_params=pltpu.CompilerParams(dimension_semantics=("arbitrary", "arbitrary")),
    )(base)


def _inproj_kernel(x_ref, c_ref, s_ref, gmix_ref, w1_ref, gqa_ref, gkva_ref, wuq_ref, wuk_ref,
                   wuv_ref, gq_ref, gqs_ref, gk_ref, gks_ref, gcaq_ref, gcak_ref, bd_ref,
                   kpad_ref, vpad_ref,
                   qt_ref, km_ref, vt_ref, qct_ref, kc_ref, vct_ref):
    del kpad_ref, vpad_ref
    n_sub = TM_IN // BLK

    def project(r):
        rows = slice(r * BLK, (r + 1) * BLK)
        h = _rms(x_ref[rows, :], gmix_ref[...]).astype(BF16)
        return _dot(h, w1_ref[...])

    def head_sums(t):
        sq = (t * t).astype(BF16)
        return jnp.concatenate([_dot(sq[:, c:c + BD_WIDTH], bd_ref[...])
                                for c in range(0, CA_WIDTH, BD_WIDTH)], axis=1)

    def up_project(proj):
        cqn = _rms(proj[:, P_CQ:P_CKV], gqa_ref[...]).astype(BF16)
        ckvn = _rms(proj[:, P_CKV:P_KR], gkva_ref[...]).astype(BF16)
        caq = proj[:, P_CAQ:P_CAK]
        cak = proj[:, P_CAK:P_CAV]
        return {
            "q2": _dot(cqn, wuq_ref[...]),
            "kn": _dot(ckvn, wuk_ref[...]),
            "v": _dot(ckvn, wuv_ref[...]),
            "ssq": head_sums(caq),
            "ssk": head_sums(cak),
        }

    def transpose_heads(t, ref, r):
        for p in range(t.shape[1] // LANES):
            tt = t[:, p * LANES:(p + 1) * LANES].T.astype(BF16)
            ref[2 * p, r] = tt[:V_DIM]
            ref[2 * p + 1, r] = tt[V_DIM:]

    def finish(r, proj, up):
        rows = slice(r * BLK, (r + 1) * BLK)
        cos = c_ref[rows, :]
        sin = s_ref[rows, :]
        gq_c, gq_s = gq_ref[...] * cos, gqs_ref[...] * sin
        gk_c, gk_s = gk_ref[...] * cos, gks_ref[...] * sin
        kr = proj[:, P_KR:P_KRS]
        kr_s = proj[:, P_KRS:P_CAQ] * gk_s
        for hd in range(MLA_HEADS):
            sl = slice(hd * HEAD_SLOT, (hd + 1) * HEAD_SLOT)
            ssl = slice(MLA_SLOTS + hd * HEAD_SLOT, MLA_SLOTS + (hd + 1) * HEAD_SLOT)
            q = up["q2"][:, sl]
            rq = lax.rsqrt(jnp.sum(q * q, axis=-1, keepdims=True) * (1.0 / MLA_QK) + EPS)
            qr = (q * gq_c + up["q2"][:, ssl] * gq_s) * rq
            qt_ref[hd, :, rows] = qr.T.astype(BF16)
            k = up["kn"][:, sl] + kr
            rk = lax.rsqrt(jnp.sum(k * k, axis=-1, keepdims=True) * (1.0 / MLA_QK) + EPS)
            km_ref[rows, sl] = ((k * gk_c + kr_s) * rk).astype(BF16)
        transpose_heads(up["v"], vt_ref, r)
        caq = proj[:, P_CAQ:P_CAK]
        cak = proj[:, P_CAK:P_CAV]
        qcn = caq * lax.rsqrt(up["ssq"] * (1.0 / CA_HEAD_DIM) + EPS) * gcaq_ref[...]
        for p in range(CA_WIDTH // LANES):
            qct_ref[p, :, rows] = qcn[:, p * LANES:(p + 1) * LANES].T.astype(BF16)
        kc_ref[rows, :] = (cak * lax.rsqrt(up["ssk"] * (1.0 / CA_HEAD_DIM) + EPS)
                           * gcak_ref[...]).astype(BF16)
        transpose_heads(proj[:, P_CAV:P_COLS], vct_ref, r)

    proj = project(0)
    for r in range(n_sub):
        nxt = project(r + 1) if r + 1 < n_sub else None
        finish(r, proj, up_project(proj))
        proj = nxt


def _inproj(l, x, cos, sin, wp, kpad, vpad):
    b, s, _ = x.shape
    n_sub = TM_IN // BLK
    pad_tiles = KV_PAD // TM_IN

    def tok(width):
        return pl.BlockSpec((None, TM_IN, width), lambda bi, i: (bi, i, 0))

    def lw(shape):
        return pl.BlockSpec((None,) + shape, lambda bi, i: (l,) + (0,) * len(shape),
                            pipeline_mode=pl.Buffered(1))

    const = pl.BlockSpec((BD_WIDTH, BD_WIDTH), lambda bi, i: (0, 0),
                         pipeline_mode=pl.Buffered(1))
    any_spec = pl.BlockSpec(memory_space=pl.ANY)
    heads_t = lambda off: pl.BlockSpec((None, MLA_HEADS, n_sub, V_DIM, BLK),
                                       lambda bi, i: (bi, 0, i + off, 0, 0))
    tok_shape = lambda w: jax.ShapeDtypeStruct((b, s, w), BF16)
    return pl.pallas_call(
        _inproj_kernel,
        grid=(b, s // TM_IN),
        in_specs=[tok(D_MODEL), tok(LANES), tok(LANES),
                  lw((1, D_MODEL)), lw((D_MODEL, P_COLS)), lw((1, Q_LORA)), lw((1, KV_LORA)),
                  lw((Q_LORA, 2 * MLA_SLOTS)), lw((KV_LORA, MLA_SLOTS)),
                  lw((KV_LORA, MLA_WIDTH)), lw((1, HEAD_SLOT)), lw((1, HEAD_SLOT)),
                  lw((1, HEAD_SLOT)), lw((1, HEAD_SLOT)),
                  lw((1, CA_WIDTH)), lw((1, CA_WIDTH)), const, any_spec, any_spec],
        out_specs=[pl.BlockSpec((None, MLA_HEADS, HEAD_SLOT, TM_IN), lambda bi, i: (bi, 0, 0, i)),
                   tok(MLA_SLOTS), heads_t(0),
                   pl.BlockSpec((None, CA_WIDTH // LANES, LANES, TM_IN),
                                lambda bi, i: (bi, 0, 0, i)),
                   pl.BlockSpec((None, TM_IN, CA_WIDTH), lambda bi, i: (bi, i + pad_tiles, 0)),
                   heads_t(pad_tiles)],
        out_shape=[jax.ShapeDtypeStruct((b, MLA_HEADS, HEAD_SLOT, s), BF16), tok_shape(MLA_SLOTS),
                   jax.ShapeDtypeStruct((b, MLA_HEADS, s // BLK, MLA_V, BLK), BF16),
                   jax.ShapeDtypeStruct((b, CA_WIDTH // LANES, LANES, s), BF16),
                   jax.ShapeDtypeStruct(kpad.shape, kpad.dtype),
                   jax.ShapeDtypeStruct(vpad.shape, vpad.dtype)],
        input_output_aliases={17: 4, 18: 5},
        compiler_params=pltpu.CompilerParams(
            dimension_semantics=("arbitrary", "arbitrary"), vmem_limit_bytes=VMEM_LIMIT),
    )(x, cos, sin, wp["g_mix"], wp["w1"], wp["g_qa"], wp["g_kva"], wp["w_uq"], wp["w_uk"],
      wp["w_uv"], wp["g_q"], wp["g_qs"], wp["g_k"], wp["g_ks"], wp["g_caq"], wp["g_cak"],
      wp["bd"], kpad, vpad)


def _mla_kernel(qt_ref, k_ref, vt_ref, o_ref, m_ref, acc_ref, s_ref):
    i = pl.program_id(2)
    t = MLA_TQ
    ones = jnp.ones((SUM_ROWS, t), BF16)
    m_ref[...] = jnp.full(m_ref.shape, NEG_INF, F32)
    acc_ref[...] = jnp.zeros(acc_ref.shape, F32)

    def scores(j, hh):
        rows = pl.ds(pl.multiple_of(j * t, t), t)
        sl = slice(hh * HEAD_SLOT, (hh + 1) * HEAD_SLOT)
        return _dot(k_ref[rows, sl], qt_ref[hh])

    def step(j, last):
        def accumulate(hh, p, alpha):
            v1 = jnp.concatenate([vt_ref[hh, j], ones], axis=0)
            acc_ref[hh] = acc_ref[hh] * alpha + _dot(v1, p)

        pending = {hh: s_ref[hh] for hh in range(MLA_LOOKAHEAD)}
        done = {}
        for hh in range(MLA_HPS):
            st = pending.pop(hh)
            ahead = hh + MLA_LOOKAHEAD
            if ahead < MLA_HPS:
                pending[ahead] = scores(j, ahead)
            elif not last:
                s_ref[ahead - MLA_HPS] = scores(j + 1, ahead - MLA_HPS)
            if last:
                kc = lax.broadcasted_iota(jnp.int32, st.shape, 0) // CHUNK
                qc = lax.broadcasted_iota(jnp.int32, st.shape, 1) // CHUNK
                st = jnp.where(kc <= qc, st, NEG_INF)
            m_prev = m_ref[hh]
            m_new = jnp.maximum(m_prev, jnp.max(st, axis=0, keepdims=True))
            m_ref[hh] = m_new
            done[hh] = (jnp.exp2(st - m_new).astype(BF16), jnp.exp2(m_prev - m_new))
            if hh >= MLA_PV_DELAY:
                accumulate(hh - MLA_PV_DELAY, *done.pop(hh - MLA_PV_DELAY))
        for hh in sorted(done):
            accumulate(hh, *done[hh])

    for hh in range(MLA_LOOKAHEAD):
        s_ref[hh] = scores(0, hh)

    def body(k, carry):
        step(2 * k, False)
        step(2 * k + 1, False)
        return carry

    lax.fori_loop(0, i // 2, body, 0)

    @pl.when(i % 2 == 1)
    def _():
        step(i - 1, False)

    step(i, True)
    for pr in range(MLA_HPS // 2):
        ot = jnp.concatenate([acc_ref[hh, :MLA_V, :] / acc_ref[hh, MLA_V:MLA_V + 1, :]
                              for hh in (2 * pr, 2 * pr + 1)], axis=0)
        o_ref[:, pr * LANES:(pr + 1) * LANES] = ot.T.astype(BF16)


def _mla_attention(qt, km, vt):
    b, s, _ = km.shape
    groups = MLA_HEADS // MLA_HPS
    nk = s // MLA_TQ
    return pl.pallas_call(
        _mla_kernel,
        grid=(b, groups, s // MLA_TQ),
        in_specs=[pl.BlockSpec((None, MLA_HPS, HEAD_SLOT, MLA_TQ), lambda bi, g, i: (bi, g, 0, i)),
                  pl.BlockSpec((None, s, MLA_HPS * HEAD_SLOT), lambda bi, g, i: (bi, 0, g)),
                  pl.BlockSpec((None, MLA_HPS, nk, MLA_V, MLA_TQ),
                               lambda bi, g, i: (bi, g, 0, 0, 0))],
        out_specs=pl.BlockSpec((None, MLA_TQ, MLA_HPS * MLA_V), lambda bi, g, i: (bi, i, g)),
        out_shape=jax.ShapeDtypeStruct((b, s, MLA_WIDTH), BF16),
        scratch_shapes=[pltpu.VMEM((MLA_HPS, 1, MLA_TQ), F32),
                        pltpu.VMEM((MLA_HPS, MLA_V + SUM_ROWS, MLA_TQ), F32),
                        pltpu.VMEM((MLA_LOOKAHEAD, MLA_TQ, MLA_TQ), F32)],
        compiler_params=pltpu.CompilerParams(
            dimension_semantics=("arbitrary", "arbitrary", "arbitrary"),
            vmem_limit_bytes=VMEM_LIMIT),
    )(qt, km, vt)


def _ca_kernel(qt_ref, k_ref, vt_ref, bias_ref, o_ref):
    lead = KV_PAD - CA_PAD
    first = lax.broadcasted_iota(jnp.int32, (LANES, 1), 0) < CA_HEAD_DIM
    ones = jnp.ones((SUM_ROWS, BLK), BF16)
    items = [(sub, h) for sub in range(CA_QBLOCKS) for h in range(CA_HEADS)]
    pieces = range(CA_WIN // BLK)

    def block(sub):
        return pl.program_id(1) * CA_QBLOCKS + sub

    def scores(sub, h):
        sl = slice((h // 2) * LANES, (h // 2 + 1) * LANES)
        qt = qt_ref[h // 2, :, sub * CA_TQ:(sub + 1) * CA_TQ]
        qh = jnp.where(first if h % 2 == 0 else ~first, qt, jnp.zeros_like(qt))
        row0 = lead + block(sub) * CA_TQ
        return [_dot(k_ref[pl.ds(pl.multiple_of(row0 + kb * BLK, BLK), BLK), sl], qh)
                for kb in pieces]

    def attend(sub, h, probs):
        blk0 = lead // BLK + block(sub)
        acc = None
        for kb in pieces:
            v1 = jnp.concatenate([vt_ref[h, blk0 + kb], ones], axis=0)
            part = _dot(v1, probs[kb])
            acc = part if acc is None else acc + part
        return acc[:CA_HEAD_DIM] / acc[CA_HEAD_DIM:CA_HEAD_DIM + 1]

    pending = {it: scores(*it) for it in items[:CA_LOOKAHEAD]}
    probs = {}
    outs = {}

    def flush(sub, h):
        outs[h] = attend(sub, h, probs.pop((sub, h)))
        if h % 2 == 1:
            ot = jnp.concatenate([outs.pop(h - 1), outs.pop(h)], axis=0)
            o_ref[sub * CA_TQ:(sub + 1) * CA_TQ, (h // 2) * LANES:(h // 2 + 1) * LANES] = (
                ot.T.astype(BF16))

    for n, (sub, h) in enumerate(items):
        st = pending.pop((sub, h))
        if n + CA_LOOKAHEAD < len(items):
            ahead = items[n + CA_LOOKAHEAD]
            pending[ahead] = scores(*ahead)
        variant = jnp.minimum(block(sub), CA_VARIANTS - 1)
        st = [st[kb] + bias_ref[variant, h, kb * BLK:(kb + 1) * BLK, :] for kb in pieces]
        m = functools.reduce(jnp.maximum, [jnp.max(t, axis=0, keepdims=True) for t in st])
        probs[(sub, h)] = [jnp.exp2(t - m).astype(BF16) for t in st]
        if n >= CA_PV_DELAY:
            flush(*items[n - CA_PV_DELAY])
    for it in items[len(items) - CA_PV_DELAY:]:
        flush(*it)


def _chunk_attention(l, qct, kc, vct, bias):
    b, _, _, s = qct.shape
    nblk = vct.shape[2]
    tq = CA_QBLOCKS * CA_TQ
    return pl.pallas_call(
        _ca_kernel,
        grid=(b, s // tq),
        in_specs=[pl.BlockSpec((None, CA_WIDTH // LANES, LANES, tq), lambda bi, i: (bi, 0, 0, i)),
                  pl.BlockSpec((None, s + KV_PAD, CA_WIDTH), lambda bi, i: (bi, 0, 0)),
                  pl.BlockSpec((None, CA_HEADS, nblk, CA_HEAD_DIM, BLK),
                               lambda bi, i: (bi, 0, 0, 0, 0)),
                  pl.BlockSpec((None, CA_VARIANTS, CA_HEADS, CA_WIN, CA_TQ),
                               lambda bi, i: (l, 0, 0, 0, 0), pipeline_mode=pl.Buffered(1))],
        out_specs=pl.BlockSpec((None, tq, CA_WIDTH), lambda bi, i: (bi, i, 0)),
        out_shape=jax.ShapeDtypeStruct((b, s, CA_WIDTH), BF16),
        compiler_params=pltpu.CompilerParams(
            dimension_semantics=("arbitrary", "arbitrary"), vmem_limit_bytes=VMEM_LIMIT),
    )(qct, kc, vct, bias)


def _ffn_kernel(x_ref, oa_ref, ob_ref, goa_ref, gob_ref, wout_ref, gffn_ref, wup_ref, cw_ref,
                cb_ref, wdown_ref, out_ref, carry_ref, u_ref, h_ref):
    tm = TM_FFN

    @pl.when(pl.program_id(1) == 0)
    def _():
        carry_ref[...] = jnp.zeros(carry_ref.shape, F32)

    na = _rms(oa_ref[...].astype(F32), goa_ref[...]).astype(BF16)
    nb = _rms(ob_ref[...].astype(F32), gob_ref[...]).astype(BF16)
    x1 = (x_ref[...] + _dot(na, wout_ref[0:MLA_WIDTH, :])
          + _dot(nb, wout_ref[MLA_WIDTH:MLA_WIDTH + CA_WIDTH, :]))
    out_ref[...] = x1
    h_ref[...] = _rms(x1, gffn_ref[...]).astype(BF16)

    def up(c, slot):
        u = _dot(h_ref[...], wup_ref[c])
        u_ref[slot, 0:CARRY_ROWS, :] = carry_ref[c]
        u_ref[slot, CARRY_ROWS:CARRY_ROWS + tm, :] = u
        carry_ref[c] = u[tm - CARRY_ROWS:, :]

    def act_down(c, slot):
        cw = cw_ref[c]
        y = cb_ref[c]
        for tap in range(CONV_W):
            off = CARRY_ROWS - (CONV_W - 1) + tap
            y = y + u_ref[slot, off:off + tm, :] * cw[tap:tap + 1, :]
        g = y[:, :FF_CHUNK]
        a = g * (1.0 / (1.0 + jnp.exp(-g))) * y[:, FF_CHUNK:]
        out_ref[...] += _dot(a.astype(BF16), wdown_ref[c])

    up(0, 0)

    def body(k, carry):
        c = 2 * k
        up(c + 1, 1)
        act_down(c, 0)
        up(c + 2, 0)
        act_down(c + 1, 1)
        return carry

    lax.fori_loop(0, (N_FF_CHUNKS - 1) // 2, body, 0)
    act_down(N_FF_CHUNKS - 1, 0)


def _outproj_ffn(l, x, oa, ob, wp):
    b, s, _ = x.shape

    def tok(width):
        return pl.BlockSpec((None, TM_FFN, width), lambda bi, i: (bi, i, 0))

    def lw(shape):
        return pl.BlockSpec((None,) + shape, lambda bi, i: (l,) + (0,) * len(shape),
                            pipeline_mode=pl.Buffered(1))

    return pl.pallas_call(
        _ffn_kernel,
        grid=(b, s // TM_FFN),
        in_specs=[tok(D_MODEL), tok(MLA_WIDTH), tok(CA_WIDTH),
                  lw((1, MLA_WIDTH)), lw((1, CA_WIDTH)), lw((D_MODEL, D_MODEL)),
                  lw((1, D_MODEL)), lw((N_FF_CHUNKS, D_MODEL, 2 * FF_CHUNK)),
                  lw((N_FF_CHUNKS, CARRY_ROWS, 2 * FF_CHUNK)),
                  lw((N_FF_CHUNKS, 1, 2 * FF_CHUNK)),
                  lw((N_FF_CHUNKS, FF_CHUNK, D_MODEL))],
        out_specs=tok(D_MODEL),
        out_shape=jax.ShapeDtypeStruct((b, s, D_MODEL), F32),
        scratch_shapes=[pltpu.VMEM((N_FF_CHUNKS, CARRY_ROWS, 2 * FF_CHUNK), F32),
                        pltpu.VMEM((2, CARRY_ROWS + TM_FFN, 2 * FF_CHUNK), F32),
                        pltpu.VMEM((TM_FFN, D_MODEL), BF16)],
        compiler_params=pltpu.CompilerParams(
            dimension_semantics=("arbitrary", "arbitrary"), vmem_limit_bytes=VMEM_LIMIT),
    )(x, oa, ob, wp["g_oa"], wp["g_ob"], wp["w_out"], wp["g_ffn"], wp["w_up"], wp["conv_w"],
      wp["conv_b"], wp["w_down"])


def _rope_slot(first, second, axis_pad):
    t = jnp.concatenate([first, second], axis=-1)
    return jnp.pad(t, axis_pad + ((MLA_NOPE, HEAD_SLOT - MLA_QK),))


def _prepare(g_mix, w_in, w_uq, w_ukv, g_q_lora, g_kv_lora, g_mla_q, g_mla_k, g_ca_q, g_ca_k,
             rel_bias, g_out_mla, g_out_ca, w_out, g_ffn, w_up, conv_w, conv_b, w_down):
    depth = w_in.shape[0]
    row = lambda g: g[:, None, :]
    zpad = HEAD_SLOT - MLA_QK
    r1 = slice(MLA_NOPE, MLA_NOPE + HALF_ROPE)
    r2 = slice(MLA_NOPE + HALF_ROPE, MLA_QK)

    w_kr = w_in[:, :, OFF_KR:OFF_CA]
    kr_slot = _rope_slot(w_kr[..., :HALF_ROPE], w_kr[..., HALF_ROPE:], ((0, 0), (0, 0)))
    kr_swap = _rope_slot(w_kr[..., HALF_ROPE:], w_kr[..., :HALF_ROPE], ((0, 0), (0, 0)))
    w1 = jnp.concatenate([w_in[:, :, OFF_CQ:OFF_KR], kr_slot, kr_swap, w_in[:, :, OFF_CA:]],
                         axis=-1)

    wq = w_uq.reshape(depth, Q_LORA, MLA_HEADS, MLA_QK)
    wq_slot = jnp.pad(wq, ((0, 0), (0, 0), (0, 0), (0, zpad))).reshape(depth, Q_LORA, -1)
    wq_swap = _rope_slot(wq[..., r2], wq[..., r1], ((0, 0), (0, 0), (0, 0)))
    wq2 = jnp.concatenate([wq_slot, wq_swap.reshape(depth, Q_LORA, -1)], axis=-1)
    wkv = w_ukv.reshape(depth, KV_LORA, MLA_HEADS, MLA_NOPE + MLA_V)
    wk = jnp.pad(wkv[..., :MLA_NOPE], ((0, 0), (0, 0), (0, 0), (0, HEAD_SLOT - MLA_NOPE)))
    wk = wk.reshape(depth, KV_LORA, -1)
    wv = wkv[..., MLA_NOPE:].reshape(depth, KV_LORA, MLA_WIDTH)

    gq = g_mla_q * (MLA_QK ** -0.5 * LOG2_E)
    g_q = jnp.pad(gq, ((0, 0), (0, zpad)))
    g_qs = _rope_slot(gq[:, r2], gq[:, r1], ((0, 0),))
    g_k = jnp.pad(g_mla_k, ((0, 0), (0, zpad)))
    g_ks = _rope_slot(g_mla_k[:, r2], g_mla_k[:, r1], ((0, 0),))
    g_caq = jnp.tile(g_ca_q * (CA_HEAD_DIM ** -0.5 * LOG2_E), (1, CA_HEADS))
    g_cak = jnp.tile(g_ca_k, (1, CA_HEADS))
    head = jnp.arange(BD_WIDTH) // CA_HEAD_DIM
    bd = (head[:, None] == head[None, :]).astype(BF16)

    ext = jnp.pad(rel_bias, ((0, 0), (0, 0), (CA_TQ - 1 - REL_CLIP, CA_WIN - REL_CLIP)),
                  mode="edge")
    base = ext[:, :, ::-1].reshape(depth * CA_HEADS, 1, BASE_W)

    def chunks(w):
        g = w[..., :D_FF].reshape(w.shape[:-1] + (N_FF_CHUNKS, FF_CHUNK))
        v = w[..., D_FF:].reshape(w.shape[:-1] + (N_FF_CHUNKS, FF_CHUNK))
        return jnp.concatenate([g, v], axis=-1)

    w_up_c = jnp.moveaxis(chunks(w_up), 2, 1)
    conv_c = jnp.moveaxis(chunks(conv_w), 2, 1)
    conv_c = jnp.pad(conv_c, ((0, 0), (0, 0), (0, CARRY_ROWS - CONV_W), (0, 0)))
    cb_c = chunks(conv_b)[:, :, None, :]
    return {
        "g_mix": row(g_mix), "w1": w1.astype(BF16), "g_qa": row(g_q_lora),
        "g_kva": row(g_kv_lora), "w_uq": wq2.astype(BF16), "w_uk": wk.astype(BF16),
        "w_uv": wv.astype(BF16), "g_q": row(g_q), "g_qs": row(g_qs), "g_k": row(g_k),
        "g_ks": row(g_ks), "g_caq": row(g_caq), "g_cak": row(g_cak), "bd": bd, "base": base,
        "g_oa": row(g_out_mla), "g_ob": row(g_out_ca), "w_out": w_out.astype(BF16),
        "g_ffn": row(g_ffn), "w_up": w_up_c.astype(BF16), "conv_w": conv_c, "conv_b": cb_c,
        "w_down": w_down.reshape(depth, N_FF_CHUNKS, FF_CHUNK, D_MODEL).astype(BF16),
    }


def kernel(x, positions, g_mix, w_in, w_uq, w_ukv, g_q_lora, g_kv_lora, g_mla_q, g_mla_k,
           g_ca_q, g_ca_k, rel_bias, g_out_mla, g_out_ca, w_out, g_ffn, w_up, conv_w, conv_b,
           w_down):
    b, s, _ = x.shape
    depth = w_in.shape[0]
    assert s % TM_IN == 0 and s % TM_FFN == 0 and KV_PAD % TM_IN == 0
    assert (KV_PAD - CA_PAD) % BLK == 0
    wp = _prepare(g_mix, w_in, w_uq, w_ukv, g_q_lora, g_kv_lora, g_mla_q, g_mla_k, g_ca_q,
                  g_ca_k, rel_bias, g_out_mla, g_out_ca, w_out, g_ffn, w_up, conv_w, conv_b,
                  w_down)

    inv = ROPE_BASE ** (-jnp.arange(0, MLA_ROPE, 2, dtype=F32) / MLA_ROPE)
    inv_full = jnp.zeros((1, LANES), F32)
    inv_full = inv_full.at[0, MLA_NOPE:MLA_NOPE + HALF_ROPE].set(inv)
    inv_full = inv_full.at[0, MLA_NOPE + HALF_ROPE:MLA_QK].set(inv)
    pos_f = positions.astype(F32).reshape(b * s, 1)
    cos, sin = [t.reshape(b, s, LANES) for t in _rope_tables(pos_f, inv_full)]
    bias = _bias_tiles(wp["base"], depth)

    kc = jnp.zeros((b, s + KV_PAD, CA_WIDTH), BF16)
    vct = jnp.zeros((b, CA_HEADS, (s + KV_PAD) // BLK, CA_HEAD_DIM, BLK), BF16)
    for l in range(depth):
        qt, km, vt, qct, kc, vct = _inproj(l, x, cos, sin, wp, kc, vct)
        oa = _mla_attention(qt, km, vt)
        ob = _chunk_attention(l, qct, kc, vct, bias)
        x = _outproj_ffn(l, x, oa, ob, wp)
    return x
```

```python
import functools

import jax
import jax.numpy as jnp
from jax import lax
from jax.experimental import pallas as pl
from jax.experimental.pallas import tpu as pltpu

F32 = jnp.float32
BF16 = jnp.bfloat16

D_MODEL = 1024
CHUNK = 64
EPS = 1e-6
NEG_INF = -1e30
LOG2_E = 1.4426950408889634

MLA_HEADS = 8
MLA_NOPE = 64
MLA_ROPE = 32
MLA_QK = MLA_NOPE + MLA_ROPE
MLA_V = 64
Q_LORA = 256
KV_LORA = 128
ROPE_BASE = 10000.0
MLA_WIDTH = MLA_HEADS * MLA_V

CA_HEADS = 8
CA_HEAD_DIM = 64
CA_WIDTH = CA_HEADS * CA_HEAD_DIM
LEFT_CHUNKS = 8
REL_CLIP = 128

OFF_CQ = 0
OFF_CKV = OFF_CQ + Q_LORA
OFF_KR = OFF_CKV + KV_LORA
OFF_CA = OFF_KR + MLA_ROPE

D_FF = 2816
CONV_W = 3

LANES = 128
HEAD_SLOT = LANES
HALF_ROPE = MLA_ROPE // 2
BD_WIDTH = 256
V_DIM = MLA_V
assert V_DIM == CA_HEAD_DIM and MLA_HEADS == CA_HEADS and 2 * V_DIM == LANES
MLA_SLOTS = MLA_HEADS * HEAD_SLOT

P_CQ = 0
P_CKV = P_CQ + Q_LORA
P_KR = P_CKV + KV_LORA
P_KRS = P_KR + HEAD_SLOT
P_CAQ = P_KRS + HEAD_SLOT
P_CAK = P_CAQ + CA_WIDTH
P_CAV = P_CAK + CA_WIDTH
P_COLS = P_CAV + CA_WIDTH

SUM_ROWS = 16
BLK = 256
TM_IN = 1024
CA_PAD = LEFT_CHUNKS * CHUNK
KV_PAD = max(CA_PAD, TM_IN)
CA_TQ = BLK
CA_WIN = CA_PAD + CA_TQ
BASE_W = CA_WIN + CA_TQ
CA_VARIANTS = CA_PAD // CA_TQ + 1
CA_LOOKAHEAD = 3
CA_PV_DELAY = 3
CA_QBLOCKS = 4
MLA_TQ = BLK
MLA_HPS = 8
MLA_LOOKAHEAD = 4
MLA_PV_DELAY = 1
TM_FFN = 512
FF_CHUNK = 256
N_FF_CHUNKS = D_FF // FF_CHUNK
CARRY_ROWS = 8
FFN_SLOTS = 4

V7X_VMEM_BYTES = 64 * 1024 * 1024
VMEM_LIMIT = V7X_VMEM_BYTES * 7 // 8


def _rms(x, g):
    return x * lax.rsqrt(jnp.mean(x * x, axis=-1, keepdims=True) + EPS) * g


def _dot(a, b):
    return jnp.dot(a, b, preferred_element_type=F32)


def _rope_kernel(pos_ref, inv_ref, c_ref, s_ref):
    ang = pos_ref[...] * inv_ref[...]
    s = jnp.sin(ang)
    lane = lax.broadcasted_iota(jnp.int32, ang.shape, 1)
    first_half = lane < MLA_NOPE + HALF_ROPE
    c_ref[...] = jnp.cos(ang)
    s_ref[...] = jnp.where(first_half, -s, s)


def _rope_tables(pos_f, inv_full):
    t = pos_f.shape[0]
    tm = 1024
    out = jax.ShapeDtypeStruct((t, LANES), F32)
    return pl.pallas_call(
        _rope_kernel,
        grid=(t // tm,),
        in_specs=[pl.BlockSpec((tm, 1), lambda i: (i, 0)),
                  pl.BlockSpec((1, LANES), lambda i: (0, 0))],
        out_specs=[pl.BlockSpec((tm, LANES), lambda i: (i, 0))] * 2,
        out_shape=[out] * 2,
        compiler_params=pltpu.CompilerParams(dimension_semantics=("arbitrary",)),
    )(pos_f, inv_full)


def _bias_kernel(base_ref, out_ref, tile_ref):
    @pl.when(pl.program_id(1) == 0)
    def _():
        x = pltpu.roll(jnp.broadcast_to(base_ref[...], (CA_TQ, BASE_W)), 0, 1,
                       stride=1, stride_axis=0)
        t = x[:, CA_TQ:] * LOG2_E
        i = lax.broadcasted_iota(jnp.int32, (CA_TQ, CA_WIN), 0) // CHUNK
        j = lax.broadcasted_iota(jnp.int32, (CA_TQ, CA_WIN), 1) // CHUNK
        band = (j >= i) & (j <= i + LEFT_CHUNKS)
        tile_ref[...] = jnp.where(band, t, NEG_INF).T

    key = lax.broadcasted_iota(jnp.int32, (CA_WIN, CA_TQ), 0)
    real = key >= CA_PAD - pl.program_id(1) * CA_TQ
    out_ref[...] = jnp.where(real, tile_ref[...], NEG_INF)


def _bias_tiles(base, depth):
    return pl.pallas_call(
        _bias_kernel,
        grid=(depth * CA_HEADS, CA_VARIANTS),
        in_specs=[pl.BlockSpec((None, 1, BASE_W), lambda i, v: (i, 0, 0))],
        out_specs=pl.BlockSpec((None, None, None, CA_WIN, CA_TQ),
                               lambda i, v: (i // CA_HEADS, v, i % CA_HEADS, 0, 0)),
        out_shape=jax.ShapeDtypeStruct((depth, CA_VARIANTS, CA_HEADS, CA_WIN, CA_TQ), F32),
        scratch_shapes=[pltpu.VMEM((CA_WIN, CA_TQ), F32)],
        compiler_params=pltpu.CompilerParams(dimension_semantics=("arbitrary", "arbitrary")),
    )(base)


def _inproj_kernel(x_ref, c_ref, s_ref, gmix_ref, w1_ref, gqa_ref, gkva_ref, wuq_ref, wuk_ref,
                   wuv_ref, gq_ref, gqs_ref, gk_ref, gks_ref, gcaq_ref, gcak_ref, bd_ref,
                   kpad_ref, vpad_ref,
                   qt_ref, km_ref, vt_ref, qct_ref, kc_ref, vct_ref):
    del kpad_ref, vpad_ref
    n_sub = TM_IN // BLK

    def project(r):
        rows = slice(r * BLK, (r + 1) * BLK)
        h = _rms(x_ref[rows, :], gmix_ref[...]).astype(BF16)
        return _dot(h, w1_ref[...])

    def head_sums(t):
        sq = (t * t).astype(BF16)
        return jnp.concatenate([_dot(sq[:, c:c + BD_WIDTH], bd_ref[...])
                                for c in range(0, CA_WIDTH, BD_WIDTH)], axis=1)

    def up_project(proj):
        cqn = _rms(proj[:, P_CQ:P_CKV], gqa_ref[...]).astype(BF16)
        ckvn = _rms(proj[:, P_CKV:P_KR], gkva_ref[...]).astype(BF16)
        caq = proj[:, P_CAQ:P_CAK]
        cak = proj[:, P_CAK:P_CAV]
        return {
            "q2": _dot(cqn, wuq_ref[...]),
            "kn": _dot(ckvn, wuk_ref[...]),
            "v": _dot(ckvn, wuv_ref[...]),
            "ssq": head_sums(caq),
            "ssk": head_sums(cak),
        }

    def transpose_heads(t, ref, r):
        for p in range(t.shape[1] // LANES):
            tt = t[:, p * LANES:(p + 1) * LANES].T.astype(BF16)
            ref[2 * p, r] = tt[:V_DIM]
            ref[2 * p + 1, r] = tt[V_DIM:]

    def finish(r, proj, up):
        rows = slice(r * BLK, (r + 1) * BLK)
        cos = c_ref[rows, :]
        sin = s_ref[rows, :]
        gq_c, gq_s = gq_ref[...] * cos, gqs_ref[...] * sin
        gk_c, gk_s = gk_ref[...] * cos, gks_ref[...] * sin
        kr = proj[:, P_KR:P_KRS]
        kr_s = proj[:, P_KRS:P_CAQ] * gk_s
        for hd in range(MLA_HEADS):
            sl = slice(hd * HEAD_SLOT, (hd + 1) * HEAD_SLOT)
            ssl = slice(MLA_SLOTS + hd * HEAD_SLOT, MLA_SLOTS + (hd + 1) * HEAD_SLOT)
            q = up["q2"][:, sl]
            rq = lax.rsqrt(jnp.sum(q * q, axis=-1, keepdims=True) * (1.0 / MLA_QK) + EPS)
            qr = (q * gq_c + up["q2"][:, ssl] * gq_s) * rq
            qt_ref[hd, :, rows] = qr.T.astype(BF16)
            k = up["kn"][:, sl] + kr
            rk = lax.rsqrt(jnp.sum(k * k, axis=-1, keepdims=True) * (1.0 / MLA_QK) + EPS)
            km_ref[rows, sl] = ((k * gk_c + kr_s) * rk).astype(BF16)
        transpose_heads(up["v"], vt_ref, r)
        caq = proj[:, P_CAQ:P_CAK]
        cak = proj[:, P_CAK:P_CAV]
        qcn = caq * lax.rsqrt(up["ssq"] * (1.0 / CA_HEAD_DIM) + EPS) * gcaq_ref[...]
        for p in range(CA_WIDTH // LANES):
            qct_ref[p, :, rows] = qcn[:, p * LANES:(p + 1) * LANES].T.astype(BF16)
        kc_ref[rows, :] = (cak * lax.rsqrt(up["ssk"] * (1.0 / CA_HEAD_DIM) + EPS)
                           * gcak_ref[...]).astype(BF16)
        transpose_heads(proj[:, P_CAV:P_COLS], vct_ref, r)

    proj = project(0)
    for r in range(n_sub):
        nxt = project(r + 1) if r + 1 < n_sub else None
        finish(r, proj, up_project(proj))
        proj = nxt


def _inproj(l, x, cos, sin, wp, kpad, vpad):
    b, s, _ = x.shape
    n_sub = TM_IN // BLK
    pad_tiles = KV_PAD // TM_IN

    def tok(width):
        return pl.BlockSpec((None, TM_IN, width), lambda bi, i: (bi, i, 0))

    def lw(shape):
        return pl.BlockSpec((None,) + shape, lambda bi, i: (l,) + (0,) * len(shape),
                            pipeline_mode=pl.Buffered(1))

    const = pl.BlockSpec((BD_WIDTH, BD_WIDTH), lambda bi, i: (0, 0),
                         pipeline_mode=pl.Buffered(1))
    any_spec = pl.BlockSpec(memory_space=pl.ANY)
    heads_t = lambda off: pl.BlockSpec((None, MLA_HEADS, n_sub, V_DIM, BLK),
                                       lambda bi, i: (bi, 0, i + off, 0, 0))
    tok_shape = lambda w: jax.ShapeDtypeStruct((b, s, w), BF16)
    return pl.pallas_call(
        _inproj_kernel,
        grid=(b, s // TM_IN),
        in_specs=[tok(D_MODEL), tok(LANES), tok(LANES),
                  lw((1, D_MODEL)), lw((D_MODEL, P_COLS)), lw((1, Q_LORA)), lw((1, KV_LORA)),
                  lw((Q_LORA, 2 * MLA_SLOTS)), lw((KV_LORA, MLA_SLOTS)),
                  lw((KV_LORA, MLA_WIDTH)), lw((1, HEAD_SLOT)), lw((1, HEAD_SLOT)),
                  lw((1, HEAD_SLOT)), lw((1, HEAD_SLOT)),
                  lw((1, CA_WIDTH)), lw((1, CA_WIDTH)), const, any_spec, any_spec],
        out_specs=[pl.BlockSpec((None, MLA_HEADS, HEAD_SLOT, TM_IN), lambda bi, i: (bi, 0, 0, i)),
                   tok(MLA_SLOTS), heads_t(0),
                   pl.BlockSpec((None, CA_WIDTH // LANES, LANES, TM_IN),
                                lambda bi, i: (bi, 0, 0, i)),
                   pl.BlockSpec((None, TM_IN, CA_WIDTH), lambda bi, i: (bi, i + pad_tiles, 0)),
                   heads_t(pad_tiles)],
        out_shape=[jax.ShapeDtypeStruct((b, MLA_HEADS, HEAD_SLOT, s), BF16), tok_shape(MLA_SLOTS),
                   jax.ShapeDtypeStruct((b, MLA_HEADS, s // BLK, MLA_V, BLK), BF16),
                   jax.ShapeDtypeStruct((b, CA_WIDTH // LANES, LANES, s), BF16),
                   jax.ShapeDtypeStruct(kpad.shape, kpad.dtype),
                   jax.ShapeDtypeStruct(vpad.shape, vpad.dtype)],
        input_output_aliases={17: 4, 18: 5},
        compiler_params=pltpu.CompilerParams(
            dimension_semantics=("arbitrary", "arbitrary"), vmem_limit_bytes=VMEM_LIMIT),
    )(x, cos, sin, wp["g_mix"], wp["w1"], wp["g_qa"], wp["g_kva"], wp["w_uq"], wp["w_uk"],
      wp["w_uv"], wp["g_q"], wp["g_qs"], wp["g_k"], wp["g_ks"], wp["g_caq"], wp["g_cak"],
      wp["bd"], kpad, vpad)


def _mla_kernel(qt_ref, k_ref, vt_ref, o_ref, m_ref, acc_ref, s_ref):
    i = pl.program_id(2)
    t = MLA_TQ
    ones = jnp.ones((SUM_ROWS, t), BF16)
    m_ref[...] = jnp.full(m_ref.shape, NEG_INF, F32)
    acc_ref[...] = jnp.zeros(acc_ref.shape, F32)

    def scores(j, hh):
        rows = pl.ds(pl.multiple_of(j * t, t), t)
        sl = slice(hh * HEAD_SLOT, (hh + 1) * HEAD_SLOT)
        return _dot(k_ref[rows, sl], qt_ref[hh])

    def step(j, last):
        def accumulate(hh, p, alpha):
            v1 = jnp.concatenate([vt_ref[hh, j], ones], axis=0)
            acc_ref[hh] = acc_ref[hh] * alpha + _dot(v1, p)

        pending = {hh: s_ref[hh] for hh in range(MLA_LOOKAHEAD)}
        done = {}
        for hh in range(MLA_HPS):
            st = pending.pop(hh)
            ahead = hh + MLA_LOOKAHEAD
            if ahead < MLA_HPS:
                pending[ahead] = scores(j, ahead)
            elif not last:
                s_ref[ahead - MLA_HPS] = scores(j + 1, ahead - MLA_HPS)
            if last:
                kc = lax.broadcasted_iota(jnp.int32, st.shape, 0) // CHUNK
                qc = lax.broadcasted_iota(jnp.int32, st.shape, 1) // CHUNK
                st = jnp.where(kc <= qc, st, NEG_INF)
            m_prev = m_ref[hh]
            m_new = jnp.maximum(m_prev, jnp.max(st, axis=0, keepdims=True))
            m_ref[hh] = m_new
            done[hh] = (jnp.exp2(st - m_new).astype(BF16), jnp.exp2(m_prev - m_new))
            if hh >= MLA_PV_DELAY:
                accumulate(hh - MLA_PV_DELAY, *done.pop(hh - MLA_PV_DELAY))
        for hh in sorted(done):
            accumulate(hh, *done[hh])

    for hh in range(MLA_LOOKAHEAD):
        s_ref[hh] = scores(0, hh)

    def body(k, carry):
        step(2 * k, False)
        step(2 * k + 1, False)
        return carry

    lax.fori_loop(0, i // 2, body, 0)

    @pl.when(i % 2 == 1)
    def _():
        step(i - 1, False)

    step(i, True)
    for pr in range(MLA_HPS // 2):
        ot = jnp.concatenate([acc_ref[hh, :MLA_V, :] / acc_ref[hh, MLA_V:MLA_V + 1, :]
                              for hh in (2 * pr, 2 * pr + 1)], axis=0)
        o_ref[:, pr * LANES:(pr + 1) * LANES] = ot.T.astype(BF16)


def _mla_attention(qt, km, vt):
    b, s, _ = km.shape
    groups = MLA_HEADS // MLA_HPS
    nk = s // MLA_TQ
    return pl.pallas_call(
        _mla_kernel,
        grid=(b, groups, s // MLA_TQ),
        in_specs=[pl.BlockSpec((None, MLA_HPS, HEAD_SLOT, MLA_TQ), lambda bi, g, i: (bi, g, 0, i)),
                  pl.BlockSpec((None, s, MLA_HPS * HEAD_SLOT), lambda bi, g, i: (bi, 0, g)),
                  pl.BlockSpec((None, MLA_HPS, nk, MLA_V, MLA_TQ),
                               lambda bi, g, i: (bi, g, 0, 0, 0))],
        out_specs=pl.BlockSpec((None, MLA_TQ, MLA_HPS * MLA_V), lambda bi, g, i: (bi, i, g)),
        out_shape=jax.ShapeDtypeStruct((b, s, MLA_WIDTH), BF16),
        scratch_shapes=[pltpu.VMEM((MLA_HPS, 1, MLA_TQ), F32),
                        pltpu.VMEM((MLA_HPS, MLA_V + SUM_ROWS, MLA_TQ), F32),
                        pltpu.VMEM((MLA_LOOKAHEAD, MLA_TQ, MLA_TQ), F32)],
        compiler_params=pltpu.CompilerParams(
            dimension_semantics=("arbitrary", "arbitrary", "arbitrary"),
            vmem_limit_bytes=VMEM_LIMIT),
    )(qt, km, vt)


def _ca_kernel(qt_ref, k_ref, vt_ref, bias_ref, o_ref):
    lead = KV_PAD - CA_PAD
    first = lax.broadcasted_iota(jnp.int32, (LANES, 1), 0) < CA_HEAD_DIM
    ones = jnp.ones((SUM_ROWS, BLK), BF16)
    items = [(sub, h) for sub in range(CA_QBLOCKS) for h in range(CA_HEADS)]
    pieces = range(CA_WIN // BLK)

    def block(sub):
        return pl.program_id(1) * CA_QBLOCKS + sub

    def scores(sub, h):
        sl = slice((h // 2) * LANES, (h // 2 + 1) * LANES)
        qt = qt_ref[h // 2, :, sub * CA_TQ:(sub + 1) * CA_TQ]
        qh = jnp.where(first if h % 2 == 0 else ~first, qt, jnp.zeros_like(qt))
        row0 = lead + block(sub) * CA_TQ
        return [_dot(k_ref[pl.ds(pl.multiple_of(row0 + kb * BLK, BLK), BLK), sl], qh)
                for kb in pieces]

    def attend(sub, h, probs):
        blk0 = lead // BLK + block(sub)
        acc = None
        for kb in pieces:
            v1 = jnp.concatenate([vt_ref[h, blk0 + kb], ones], axis=0)
            part = _dot(v1, probs[kb])
            acc = part if acc is None else acc + part
        return acc[:CA_HEAD_DIM] / acc[CA_HEAD_DIM:CA_HEAD_DIM + 1]

    pending = {it: scores(*it) for it in items[:CA_LOOKAHEAD]}
    probs = {}
    outs = {}

    def flush(sub, h):
        outs[h] = attend(sub, h, probs.pop((sub, h)))
        if h % 2 == 1:
            ot = jnp.concatenate([outs.pop(h - 1), outs.pop(h)], axis=0)
            o_ref[sub * CA_TQ:(sub + 1) * CA_TQ, (h // 2) * LANES:(h // 2 + 1) * LANES] = (
                ot.T.astype(BF16))

    for n, (sub, h) in enumerate(items):
        st = pending.pop((sub, h))
        if n + CA_LOOKAHEAD < len(items):
            ahead = items[n + CA_LOOKAHEAD]
            pending[ahead] = scores(*ahead)
        variant = jnp.minimum(block(sub), CA_VARIANTS - 1)
        st = [st[kb] + bias_ref[variant, h, kb * BLK:(kb + 1) * BLK, :] for kb in pieces]
        m = functools.reduce(jnp.maximum, [jnp.max(t, axis=0, keepdims=True) for t in st])
        probs[(sub, h)] = [jnp.exp2(t - m).astype(BF16) for t in st]
        if n >= CA_PV_DELAY:
            flush(*items[n - CA_PV_DELAY])
    for it in items[len(items) - CA_PV_DELAY:]:
        flush(*it)


def _chunk_attention(l, qct, kc, vct, bias):
    b, _, _, s = qct.shape
    nblk = vct.shape[2]
    tq = CA_QBLOCKS * CA_TQ
    return pl.pallas_call(
        _ca_kernel,
        grid=(b, s // tq),
        in_specs=[pl.BlockSpec((None, CA_WIDTH // LANES, LANES, tq), lambda bi, i: (bi, 0, 0, i)),
                  pl.BlockSpec((None, s + KV_PAD, CA_WIDTH), lambda bi, i: (bi, 0, 0)),
                  pl.BlockSpec((None, CA_HEADS, nblk, CA_HEAD_DIM, BLK),
                               lambda bi, i: (bi, 0, 0, 0, 0)),
                  pl.BlockSpec((None, CA_VARIANTS, CA_HEADS, CA_WIN, CA_TQ),
                               lambda bi, i: (l, 0, 0, 0, 0), pipeline_mode=pl.Buffered(1))],
        out_specs=pl.BlockSpec((None, tq, CA_WIDTH), lambda bi, i: (bi, i, 0)),
        out_shape=jax.ShapeDtypeStruct((b, s, CA_WIDTH), BF16),
        compiler_params=pltpu.CompilerParams(
            dimension_semantics=("arbitrary", "arbitrary"), vmem_limit_bytes=VMEM_LIMIT),
    )(qct, kc, vct, bias)


def _ffn_kernel(x_ref, oa_ref, ob_ref, goa_ref, gob_ref, wout_ref, gffn_ref, wup_ref, cw_ref,
                cb_ref, wdown_ref, out_ref, carry_ref, u_ref, h_ref):
    tm = TM_FFN

    @pl.when(pl.program_id(1) == 0)
    def _():
        carry_ref[...] = jnp.zeros(carry_ref.shape, F32)

    na = _rms(oa_ref[...].astype(F32), goa_ref[...]).astype(BF16)
    nb = _rms(ob_ref[...].astype(F32), gob_ref[...]).astype(BF16)
    x1 = (x_ref[...] + _dot(na, wout_ref[0:MLA_WIDTH, :])
          + _dot(nb, wout_ref[MLA_WIDTH:MLA_WIDTH + CA_WIDTH, :]))
    out_ref[...] = x1
    h_ref[...] = _rms(x1, gffn_ref[...]).astype(BF16)

    def up(c, slot):
        u = _dot(h_ref[...], wup_ref[c])
        u_ref[slot, 0:CARRY_ROWS, :] = carry_ref[c]
        u_ref[slot, CARRY_ROWS:CARRY_ROWS + tm, :] = u
        carry_ref[c] = u[tm - CARRY_ROWS:, :]

    def act(c, slot):
        cw = cw_ref[c]
        y = cb_ref[c]
        for tap in range(CONV_W):
            off = CARRY_ROWS - (CONV_W - 1) + tap
            y = y + u_ref[slot, pl.ds(off, tm), :] * cw[tap:tap + 1, :]
        g = y[:, :FF_CHUNK]
        return (g * (1.0 / (1.0 + jnp.exp(-g))) * y[:, FF_CHUNK:]).astype(BF16)

    def down(c, a):
        out_ref[...] += _dot(a, wdown_ref[c])

    up(0, 0)

    def pair(c, d):
        up(c + 1, (d + 1) % FFN_SLOTS)
        a0 = act(c, d)
        up(c + 2, (d + 2) % FFN_SLOTS)
        down(c, a0)
        a1 = act(c + 1, (d + 1) % FFN_SLOTS)
        down(c + 1, a1)

    def body(k, carry):
        c = FFN_SLOTS * k
        for d in range(0, FFN_SLOTS, 2):
            pair(c + d, d)
        return carry

    n_pairs = (N_FF_CHUNKS - 1) // 2
    n_loop = n_pairs // (FFN_SLOTS // 2)
    lax.fori_loop(0, n_loop, body, 0)
    for c in range(n_loop * FFN_SLOTS, N_FF_CHUNKS - 1, 2):
        pair(c, c % FFN_SLOTS)
    last = N_FF_CHUNKS - 1
    down(last, act(last, last % FFN_SLOTS))


def _outproj_ffn(l, x, oa, ob, wp):
    b, s, _ = x.shape

    def tok(width):
        return pl.BlockSpec((None, TM_FFN, width), lambda bi, i: (bi, i, 0))

    def lw(shape):
        return pl.BlockSpec((None,) + shape, lambda bi, i: (l,) + (0,) * len(shape),
                            pipeline_mode=pl.Buffered(1))

    return pl.pallas_call(
        _ffn_kernel,
        grid=(b, s // TM_FFN),
        in_specs=[tok(D_MODEL), tok(MLA_WIDTH), tok(CA_WIDTH),
                  lw((1, MLA_WIDTH)), lw((1, CA_WIDTH)), lw((D_MODEL, D_MODEL)),
                  lw((1, D_MODEL)), lw((N_FF_CHUNKS, D_MODEL, 2 * FF_CHUNK)),
                  lw((N_FF_CHUNKS, CARRY_ROWS, 2 * FF_CHUNK)),
                  lw((N_FF_CHUNKS, 1, 2 * FF_CHUNK)),
                  lw((N_FF_CHUNKS, FF_CHUNK, D_MODEL))],
        out_specs=tok(D_MODEL),
        out_shape=jax.ShapeDtypeStruct((b, s, D_MODEL), F32),
        scratch_shapes=[pltpu.VMEM((N_FF_CHUNKS, CARRY_ROWS, 2 * FF_CHUNK), F32),
                        pltpu.VMEM((FFN_SLOTS, CARRY_ROWS + TM_FFN, 2 * FF_CHUNK), F32),
                        pltpu.VMEM((TM_FFN, D_MODEL), BF16)],
        compiler_params=pltpu.CompilerParams(
            dimension_semantics=("arbitrary", "arbitrary"), vmem_limit_bytes=VMEM_LIMIT),
    )(x, oa, ob, wp["g_oa"], wp["g_ob"], wp["w_out"], wp["g_ffn"], wp["w_up"], wp["conv_w"],
      wp["conv_b"], wp["w_down"])


def _rope_slot(first, second, axis_pad):
    t = jnp.concatenate([first, second], axis=-1)
    return jnp.pad(t, axis_pad + ((MLA_NOPE, HEAD_SLOT - MLA_QK),))


def _prepare(g_mix, w_in, w_uq, w_ukv, g_q_lora, g_kv_lora, g_mla_q, g_mla_k, g_ca_q, g_ca_k,
             rel_bias, g_out_mla, g_out_ca, w_out, g_ffn, w_up, conv_w, conv_b, w_down):
    depth = w_in.shape[0]
    row = lambda g: g[:, None, :]
    zpad = HEAD_SLOT - MLA_QK
    r1 = slice(MLA_NOPE, MLA_NOPE + HALF_ROPE)
    r2 = slice(MLA_NOPE + HALF_ROPE, MLA_QK)

    w_kr = w_in[:, :, OFF_KR:OFF_CA]
    kr_slot = _rope_slot(w_kr[..., :HALF_ROPE], w_kr[..., HALF_ROPE:], ((0, 0), (0, 0)))
    kr_swap = _rope_slot(w_kr[..., HALF_ROPE:], w_kr[..., :HALF_ROPE], ((0, 0), (0, 0)))
    w1 = jnp.concatenate([w_in[:, :, OFF_CQ:OFF_KR], kr_slot, kr_swap, w_in[:, :, OFF_CA:]],
                         axis=-1)

    wq = w_uq.reshape(depth, Q_LORA, MLA_HEADS, MLA_QK)
    wq_slot = jnp.pad(wq, ((0, 0), (0, 0), (0, 0), (0, zpad))).reshape(depth, Q_LORA, -1)
    wq_swap = _rope_slot(wq[..., r2], wq[..., r1], ((0, 0), (0, 0), (0, 0)))
    wq2 = jnp.concatenate([wq_slot, wq_swap.reshape(depth, Q_LORA, -1)], axis=-1)
    wkv = w_ukv.reshape(depth, KV_LORA, MLA_HEADS, MLA_NOPE + MLA_V)
    wk = jnp.pad(wkv[..., :MLA_NOPE], ((0, 0), (0, 0), (0, 0), (0, HEAD_SLOT - MLA_NOPE)))
    wk = wk.reshape(depth, KV_LORA, -1)
    wv = wkv[..., MLA_NOPE:].reshape(depth, KV_LORA, MLA_WIDTH)

    gq = g_mla_q * (MLA_QK ** -0.5 * LOG2_E)
    g_q = jnp.pad(gq, ((0, 0), (0, zpad)))
    g_qs = _rope_slot(gq[:, r2], gq[:, r1], ((0, 0),))
    g_k = jnp.pad(g_mla_k, ((0, 0), (0, zpad)))
    g_ks = _rope_slot(g_mla_k[:, r2], g_mla_k[:, r1], ((0, 0),))
    g_caq = jnp.tile(g_ca_q * (CA_HEAD_DIM ** -0.5 * LOG2_E), (1, CA_HEADS))
    g_cak = jnp.tile(g_ca_k, (1, CA_HEADS))
    head = jnp.arange(BD_WIDTH) // CA_HEAD_DIM
    bd = (head[:, None] == head[None, :]).astype(BF16)

    ext = jnp.pad(rel_bias, ((0, 0), (0, 0), (CA_TQ - 1 - REL_CLIP, CA_WIN - REL_CLIP)),
                  mode="edge")
    base = ext[:, :, ::-1].reshape(depth * CA_HEADS, 1, BASE_W)

    def chunks(w):
        g = w[..., :D_FF].reshape(w.shape[:-1] + (N_FF_CHUNKS, FF_CHUNK))
        v = w[..., D_FF:].reshape(w.shape[:-1] + (N_FF_CHUNKS, FF_CHUNK))
        return jnp.concatenate([g, v], axis=-1)

    w_up_c = jnp.moveaxis(chunks(w_up), 2, 1)
    conv_c = jnp.moveaxis(chunks(conv_w), 2, 1)
    conv_c = jnp.pad(conv_c, ((0, 0), (0, 0), (0, CARRY_ROWS - CONV_W), (0, 0)))
    cb_c = chunks(conv_b)[:, :, None, :]
    return {
        "g_mix": row(g_mix), "w1": w1.astype(BF16), "g_qa": row(g_q_lora),
        "g_kva": row(g_kv_lora), "w_uq": wq2.astype(BF16), "w_uk": wk.astype(BF16),
        "w_uv": wv.astype(BF16), "g_q": row(g_q), "g_qs": row(g_qs), "g_k": row(g_k),
        "g_ks": row(g_ks), "g_caq": row(g_caq), "g_cak": row(g_cak), "bd": bd, "base": base,
        "g_oa": row(g_out_mla), "g_ob": row(g_out_ca), "w_out": w_out.astype(BF16),
        "g_ffn": row(g_ffn), "w_up": w_up_c.astype(BF16), "conv_w": conv_c, "conv_b": cb_c,
        "w_down": w_down.reshape(depth, N_FF_CHUNKS, FF_CHUNK, D_MODEL).astype(BF16),
    }


def kernel(x, positions, g_mix, w_in, w_uq, w_ukv, g_q_lora, g_kv_lora, g_mla_q, g_mla_k,
           g_ca_q, g_ca_k, rel_bias, g_out_mla, g_out_ca, w_out, g_ffn, w_up, conv_w, conv_b,
           w_down):
    b, s, _ = x.shape
    depth = w_in.shape[0]
    assert s % TM_IN == 0 and s % TM_FFN == 0 and KV_PAD % TM_IN == 0
    assert (KV_PAD - CA_PAD) % BLK == 0
    wp = _prepare(g_mix, w_in, w_uq, w_ukv, g_q_lora, g_kv_lora, g_mla_q, g_mla_k, g_ca_q,
                  g_ca_k, rel_bias, g_out_mla, g_out_ca, w_out, g_ffn, w_up, conv_w, conv_b,
                  w_down)

    inv = ROPE_BASE ** (-jnp.arange(0, MLA_ROPE, 2, dtype=F32) / MLA_ROPE)
    inv_full = jnp.zeros((1, LANES), F32)
    inv_full = inv_full.at[0, MLA_NOPE:MLA_NOPE + HALF_ROPE].set(inv)
    inv_full = inv_full.at[0, MLA_NOPE + HALF_ROPE:MLA_QK].set(inv)
    pos_f = positions.astype(F32).reshape(b * s, 1)
    cos, sin = [t.reshape(b, s, LANES) for t in _rope_tables(pos_f, inv_full)]
    bias = _bias_tiles(wp["base"], depth)

    kc = jnp.zeros((b, s + KV_PAD, CA_WIDTH), BF16)
    vct = jnp.zeros((b, CA_HEADS, (s + KV_PAD) // BLK, CA_HEAD_DIM, BLK), BF16)
    for l in range(depth):
        qt, km, vt, qct, kc, vct = _inproj(l, x, cos, sin, wp, kc, vct)
        oa = _mla_attention(qt, km, vt)
        ob = _chunk_attention(l, qct, kc, vct, bias)
        x = _outproj_ffn(l, x, oa, ob, wp)
    return x
```

```python
import functools

import jax
import jax.numpy as jnp
from jax import lax
from jax.experimental import pallas as pl
from jax.experimental.pallas import tpu as pltpu

F32 = jnp.float32
BF16 = jnp.bfloat16

D_MODEL = 1024
CHUNK = 64
EPS = 1e-6
NEG_INF = -1e30
LOG2_E = 1.4426950408889634

MLA_HEADS = 8
MLA_NOPE = 64
MLA_ROPE = 32
MLA_QK = MLA_NOPE + MLA_ROPE
MLA_V = 64
Q_LORA = 256
KV_LORA = 128
ROPE_BASE = 10000.0
MLA_WIDTH = MLA_HEADS * MLA_V

CA_HEADS = 8
CA_HEAD_DIM = 64
CA_WIDTH = CA_HEADS * CA_HEAD_DIM
LEFT_CHUNKS = 8
REL_CLIP = 128

OFF_CQ = 0
OFF_CKV = OFF_CQ + Q_LORA
OFF_KR = OFF_CKV + KV_LORA
OFF_CA = OFF_KR + MLA_ROPE

D_FF = 2816
CONV_W = 3

LANES = 128
HEAD_SLOT = LANES
HALF_ROPE = MLA_ROPE // 2
ROPE_PACK = LANES // MLA_ROPE
BD_WIDTH = 256
V_DIM = MLA_V
assert V_DIM == CA_HEAD_DIM and MLA_HEADS == CA_HEADS and 2 * V_DIM == LANES
MLA_SLOTS = MLA_HEADS * HEAD_SLOT

P_CQ = 0
P_CKV = P_CQ + Q_LORA
P_KR = P_CKV + KV_LORA
P_KRS = P_KR + HEAD_SLOT
P_CAQ = P_KRS + HEAD_SLOT
P_CAK = P_CAQ + CA_WIDTH
P_CAV = P_CAK + CA_WIDTH
P_COLS = P_CAV + CA_WIDTH

SUM_ROWS = 16
BLK = 256
TM_IN = 1024
CA_PAD = LEFT_CHUNKS * CHUNK
KV_PAD = max(CA_PAD, TM_IN)
CA_TQ = BLK
CA_WIN = CA_PAD + CA_TQ
BASE_W = CA_WIN + CA_TQ
CA_VARIANTS = CA_PAD // CA_TQ + 1
CA_LOOKAHEAD = 3
CA_PV_DELAY = 3
CA_QBLOCKS = 4
MLA_TQ = BLK
MLA_HPS = 8
MLA_LOOKAHEAD = 4
MLA_PV_DELAY = 1
TM_FFN = 512
FF_CHUNK = 256
N_FF_CHUNKS = D_FF // FF_CHUNK
CARRY_ROWS = 8
FFN_SLOTS = 4

V7X_VMEM_BYTES = 64 * 1024 * 1024
VMEM_LIMIT = V7X_VMEM_BYTES * 7 // 8


def _rms(x, g):
    return x * lax.rsqrt(jnp.mean(x * x, axis=-1, keepdims=True) + EPS) * g


def _dot(a, b):
    return jnp.dot(a, b, preferred_element_type=F32)


def _rope_kernel(pos_ref, inv_ref, c_ref, s_ref):
    lane = lax.broadcasted_iota(jnp.int32, (1, LANES), 1)
    pos = pos_ref[:, 0:1]
    for m in range(1, ROPE_PACK):
        pos = jnp.where(lane // MLA_ROPE == m, pos_ref[:, m:m + 1], pos)
    ang = pos * inv_ref[...]
    c = jnp.cos(ang)
    s = jnp.sin(ang)
    s = jnp.where(lane % MLA_ROPE < HALF_ROPE, -s, s)
    rotary = (lane >= MLA_NOPE) & (lane < MLA_QK)
    for m in range(ROPE_PACK):
        shift = (MLA_NOPE - m * MLA_ROPE) % LANES
        c_ref[m] = jnp.where(rotary, pltpu.roll(c, shift, 1) if shift else c, 1.0)
        s_ref[m] = jnp.where(rotary, pltpu.roll(s, shift, 1) if shift else s, 0.0)


def _rope_tables(positions):
    t = positions.size
    quarter = t // ROPE_PACK
    rows = 256
    assert t % (ROPE_PACK * rows) == 0
    pos = positions.astype(F32).reshape(ROPE_PACK, quarter).T
    inv = ROPE_BASE ** (-jnp.arange(0, MLA_ROPE, 2, dtype=F32) / MLA_ROPE)
    inv = jnp.tile(jnp.concatenate([inv, inv]), ROPE_PACK)[None, :]
    out = jax.ShapeDtypeStruct((ROPE_PACK, quarter, LANES), F32)
    cos, sin = pl.pallas_call(
        _rope_kernel,
        grid=(quarter // rows,),
        in_specs=[pl.BlockSpec((rows, ROPE_PACK), lambda i: (i, 0)),
                  pl.BlockSpec((1, LANES), lambda i: (0, 0))],
        out_specs=[pl.BlockSpec((ROPE_PACK, rows, LANES), lambda i: (0, i, 0))] * 2,
        out_shape=[out] * 2,
        compiler_params=pltpu.CompilerParams(dimension_semantics=("arbitrary",)),
    )(pos, inv)
    shape = positions.shape + (LANES,)
    return cos.reshape(shape), sin.reshape(shape)


def _bias_kernel(base_ref, out_ref):
    x = pltpu.roll(jnp.broadcast_to(base_ref[...], (CA_TQ, BASE_W)), 0, 1,
                   stride=1, stride_axis=0)
    t = x[:, CA_TQ:] * LOG2_E
    i = lax.broadcasted_iota(jnp.int32, (CA_TQ, CA_WIN), 0) // CHUNK
    j = lax.broadcasted_iota(jnp.int32, (CA_TQ, CA_WIN), 1) // CHUNK
    band = (j >= i) & (j <= i + LEFT_CHUNKS)
    tile = jnp.where(band, t, NEG_INF).T
    key = lax.broadcasted_iota(jnp.int32, (CA_WIN, CA_TQ), 0)
    for v in range(CA_VARIANTS):
        out_ref[v] = jnp.where(key >= CA_PAD - v * CA_TQ, tile, NEG_INF)


def _bias_tiles(base, depth):
    return pl.pallas_call(
        _bias_kernel,
        grid=(depth * CA_HEADS,),
        in_specs=[pl.BlockSpec((None, 1, BASE_W), lambda i: (i, 0, 0))],
        out_specs=pl.BlockSpec((None, CA_VARIANTS, None, CA_WIN, CA_TQ),
                               lambda i: (i // CA_HEADS, 0, i % CA_HEADS, 0, 0)),
        out_shape=jax.ShapeDtypeStruct((depth, CA_VARIANTS, CA_HEADS, CA_WIN, CA_TQ), F32),
        compiler_params=pltpu.CompilerParams(dimension_semantics=("arbitrary",)),
    )(base)


def _inproj_kernel(x_ref, c_ref, s_ref, gmix_ref, w1_ref, gqa_ref, gkva_ref, wuq_ref, wuk_ref,
                   wuv_ref, gq_ref, gqs_ref, gk_ref, gks_ref, gcaq_ref, gcak_ref, bd_ref,
                   kpad_ref, vpad_ref,
                   qt_ref, km_ref, vt_ref, qct_ref, kc_ref, vct_ref):
    del kpad_ref, vpad_ref
    n_sub = TM_IN // BLK

    def project(r):
        rows = slice(r * BLK, (r + 1) * BLK)
        h = _rms(x_ref[rows, :], gmix_ref[...]).astype(BF16)
        return _dot(h, w1_ref[...])

    def head_sums(t):
        sq = (t * t).astype(BF16)
        return jnp.concatenate([_dot(sq[:, c:c + BD_WIDTH], bd_ref[...])
                                for c in range(0, CA_WIDTH, BD_WIDTH)], axis=1)

    def up_project(proj):
        cqn = _rms(proj[:, P_CQ:P_CKV], gqa_ref[...]).astype(BF16)
        ckvn = _rms(proj[:, P_CKV:P_KR], gkva_ref[...]).astype(BF16)
        caq = proj[:, P_CAQ:P_CAK]
        cak = proj[:, P_CAK:P_CAV]
        return {
            "q2": _dot(cqn, wuq_ref[...]),
            "kn": _dot(ckvn, wuk_ref[...]),
            "v": _dot(ckvn, wuv_ref[...]),
            "ssq": head_sums(caq),
            "ssk": head_sums(cak),
        }

    def transpose_heads(t, ref, r):
        for p in range(t.shape[1] // LANES):
            tt = t[:, p * LANES:(p + 1) * LANES].T.astype(BF16)
            ref[2 * p, r] = tt[:V_DIM]
            ref[2 * p + 1, r] = tt[V_DIM:]

    def finish(r, proj, up):
        rows = slice(r * BLK, (r + 1) * BLK)
        cos = c_ref[rows, :]
        sin = s_ref[rows, :]
        gq_c, gq_s = gq_ref[...] * cos, gqs_ref[...] * sin
        gk_c, gk_s = gk_ref[...] * cos, gks_ref[...] * sin
        kr = proj[:, P_KR:P_KRS]
        kr_s = proj[:, P_KRS:P_CAQ] * gk_s
        for hd in range(MLA_HEADS):
            sl = slice(hd * HEAD_SLOT, (hd + 1) * HEAD_SLOT)
            ssl = slice(MLA_SLOTS + hd * HEAD_SLOT, MLA_SLOTS + (hd + 1) * HEAD_SLOT)
            q = up["q2"][:, sl]
            rq = lax.rsqrt(jnp.sum(q * q, axis=-1, keepdims=True) * (1.0 / MLA_QK) + EPS)
            qr = (q * gq_c + up["q2"][:, ssl] * gq_s) * rq
            qt_ref[hd, :, rows] = qr.T.astype(BF16)
            k = up["kn"][:, sl] + kr
            rk = lax.rsqrt(jnp.sum(k * k, axis=-1, keepdims=True) * (1.0 / MLA_QK) + EPS)
            km_ref[rows, sl] = ((k * gk_c + kr_s) * rk).astype(BF16)
        transpose_heads(up["v"], vt_ref, r)
        caq = proj[:, P_CAQ:P_CAK]
        cak = proj[:, P_CAK:P_CAV]
        qcn = caq * lax.rsqrt(up["ssq"] * (1.0 / CA_HEAD_DIM) + EPS) * gcaq_ref[...]
        for p in range(CA_WIDTH // LANES):
            qct_ref[p, :, rows] = qcn[:, p * LANES:(p + 1) * LANES].T.astype(BF16)
        kc_ref[rows, :] = (cak * lax.rsqrt(up["ssk"] * (1.0 / CA_HEAD_DIM) + EPS)
                           * gcak_ref[...]).astype(BF16)
        transpose_heads(proj[:, P_CAV:P_COLS], vct_ref, r)

    proj = project(0)
    for r in range(n_sub):
        nxt = project(r + 1) if r + 1 < n_sub else None
        finish(r, proj, up_project(proj))
        proj = nxt


def _inproj(l, x, cos, sin, wp, kpad, vpad):
    b, s, _ = x.shape
    n_sub = TM_IN // BLK
    pad_tiles = KV_PAD // TM_IN

    def tok(width):
        return pl.BlockSpec((None, TM_IN, width), lambda bi, i: (bi, i, 0))

    def lw(shape):
        return pl.BlockSpec((None,) + shape, lambda bi, i: (l,) + (0,) * len(shape),
                            pipeline_mode=pl.Buffered(1))

    const = pl.BlockSpec((BD_WIDTH, BD_WIDTH), lambda bi, i: (0, 0),
                         pipeline_mode=pl.Buffered(1))
    any_spec = pl.BlockSpec(memory_space=pl.ANY)
    heads_t = lambda off: pl.BlockSpec((None, MLA_HEADS, n_sub, V_DIM, BLK),
                                       lambda bi, i: (bi, 0, i + off, 0, 0))
    tok_shape = lambda w: jax.ShapeDtypeStruct((b, s, w), BF16)
    return pl.pallas_call(
        _inproj_kernel,
        grid=(b, s // TM_IN),
        in_specs=[tok(D_MODEL), tok(LANES), tok(LANES),
                  lw((1, D_MODEL)), lw((D_MODEL, P_COLS)), lw((1, Q_LORA)), lw((1, KV_LORA)),
                  lw((Q_LORA, 2 * MLA_SLOTS)), lw((KV_LORA, MLA_SLOTS)),
                  lw((KV_LORA, MLA_WIDTH)), lw((1, HEAD_SLOT)), lw((1, HEAD_SLOT)),
                  lw((1, HEAD_SLOT)), lw((1, HEAD_SLOT)),
                  lw((1, CA_WIDTH)), lw((1, CA_WIDTH)), const, any_spec, any_spec],
        out_specs=[pl.BlockSpec((None, MLA_HEADS, HEAD_SLOT, TM_IN), lambda bi, i: (bi, 0, 0, i)),
                   tok(MLA_SLOTS), heads_t(0),
                   pl.BlockSpec((None, CA_WIDTH // LANES, LANES, TM_IN),
                                lambda bi, i: (bi, 0, 0, i)),
                   pl.BlockSpec((None, TM_IN, CA_WIDTH), lambda bi, i: (bi, i + pad_tiles, 0)),
                   heads_t(pad_tiles)],
        out_shape=[jax.ShapeDtypeStruct((b, MLA_HEADS, HEAD_SLOT, s), BF16), tok_shape(MLA_SLOTS),
                   jax.ShapeDtypeStruct((b, MLA_HEADS, s // BLK, MLA_V, BLK), BF16),
                   jax.ShapeDtypeStruct((b, CA_WIDTH // LANES, LANES, s), BF16),
                   jax.ShapeDtypeStruct(kpad.shape, kpad.dtype),
                   jax.ShapeDtypeStruct(vpad.shape, vpad.dtype)],
        input_output_aliases={17: 4, 18: 5},
        compiler_params=pltpu.CompilerParams(
            dimension_semantics=("arbitrary", "arbitrary"), vmem_limit_bytes=VMEM_LIMIT),
    )(x, cos, sin, wp["g_mix"], wp["w1"], wp["g_qa"], wp["g_kva"], wp["w_uq"], wp["w_uk"],
      wp["w_uv"], wp["g_q"], wp["g_qs"], wp["g_k"], wp["g_ks"], wp["g_caq"], wp["g_cak"],
      wp["bd"], kpad, vpad)


def _mla_kernel(qt_ref, k_ref, vt_ref, o_ref, m_ref, acc_ref, s_ref):
    i = pl.program_id(2)
    t = MLA_TQ
    ones = jnp.ones((SUM_ROWS, t), BF16)
    m_ref[...] = jnp.full(m_ref.shape, NEG_INF, F32)
    acc_ref[...] = jnp.zeros(acc_ref.shape, F32)

    def scores(j, hh):
        rows = pl.ds(pl.multiple_of(j * t, t), t)
        sl = slice(hh * HEAD_SLOT, (hh + 1) * HEAD_SLOT)
        return _dot(k_ref[rows, sl], qt_ref[hh])

    def step(j, last):
        def accumulate(hh, p, alpha):
            v1 = jnp.concatenate([vt_ref[hh, j], ones], axis=0)
            acc_ref[hh] = acc_ref[hh] * alpha + _dot(v1, p)

        pending = {hh: s_ref[hh] for hh in range(MLA_LOOKAHEAD)}
        done = {}
        for hh in range(MLA_HPS):
            st = pending.pop(hh)
            ahead = hh + MLA_LOOKAHEAD
            if ahead < MLA_HPS:
                pending[ahead] = scores(j, ahead)
            elif not last:
                s_ref[ahead - MLA_HPS] = scores(j + 1, ahead - MLA_HPS)
            if last:
                kc = lax.broadcasted_iota(jnp.int32, st.shape, 0) // CHUNK
                qc = lax.broadcasted_iota(jnp.int32, st.shape, 1) // CHUNK
                st = jnp.where(kc <= qc, st, NEG_INF)
            m_prev = m_ref[hh]
            m_new = jnp.maximum(m_prev, jnp.max(st, axis=0, keepdims=True))
            m_ref[hh] = m_new
            done[hh] = (jnp.exp2(st - m_new).astype(BF16), jnp.exp2(m_prev - m_new))
            if hh >= MLA_PV_DELAY:
                accumulate(hh - MLA_PV_DELAY, *done.pop(hh - MLA_PV_DELAY))
        for hh in sorted(done):
            accumulate(hh, *done[hh])

    for hh in range(MLA_LOOKAHEAD):
        s_ref[hh] = scores(0, hh)

    def body(k, carry):
        step(2 * k, False)
        step(2 * k + 1, False)
        return carry

    lax.fori_loop(0, i // 2, body, 0)

    @pl.when(i % 2 == 1)
    def _():
        step(i - 1, False)

    step(i, True)
    for pr in range(MLA_HPS // 2):
        ot = jnp.concatenate([acc_ref[hh, :MLA_V, :] / acc_ref[hh, MLA_V:MLA_V + 1, :]
                              for hh in (2 * pr, 2 * pr + 1)], axis=0)
        o_ref[:, pr * LANES:(pr + 1) * LANES] = ot.T.astype(BF16)


def _mla_attention(qt, km, vt):
    b, s, _ = km.shape
    groups = MLA_HEADS // MLA_HPS
    nk = s // MLA_TQ
    return pl.pallas_call(
        _mla_kernel,
        grid=(b, groups, s // MLA_TQ),
        in_specs=[pl.BlockSpec((None, MLA_HPS, HEAD_SLOT, MLA_TQ), lambda bi, g, i: (bi, g, 0, i)),
                  pl.BlockSpec((None, s, MLA_HPS * HEAD_SLOT), lambda bi, g, i: (bi, 0, g)),
                  pl.BlockSpec((None, MLA_HPS, nk, MLA_V, MLA_TQ),
                               lambda bi, g, i: (bi, g, 0, 0, 0))],
        out_specs=pl.BlockSpec((None, MLA_TQ, MLA_HPS * MLA_V), lambda bi, g, i: (bi, i, g)),
        out_shape=jax.ShapeDtypeStruct((b, s, MLA_WIDTH), BF16),
        scratch_shapes=[pltpu.VMEM((MLA_HPS, 1, MLA_TQ), F32),
                        pltpu.VMEM((MLA_HPS, MLA_V + SUM_ROWS, MLA_TQ), F32),
                        pltpu.VMEM((MLA_LOOKAHEAD, MLA_TQ, MLA_TQ), F32)],
        compiler_params=pltpu.CompilerParams(
            dimension_semantics=("arbitrary", "arbitrary", "arbitrary"),
            vmem_limit_bytes=VMEM_LIMIT),
    )(qt, km, vt)


def _ca_kernel(qt_ref, k_ref, vt_ref, bias_ref, o_ref):
    lead = KV_PAD - CA_PAD
    first = lax.broadcasted_iota(jnp.int32, (LANES, 1), 0) < CA_HEAD_DIM
    ones = jnp.ones((SUM_ROWS, BLK), BF16)
    items = [(sub, h) for sub in range(CA_QBLOCKS) for h in range(CA_HEADS)]
    pieces = range(CA_WIN // BLK)

    def block(sub):
        return pl.program_id(1) * CA_QBLOCKS + sub

    def scores(sub, h):
        sl = slice((h // 2) * LANES, (h // 2 + 1) * LANES)
        qt = qt_ref[h // 2, :, sub * CA_TQ:(sub + 1) * CA_TQ]
        qh = jnp.where(first if h % 2 == 0 else ~first, qt, jnp.zeros_like(qt))
        row0 = lead + block(sub) * CA_TQ
        return [_dot(k_ref[pl.ds(pl.multiple_of(row0 + kb * BLK, BLK), BLK), sl], qh)
                for kb in pieces]

    def attend(sub, h, probs):
        blk0 = lead // BLK + block(sub)
        acc = None
        for kb in pieces:
            v1 = jnp.concatenate([vt_ref[h, blk0 + kb], ones], axis=0)
            part = _dot(v1, probs[kb])
            acc = part if acc is None else acc + part
        return acc[:CA_HEAD_DIM] / acc[CA_HEAD_DIM:CA_HEAD_DIM + 1]

    pending = {it: scores(*it) for it in items[:CA_LOOKAHEAD]}
    probs = {}
    outs = {}

    def flush(sub, h):
        outs[h] = attend(sub, h, probs.pop((sub, h)))
        if h % 2 == 1:
            ot = jnp.concatenate([outs.pop(h - 1), outs.pop(h)], axis=0)
            o_ref[sub * CA_TQ:(sub + 1) * CA_TQ, (h // 2) * LANES:(h // 2 + 1) * LANES] = (
                ot.T.astype(BF16))

    for n, (sub, h) in enumerate(items):
        st = pending.pop((sub, h))
        if n + CA_LOOKAHEAD < len(items):
            ahead = items[n + CA_LOOKAHEAD]
            pending[ahead] = scores(*ahead)
        variant = jnp.minimum(block(sub), CA_VARIANTS - 1)
        st = [st[kb] + bias_ref[variant, h, kb * BLK:(kb + 1) * BLK, :] for kb in pieces]
        m = functools.reduce(jnp.maximum, [jnp.max(t, axis=0, keepdims=True) for t in st])
        probs[(sub, h)] = [jnp.exp2(t - m).astype(BF16) for t in st]
        if n >= CA_PV_DELAY:
            flush(*items[n - CA_PV_DELAY])
    for it in items[len(items) - CA_PV_DELAY:]:
        flush(*it)


def _chunk_attention(l, qct, kc, vct, bias):
    b, _, _, s = qct.shape
    nblk = vct.shape[2]
    tq = CA_QBLOCKS * CA_TQ
    return pl.pallas_call(
        _ca_kernel,
        grid=(b, s // tq),
        in_specs=[pl.BlockSpec((None, CA_WIDTH // LANES, LANES, tq), lambda bi, i: (bi, 0, 0, i)),
                  pl.BlockSpec((None, s + KV_PAD, CA_WIDTH), lambda bi, i: (bi, 0, 0)),
                  pl.BlockSpec((None, CA_HEADS, nblk, CA_HEAD_DIM, BLK),
                               lambda bi, i: (bi, 0, 0, 0, 0)),
                  pl.BlockSpec((None, CA_VARIANTS, CA_HEADS, CA_WIN, CA_TQ),
                               lambda bi, i: (l, 0, 0, 0, 0), pipeline_mode=pl.Buffered(1))],
        out_specs=pl.BlockSpec((None, tq, CA_WIDTH), lambda bi, i: (bi, i, 0)),
        out_shape=jax.ShapeDtypeStruct((b, s, CA_WIDTH), BF16),
        compiler_params=pltpu.CompilerParams(
            dimension_semantics=("arbitrary", "arbitrary"), vmem_limit_bytes=VMEM_LIMIT),
    )(qct, kc, vct, bias)


def _ffn_kernel(x_ref, oa_ref, ob_ref, goa_ref, gob_ref, wout_ref, gffn_ref, wup_ref, cw_ref,
                cb_ref, wdown_ref, out_ref, carry_ref, u_ref, h_ref):
    tm = TM_FFN

    @pl.when(pl.program_id(1) == 0)
    def _():
        carry_ref[...] = jnp.zeros(carry_ref.shape, F32)

    na = _rms(oa_ref[...].astype(F32), goa_ref[...]).astype(BF16)
    nb = _rms(ob_ref[...].astype(F32), gob_ref[...]).astype(BF16)
    x1 = (x_ref[...] + _dot(na, wout_ref[0:MLA_WIDTH, :])
          + _dot(nb, wout_ref[MLA_WIDTH:MLA_WIDTH + CA_WIDTH, :]))
    out_ref[...] = x1
    h_ref[...] = _rms(x1, gffn_ref[...]).astype(BF16)

    def up(c, slot):
        u = _dot(h_ref[...], wup_ref[c])
        u_ref[slot, 0:CARRY_ROWS, :] = carry_ref[c]
        u_ref[slot, CARRY_ROWS:CARRY_ROWS + tm, :] = u
        carry_ref[c] = u[tm - CARRY_ROWS:, :]

    def act(c, slot):
        cw = cw_ref[c]
        y = cb_ref[c]
        for tap in range(CONV_W):
            off = CARRY_ROWS - (CONV_W - 1) + tap
            y = y + u_ref[slot, pl.ds(off, tm), :] * cw[tap:tap + 1, :]
        g = y[:, :FF_CHUNK]
        return (g * (1.0 / (1.0 + jnp.exp(-g))) * y[:, FF_CHUNK:]).astype(BF16)

    def down(c, a):
        out_ref[...] += _dot(a, wdown_ref[c])

    up(0, 0)

    def pair(c, d):
        up(c + 1, (d + 1) % FFN_SLOTS)
        a0 = act(c, d)
        up(c + 2, (d + 2) % FFN_SLOTS)
        down(c, a0)
        a1 = act(c + 1, (d + 1) % FFN_SLOTS)
        down(c + 1, a1)

    def body(k, carry):
        c = FFN_SLOTS * k
        for d in range(0, FFN_SLOTS, 2):
            pair(c + d, d)
        return carry

    n_pairs = (N_FF_CHUNKS - 1) // 2
    n_loop = n_pairs // (FFN_SLOTS // 2)
    lax.fori_loop(0, n_loop, body, 0)
    for c in range(n_loop * FFN_SLOTS, N_FF_CHUNKS - 1, 2):
        pair(c, c % FFN_SLOTS)
    last = N_FF_CHUNKS - 1
    down(last, act(last, last % FFN_SLOTS))


def _outproj_ffn(l, x, oa, ob, wp):
    b, s, _ = x.shape

    def tok(width):
        return pl.BlockSpec((None, TM_FFN, width), lambda bi, i: (bi, i, 0))

    def lw(shape):
        return pl.BlockSpec((None,) + shape, lambda bi, i: (l,) + (0,) * len(shape),
                            pipeline_mode=pl.Buffered(1))

    return pl.pallas_call(
        _ffn_kernel,
        grid=(b, s // TM_FFN),
        in_specs=[tok(D_MODEL), tok(MLA_WIDTH), tok(CA_WIDTH),
                  lw((1, MLA_WIDTH)), lw((1, CA_WIDTH)), lw((D_MODEL, D_MODEL)),
                  lw((1, D_MODEL)), lw((N_FF_CHUNKS, D_MODEL, 2 * FF_CHUNK)),
                  lw((N_FF_CHUNKS, CARRY_ROWS, 2 * FF_CHUNK)),
                  lw((N_FF_CHUNKS, 1, 2 * FF_CHUNK)),
                  lw((N_FF_CHUNKS, FF_CHUNK, D_MODEL))],
        out_specs=tok(D_MODEL),
        out_shape=jax.ShapeDtypeStruct((b, s, D_MODEL), F32),
        scratch_shapes=[pltpu.VMEM((N_FF_CHUNKS, CARRY_ROWS, 2 * FF_CHUNK), F32),
                        pltpu.VMEM((FFN_SLOTS, CARRY_ROWS + TM_FFN, 2 * FF_CHUNK), F32),
                        pltpu.VMEM((TM_FFN, D_MODEL), BF16)],
        compiler_params=pltpu.CompilerParams(
            dimension_semantics=("arbitrary", "arbitrary"), vmem_limit_bytes=VMEM_LIMIT),
    )(x, oa, ob, wp["g_oa"], wp["g_ob"], wp["w_out"], wp["g_ffn"], wp["w_up"], wp["conv_w"],
      wp["conv_b"], wp["w_down"])


def _rope_slot(first, second, axis_pad):
    t = jnp.concatenate([first, second], axis=-1)
    return jnp.pad(t, axis_pad + ((MLA_NOPE, HEAD_SLOT - MLA_QK),))


def _prepare(g_mix, w_in, w_uq, w_ukv, g_q_lora, g_kv_lora, g_mla_q, g_mla_k, g_ca_q, g_ca_k,
             rel_bias, g_out_mla, g_out_ca, w_out, g_ffn, w_up, conv_w, conv_b, w_down):
    depth = w_in.shape[0]
    row = lambda g: g[:, None, :]
    zpad = HEAD_SLOT - MLA_QK
    r1 = slice(MLA_NOPE, MLA_NOPE + HALF_ROPE)
    r2 = slice(MLA_NOPE + HALF_ROPE, MLA_QK)

    w_kr = w_in[:, :, OFF_KR:OFF_CA]
    kr_slot = _rope_slot(w_kr[..., :HALF_ROPE], w_kr[..., HALF_ROPE:], ((0, 0), (0, 0)))
    kr_swap = _rope_slot(w_kr[..., HALF_ROPE:], w_kr[..., :HALF_ROPE], ((0, 0), (0, 0)))
    w1 = jnp.concatenate([w_in[:, :, OFF_CQ:OFF_KR], kr_slot, kr_swap, w_in[:, :, OFF_CA:]],
                         axis=-1)

    wq = w_uq.reshape(depth, Q_LORA, MLA_HEADS, MLA_QK)
    wq_slot = jnp.pad(wq, ((0, 0), (0, 0), (0, 0), (0, zpad))).reshape(depth, Q_LORA, -1)
    wq_swap = _rope_slot(wq[..., r2], wq[..., r1], ((0, 0), (0, 0), (0, 0)))
    wq2 = jnp.concatenate([wq_slot, wq_swap.reshape(depth, Q_LORA, -1)], axis=-1)
    wkv = w_ukv.reshape(depth, KV_LORA, MLA_HEADS, MLA_NOPE + MLA_V)
    wk = jnp.pad(wkv[..., :MLA_NOPE], ((0, 0), (0, 0), (0, 0), (0, HEAD_SLOT - MLA_NOPE)))
    wk = wk.reshape(depth, KV_LORA, -1)
    wv = wkv[..., MLA_NOPE:].reshape(depth, KV_LORA, MLA_WIDTH)

    gq = g_mla_q * (MLA_QK ** -0.5 * LOG2_E)
    g_q = jnp.pad(gq, ((0, 0), (0, zpad)))
    g_qs = _rope_slot(gq[:, r2], gq[:, r1], ((0, 0),))
    g_k = jnp.pad(g_mla_k, ((0, 0), (0, zpad)))
    g_ks = _rope_slot(g_mla_k[:, r2], g_mla_k[:, r1], ((0, 0),))
    g_caq = jnp.tile(g_ca_q * (CA_HEAD_DIM ** -0.5 * LOG2_E), (1, CA_HEADS))
    g_cak = jnp.tile(g_ca_k, (1, CA_HEADS))
    head = jnp.arange(BD_WIDTH) // CA_HEAD_DIM
    bd = (head[:, None] == head[None, :]).astype(BF16)

    ext = jnp.pad(rel_bias, ((0, 0), (0, 0), (CA_TQ - 1 - REL_CLIP, CA_WIN - REL_CLIP)),
                  mode="edge")
    base = ext[:, :, ::-1].reshape(depth * CA_HEADS, 1, BASE_W)

    def chunks(w):
        g = w[..., :D_FF].reshape(w.shape[:-1] + (N_FF_CHUNKS, FF_CHUNK))
        v = w[..., D_FF:].reshape(w.shape[:-1] + (N_FF_CHUNKS, FF_CHUNK))
        return jnp.concatenate([g, v], axis=-1)

    w_up_c = jnp.moveaxis(chunks(w_up), 2, 1)
    conv_c = jnp.moveaxis(chunks(conv_w), 2, 1)
    conv_c = jnp.pad(conv_c, ((0, 0), (0, 0), (0, CARRY_ROWS - CONV_W), (0, 0)))
    cb_c = chunks(conv_b)[:, :, None, :]
    return {
        "g_mix": row(g_mix), "w1": w1.astype(BF16), "g_qa": row(g_q_lora),
        "g_kva": row(g_kv_lora), "w_uq": wq2.astype(BF16), "w_uk": wk.astype(BF16),
        "w_uv": wv.astype(BF16), "g_q": row(g_q), "g_qs": row(g_qs), "g_k": row(g_k),
        "g_ks": row(g_ks), "g_caq": row(g_caq), "g_cak": row(g_cak), "bd": bd, "base": base,
        "g_oa": row(g_out_mla), "g_ob": row(g_out_ca), "w_out": w_out.astype(BF16),
        "g_ffn": row(g_ffn), "w_up": w_up_c.astype(BF16), "conv_w": conv_c, "conv_b": cb_c,
        "w_down": w_down.reshape(depth, N_FF_CHUNKS, FF_CHUNK, D_MODEL).astype(BF16),
    }


def kernel(x, positions, g_mix, w_in, w_uq, w_ukv, g_q_lora, g_kv_lora, g_mla_q, g_mla_k,
           g_ca_q, g_ca_k, rel_bias, g_out_mla, g_out_ca, w_out, g_ffn, w_up, conv_w, conv_b,
           w_down):
    b, s, _ = x.shape
    depth = w_in.shape[0]
    assert s % TM_IN == 0 and s % TM_FFN == 0 and KV_PAD % TM_IN == 0
    assert (KV_PAD - CA_PAD) % BLK == 0
    wp = _prepare(g_mix, w_in, w_uq, w_ukv, g_q_lora, g_kv_lora, g_mla_q, g_mla_k, g_ca_q,
                  g_ca_k, rel_bias, g_out_mla, g_out_ca, w_out, g_ffn, w_up, conv_w, conv_b,
                  w_down)

    cos, sin = _rope_tables(positions)
    bias = _bias_tiles(wp["base"], depth)

    kc = jnp.zeros((b, s + KV_PAD, CA_WIDTH), BF16)
    vct = jnp.zeros((b, CA_HEADS, (s + KV_PAD) // BLK, CA_HEAD_DIM, BLK), BF16)
    for l in range(depth):
        qt, km, vt, qct, kc, vct = _inproj(l, x, cos, sin, wp, kc, vct)
        oa = _mla_attention(qt, km, vt)
        ob = _chunk_attention(l, qct, kc, vct, bias)
        x = _outproj_ffn(l, x, oa, ob, wp)
    return x
```

```python
import functools

import jax
import jax.numpy as jnp
from jax import lax
from jax.experimental import pallas as pl
from jax.experimental.pallas import tpu as pltpu

F32 = jnp.float32
BF16 = jnp.bfloat16

D_MODEL = 1024
CHUNK = 64
EPS = 1e-6
NEG_INF = -1e30
LOG2_E = 1.4426950408889634

MLA_HEADS = 8
MLA_NOPE = 64
MLA_ROPE = 32
MLA_QK = MLA_NOPE + MLA_ROPE
MLA_V = 64
Q_LORA = 256
KV_LORA = 128
ROPE_BASE = 10000.0
MLA_WIDTH = MLA_HEADS * MLA_V

CA_HEADS = 8
CA_HEAD_DIM = 64
CA_WIDTH = CA_HEADS * CA_HEAD_DIM
LEFT_CHUNKS = 8
REL_CLIP = 128

OFF_CQ = 0
OFF_CKV = OFF_CQ + Q_LORA
OFF_KR = OFF_CKV + KV_LORA
OFF_CA = OFF_KR + MLA_ROPE

D_FF = 2816
CONV_W = 3

LANES = 128
HEAD_SLOT = LANES
HALF_ROPE = MLA_ROPE // 2
ROPE_PACK = LANES // MLA_ROPE
BD_WIDTH = 256
V_DIM = MLA_V
assert V_DIM == CA_HEAD_DIM and MLA_HEADS == CA_HEADS and 2 * V_DIM == LANES
MLA_SLOTS = MLA_HEADS * HEAD_SLOT

P_CQ = 0
P_CKV = P_CQ + Q_LORA
P_KR = P_CKV + KV_LORA
P_KRS = P_KR + HEAD_SLOT
P_CAQ = P_KRS + HEAD_SLOT
P_CAK = P_CAQ + CA_WIDTH
P_CAV = P_CAK + CA_WIDTH
P_COLS = P_CAV + CA_WIDTH

SUM_ROWS = 16
BLK = 256
TM_IN = 1024
CA_PAD = LEFT_CHUNKS * CHUNK
KV_PAD = max(CA_PAD, TM_IN)
CA_TQ = BLK
CA_WIN = CA_PAD + CA_TQ
BASE_W = CA_WIN + CA_TQ
CA_VARIANTS = CA_PAD // CA_TQ + 1
CA_LOOKAHEAD = 3
CA_PV_DELAY = 3
CA_QBLOCKS = 4
MLA_TQ = BLK
MLA_HPS = 8
MLA_LOOKAHEAD = 4
MLA_PV_DELAY = 1
TM_FFN = 512
FF_CHUNK = 256
N_FF_CHUNKS = D_FF // FF_CHUNK
CARRY_ROWS = 8
FFN_SLOTS = 4

V7X_VMEM_BYTES = 64 * 1024 * 1024
VMEM_LIMIT = V7X_VMEM_BYTES * 7 // 8


def _rms(x, g):
    return x * lax.rsqrt(jnp.mean(x * x, axis=-1, keepdims=True) + EPS) * g


def _dot(a, b):
    return jnp.dot(a, b, preferred_element_type=F32)


def _rope_kernel(pos_ref, inv_ref, c_ref, s_ref):
    lane = lax.broadcasted_iota(jnp.int32, (1, LANES), 1)
    pos = pos_ref[:, 0:1]
    for m in range(1, ROPE_PACK):
        pos = jnp.where(lane // MLA_ROPE == m, pos_ref[:, m:m + 1], pos)
    ang = pos * inv_ref[...]
    c = jnp.cos(ang)
    s = jnp.sin(ang)
    s = jnp.where(lane % MLA_ROPE < HALF_ROPE, -s, s)
    rotary = (lane >= MLA_NOPE) & (lane < MLA_QK)
    for m in range(ROPE_PACK):
        shift = (MLA_NOPE - m * MLA_ROPE) % LANES
        c_ref[m] = jnp.where(rotary, pltpu.roll(c, shift, 1) if shift else c, 1.0)
        s_ref[m] = jnp.where(rotary, pltpu.roll(s, shift, 1) if shift else s, 0.0)


def _rope_tables(positions):
    t = positions.size
    quarter = t // ROPE_PACK
    rows = 256
    assert t % (ROPE_PACK * rows) == 0
    pos = positions.astype(F32).reshape(ROPE_PACK, quarter).T
    inv = ROPE_BASE ** (-jnp.arange(0, MLA_ROPE, 2, dtype=F32) / MLA_ROPE)
    inv = jnp.tile(jnp.concatenate([inv, inv]), ROPE_PACK)[None, :]
    out = jax.ShapeDtypeStruct((ROPE_PACK, quarter, LANES), F32)
    cos, sin = pl.pallas_call(
        _rope_kernel,
        grid=(quarter // rows,),
        in_specs=[pl.BlockSpec((rows, ROPE_PACK), lambda i: (i, 0)),
                  pl.BlockSpec((1, LANES), lambda i: (0, 0))],
        out_specs=[pl.BlockSpec((ROPE_PACK, rows, LANES), lambda i: (0, i, 0))] * 2,
        out_shape=[out] * 2,
        compiler_params=pltpu.CompilerParams(dimension_semantics=("arbitrary",)),
    )(pos, inv)
    shape = positions.shape + (LANES,)
    return cos.reshape(shape), sin.reshape(shape)


def _bias_kernel(base_ref, out_ref):
    x = pltpu.roll(jnp.broadcast_to(base_ref[...], (CA_TQ, BASE_W)), 0, 1,
                   stride=1, stride_axis=0)
    t = x[:, CA_TQ:] * LOG2_E
    i = lax.broadcasted_iota(jnp.int32, (CA_TQ, CA_WIN), 0) // CHUNK
    j = lax.broadcasted_iota(jnp.int32, (CA_TQ, CA_WIN), 1) // CHUNK
    band = (j >= i) & (j <= i + LEFT_CHUNKS)
    tile = jnp.where(band, t, NEG_INF).T
    key = lax.broadcasted_iota(jnp.int32, (CA_WIN, CA_TQ), 0)
    for v in range(CA_VARIANTS):
        out_ref[v] = jnp.where(key >= CA_PAD - v * CA_TQ, tile, NEG_INF)


def _bias_tiles(base, depth):
    return pl.pallas_call(
        _bias_kernel,
        grid=(depth * CA_HEADS,),
        in_specs=[pl.BlockSpec((None, 1, BASE_W), lambda i: (i, 0, 0))],
        out_specs=pl.BlockSpec((None, CA_VARIANTS, None, CA_WIN, CA_TQ),
                               lambda i: (i // CA_HEADS, 0, i % CA_HEADS, 0, 0)),
        out_shape=jax.ShapeDtypeStruct((depth, CA_VARIANTS, CA_HEADS, CA_WIN, CA_TQ), F32),
        compiler_params=pltpu.CompilerParams(dimension_semantics=("arbitrary",)),
    )(base)


def _inproj_kernel(x_ref, c_ref, s_ref, gmix_ref, w1_ref, gqa_ref, gkva_ref, wuq_ref, wuk_ref,
                   wuv_ref, gq_ref, gqs_ref, gk_ref, gks_ref, gcaq_ref, gcak_ref, bd_ref,
                   kpad_ref, vpad_ref,
                   qt_ref, km_ref, vt_ref, qct_ref, kc_ref, vct_ref):
    del kpad_ref, vpad_ref
    n_sub = TM_IN // BLK

    def project(r):
        rows = slice(r * BLK, (r + 1) * BLK)
        h = _rms(x_ref[rows, :], gmix_ref[...]).astype(BF16)
        return _dot(h, w1_ref[...])

    def head_sums(t):
        sq = (t * t).astype(BF16)
        return jnp.concatenate([_dot(sq[:, c:c + BD_WIDTH], bd_ref[...])
                                for c in range(0, CA_WIDTH, BD_WIDTH)], axis=1)

    def up_project(proj):
        cqn = _rms(proj[:, P_CQ:P_CKV], gqa_ref[...]).astype(BF16)
        ckvn = _rms(proj[:, P_CKV:P_KR], gkva_ref[...]).astype(BF16)
        caq = proj[:, P_CAQ:P_CAK]
        cak = proj[:, P_CAK:P_CAV]
        return {
            "q2": _dot(cqn, wuq_ref[...]),
            "kn": _dot(ckvn, wuk_ref[...]),
            "v": _dot(ckvn, wuv_ref[...]),
            "ssq": head_sums(caq),
            "ssk": head_sums(cak),
        }

    def transpose_heads(t, ref, r):
        for p in range(t.shape[1] // LANES):
            tt = t[:, p * LANES:(p + 1) * LANES].T.astype(BF16)
            ref[2 * p, r] = tt[:V_DIM]
            ref[2 * p + 1, r] = tt[V_DIM:]

    def finish(r, proj, up):
        rows = slice(r * BLK, (r + 1) * BLK)
        cos = c_ref[rows, :]
        sin = s_ref[rows, :]
        gq_c, gq_s = gq_ref[...] * cos, gqs_ref[...] * sin
        gk_c, gk_s = gk_ref[...] * cos, gks_ref[...] * sin
        kr = proj[:, P_KR:P_KRS]
        kr_s = proj[:, P_KRS:P_CAQ] * gk_s
        for hd in range(MLA_HEADS):
            sl = slice(hd * HEAD_SLOT, (hd + 1) * HEAD_SLOT)
            ssl = slice(MLA_SLOTS + hd * HEAD_SLOT, MLA_SLOTS + (hd + 1) * HEAD_SLOT)
            q = up["q2"][:, sl]
            rq = lax.rsqrt(jnp.sum(q * q, axis=-1, keepdims=True) * (1.0 / MLA_QK) + EPS)
            qr = (q * gq_c + up["q2"][:, ssl] * gq_s) * rq
            qt_ref[hd, :, rows] = qr.T.astype(BF16)
            k = up["kn"][:, sl] + kr
            rk = lax.rsqrt(jnp.sum(k * k, axis=-1, keepdims=True) * (1.0 / MLA_QK) + EPS)
            km_ref[rows, sl] = ((k * gk_c + kr_s) * rk).astype(BF16)
        transpose_heads(up["v"], vt_ref, r)
        caq = proj[:, P_CAQ:P_CAK]
        cak = proj[:, P_CAK:P_CAV]
        qcn = caq * lax.rsqrt(up["ssq"] * (1.0 / CA_HEAD_DIM) + EPS) * gcaq_ref[...]
        for p in range(CA_WIDTH // LANES):
            qct_ref[p, :, rows] = qcn[:, p * LANES:(p + 1) * LANES].T.astype(BF16)
        kc_ref[rows, :] = (cak * lax.rsqrt(up["ssk"] * (1.0 / CA_HEAD_DIM) + EPS)
                           * gcak_ref[...]).astype(BF16)
        transpose_heads(proj[:, P_CAV:P_COLS], vct_ref, r)

    proj = project(0)
    for r in range(n_sub):
        nxt = project(r + 1) if r + 1 < n_sub else None
        finish(r, proj, up_project(proj))
        proj = nxt


def _inproj(l, x, cos, sin, wp, kpad, vpad):
    b, s, _ = x.shape
    n_sub = TM_IN // BLK
    pad_tiles = KV_PAD // TM_IN

    def tok(width):
        return pl.BlockSpec((None, TM_IN, width), lambda bi, i: (bi, i, 0))

    def lw(shape):
        return pl.BlockSpec((None,) + shape, lambda bi, i: (l,) + (0,) * len(shape),
                            pipeline_mode=pl.Buffered(1))

    const = pl.BlockSpec((BD_WIDTH, BD_WIDTH), lambda bi, i: (0, 0),
                         pipeline_mode=pl.Buffered(1))
    any_spec = pl.BlockSpec(memory_space=pl.ANY)
    heads_t = lambda off: pl.BlockSpec((None, MLA_HEADS, n_sub, V_DIM, BLK),
                                       lambda bi, i: (bi, 0, i + off, 0, 0))
    tok_shape = lambda w: jax.ShapeDtypeStruct((b, s, w), BF16)
    return pl.pallas_call(
        _inproj_kernel,
        grid=(b, s // TM_IN),
        in_specs=[tok(D_MODEL), tok(LANES), tok(LANES),
                  lw((1, D_MODEL)), lw((D_MODEL, P_COLS)), lw((1, Q_LORA)), lw((1, KV_LORA)),
                  lw((Q_LORA, 2 * MLA_SLOTS)), lw((KV_LORA, MLA_SLOTS)),
                  lw((KV_LORA, MLA_WIDTH)), lw((1, HEAD_SLOT)), lw((1, HEAD_SLOT)),
                  lw((1, HEAD_SLOT)), lw((1, HEAD_SLOT)),
                  lw((1, CA_WIDTH)), lw((1, CA_WIDTH)), const, any_spec, any_spec],
        out_specs=[pl.BlockSpec((None, MLA_HEADS, HEAD_SLOT, TM_IN), lambda bi, i: (bi, 0, 0, i)),
                   tok(MLA_SLOTS), heads_t(0),
                   pl.BlockSpec((None, CA_WIDTH // LANES, LANES, TM_IN),
                                lambda bi, i: (bi, 0, 0, i)),
                   pl.BlockSpec((None, TM_IN, CA_WIDTH), lambda bi, i: (bi, i + pad_tiles, 0)),
                   heads_t(pad_tiles)],
        out_shape=[jax.ShapeDtypeStruct((b, MLA_HEADS, HEAD_SLOT, s), BF16), tok_shape(MLA_SLOTS),
                   jax.ShapeDtypeStruct((b, MLA_HEADS, s // BLK, MLA_V, BLK), BF16),
                   jax.ShapeDtypeStruct((b, CA_WIDTH // LANES, LANES, s), BF16),
                   jax.ShapeDtypeStruct(kpad.shape, kpad.dtype),
                   jax.ShapeDtypeStruct(vpad.shape, vpad.dtype)],
        input_output_aliases={17: 4, 18: 5},
        compiler_params=pltpu.CompilerParams(
            dimension_semantics=("arbitrary", "arbitrary"), vmem_limit_bytes=VMEM_LIMIT),
    )(x, cos, sin, wp["g_mix"], wp["w1"], wp["g_qa"], wp["g_kva"], wp["w_uq"], wp["w_uk"],
      wp["w_uv"], wp["g_q"], wp["g_qs"], wp["g_k"], wp["g_ks"], wp["g_caq"], wp["g_cak"],
      wp["bd"], kpad, vpad)


def _mla_kernel(qt_ref, k_ref, vt_ref, o_ref, m_ref, acc_ref, s_ref):
    i = pl.program_id(2)
    t = MLA_TQ
    ones = jnp.ones((SUM_ROWS, t), BF16)
    m_ref[...] = jnp.full(m_ref.shape, NEG_INF, F32)
    acc_ref[...] = jnp.zeros(acc_ref.shape, F32)

    def scores(j, hh):
        rows = pl.ds(pl.multiple_of(j * t, t), t)
        sl = slice(hh * HEAD_SLOT, (hh + 1) * HEAD_SLOT)
        return _dot(k_ref[rows, sl], qt_ref[hh])

    def step(j, last):
        def accumulate(hh, p, alpha):
            v1 = jnp.concatenate([vt_ref[hh, j], ones], axis=0)
            acc_ref[hh] = acc_ref[hh] * alpha + _dot(v1, p)

        pending = {hh: s_ref[hh] for hh in range(MLA_LOOKAHEAD)}
        done = {}
        for hh in range(MLA_HPS):
            st = pending.pop(hh)
            ahead = hh + MLA_LOOKAHEAD
            if ahead < MLA_HPS:
                pending[ahead] = scores(j, ahead)
            elif not last:
                s_ref[ahead - MLA_HPS] = scores(j + 1, ahead - MLA_HPS)
            if last:
                kc = lax.broadcasted_iota(jnp.int32, st.shape, 0) // CHUNK
                qc = lax.broadcasted_iota(jnp.int32, st.shape, 1) // CHUNK
                st = jnp.where(kc <= qc, st, NEG_INF)
            m_prev = m_ref[hh]
            m_new = jnp.maximum(m_prev, jnp.max(st, axis=0, keepdims=True))
            m_ref[hh] = m_new
            done[hh] = (jnp.exp2(st - m_new).astype(BF16), jnp.exp2(m_prev - m_new))
            if hh >= MLA_PV_DELAY:
                accumulate(hh - MLA_PV_DELAY, *done.pop(hh - MLA_PV_DELAY))
        for hh in sorted(done):
            accumulate(hh, *done[hh])

    for hh in range(MLA_LOOKAHEAD):
        s_ref[hh] = scores(0, hh)

    def body(k, carry):
        for n in range(3):
            step(3 * k + n, False)
        return carry

    lax.fori_loop(0, i // 3, body, 0)

    def single(j, carry):
        step(j, False)
        return carry

    lax.fori_loop((i // 3) * 3, i, single, 0)
    step(i, True)
    for pr in range(MLA_HPS // 2):
        ot = jnp.concatenate([acc_ref[hh, :MLA_V, :] / acc_ref[hh, MLA_V:MLA_V + 1, :]
                              for hh in (2 * pr, 2 * pr + 1)], axis=0)
        o_ref[:, pr * LANES:(pr + 1) * LANES] = ot.T.astype(BF16)


def _mla_attention(qt, km, vt):
    b, s, _ = km.shape
    groups = MLA_HEADS // MLA_HPS
    nk = s // MLA_TQ
    return pl.pallas_call(
        _mla_kernel,
        grid=(b, groups, s // MLA_TQ),
        in_specs=[pl.BlockSpec((None, MLA_HPS, HEAD_SLOT, MLA_TQ), lambda bi, g, i: (bi, g, 0, i)),
                  pl.BlockSpec((None, s, MLA_HPS * HEAD_SLOT), lambda bi, g, i: (bi, 0, g)),
                  pl.BlockSpec((None, MLA_HPS, nk, MLA_V, MLA_TQ),
                               lambda bi, g, i: (bi, g, 0, 0, 0))],
        out_specs=pl.BlockSpec((None, MLA_TQ, MLA_HPS * MLA_V), lambda bi, g, i: (bi, i, g)),
        out_shape=jax.ShapeDtypeStruct((b, s, MLA_WIDTH), BF16),
        scratch_shapes=[pltpu.VMEM((MLA_HPS, 1, MLA_TQ), F32),
                        pltpu.VMEM((MLA_HPS, MLA_V + SUM_ROWS, MLA_TQ), F32),
                        pltpu.VMEM((MLA_LOOKAHEAD, MLA_TQ, MLA_TQ), F32)],
        compiler_params=pltpu.CompilerParams(
            dimension_semantics=("arbitrary", "arbitrary", "arbitrary"),
            vmem_limit_bytes=VMEM_LIMIT),
    )(qt, km, vt)


def _ca_kernel(qt_ref, k_ref, vt_ref, bias_ref, o_ref):
    lead = KV_PAD - CA_PAD
    first = lax.broadcasted_iota(jnp.int32, (LANES, 1), 0) < CA_HEAD_DIM
    ones = jnp.ones((SUM_ROWS, BLK), BF16)
    items = [(sub, h) for sub in range(CA_QBLOCKS) for h in range(CA_HEADS)]
    pieces = range(CA_WIN // BLK)

    def block(sub):
        return pl.program_id(1) * CA_QBLOCKS + sub

    def scores(sub, h):
        sl = slice((h // 2) * LANES, (h // 2 + 1) * LANES)
        qt = qt_ref[h // 2, :, sub * CA_TQ:(sub + 1) * CA_TQ]
        qh = jnp.where(first if h % 2 == 0 else ~first, qt, jnp.zeros_like(qt))
        row0 = lead + block(sub) * CA_TQ
        return [_dot(k_ref[pl.ds(pl.multiple_of(row0 + kb * BLK, BLK), BLK), sl], qh)
                for kb in pieces]

    def attend(sub, h, probs):
        blk0 = lead // BLK + block(sub)
        acc = None
        for kb in pieces:
            v1 = jnp.concatenate([vt_ref[h, blk0 + kb], ones], axis=0)
            part = _dot(v1, probs[kb])
            acc = part if acc is None else acc + part
        return acc[:CA_HEAD_DIM] / acc[CA_HEAD_DIM:CA_HEAD_DIM + 1]

    pending = {it: scores(*it) for it in items[:CA_LOOKAHEAD]}
    probs = {}
    outs = {}

    def flush(sub, h):
        outs[h] = attend(sub, h, probs.pop((sub, h)))
        if h % 2 == 1:
            ot = jnp.concatenate([outs.pop(h - 1), outs.pop(h)], axis=0)
            o_ref[sub * CA_TQ:(sub + 1) * CA_TQ, (h // 2) * LANES:(h // 2 + 1) * LANES] = (
                ot.T.astype(BF16))

    for n, (sub, h) in enumerate(items):
        st = pending.pop((sub, h))
        if n + CA_LOOKAHEAD < len(items):
            ahead = items[n + CA_LOOKAHEAD]
            pending[ahead] = scores(*ahead)
        variant = jnp.minimum(block(sub), CA_VARIANTS - 1)
        st = [st[kb] + bias_ref[variant, h, kb * BLK:(kb + 1) * BLK, :] for kb in pieces]
        m = functools.reduce(jnp.maximum, [jnp.max(t, axis=0, keepdims=True) for t in st])
        probs[(sub, h)] = [jnp.exp2(t - m).astype(BF16) for t in st]
        if n >= CA_PV_DELAY:
            flush(*items[n - CA_PV_DELAY])
    for it in items[len(items) - CA_PV_DELAY:]:
        flush(*it)


def _chunk_attention(l, qct, kc, vct, bias):
    b, _, _, s = qct.shape
    nblk = vct.shape[2]
    tq = CA_QBLOCKS * CA_TQ
    return pl.pallas_call(
        _ca_kernel,
        grid=(b, s // tq),
        in_specs=[pl.BlockSpec((None, CA_WIDTH // LANES, LANES, tq), lambda bi, i: (bi, 0, 0, i)),
                  pl.BlockSpec((None, s + KV_PAD, CA_WIDTH), lambda bi, i: (bi, 0, 0)),
                  pl.BlockSpec((None, CA_HEADS, nblk, CA_HEAD_DIM, BLK),
                               lambda bi, i: (bi, 0, 0, 0, 0)),
                  pl.BlockSpec((None, CA_VARIANTS, CA_HEADS, CA_WIN, CA_TQ),
                               lambda bi, i: (l, 0, 0, 0, 0), pipeline_mode=pl.Buffered(1))],
        out_specs=pl.BlockSpec((None, tq, CA_WIDTH), lambda bi, i: (bi, i, 0)),
        out_shape=jax.ShapeDtypeStruct((b, s, CA_WIDTH), BF16),
        compiler_params=pltpu.CompilerParams(
            dimension_semantics=("arbitrary", "arbitrary"), vmem_limit_bytes=VMEM_LIMIT),
    )(qct, kc, vct, bias)


def _ffn_kernel(x_ref, oa_ref, ob_ref, goa_ref, gob_ref, wout_ref, gffn_ref, wup_ref, cw_ref,
                cb_ref, wdown_ref, out_ref, carry_ref, u_ref, h_ref):
    tm = TM_FFN

    @pl.when(pl.program_id(1) == 0)
    def _():
        carry_ref[...] = jnp.zeros(carry_ref.shape, F32)

    na = _rms(oa_ref[...].astype(F32), goa_ref[...]).astype(BF16)
    nb = _rms(ob_ref[...].astype(F32), gob_ref[...]).astype(BF16)
    x1 = (x_ref[...] + _dot(na, wout_ref[0:MLA_WIDTH, :])
          + _dot(nb, wout_ref[MLA_WIDTH:MLA_WIDTH + CA_WIDTH, :]))
    out_ref[...] = x1
    h_ref[...] = _rms(x1, gffn_ref[...]).astype(BF16)

    def up(c, slot):
        u = _dot(h_ref[...], wup_ref[c])
        u_ref[slot, 0:CARRY_ROWS, :] = carry_ref[c]
        u_ref[slot, CARRY_ROWS:CARRY_ROWS + tm, :] = u
        carry_ref[c] = u[tm - CARRY_ROWS:, :]

    def act(c, slot):
        cw = cw_ref[c]
        y = cb_ref[c]
        for tap in range(CONV_W):
            off = CARRY_ROWS - (CONV_W - 1) + tap
            y = y + u_ref[slot, pl.ds(off, tm), :] * cw[tap:tap + 1, :]
        g = y[:, :FF_CHUNK]
        return (g * (1.0 / (1.0 + jnp.exp(-g))) * y[:, FF_CHUNK:]).astype(BF16)

    def down(c, a):
        out_ref[...] += _dot(a, wdown_ref[c])

    up(0, 0)

    def pair(c, d):
        up(c + 1, (d + 1) % FFN_SLOTS)
        a0 = act(c, d)
        up(c + 2, (d + 2) % FFN_SLOTS)
        down(c, a0)
        a1 = act(c + 1, (d + 1) % FFN_SLOTS)
        down(c + 1, a1)

    def body(k, carry):
        c = FFN_SLOTS * k
        for d in range(0, FFN_SLOTS, 2):
            pair(c + d, d)
        return carry

    n_pairs = (N_FF_CHUNKS - 1) // 2
    n_loop = n_pairs // (FFN_SLOTS // 2)
    lax.fori_loop(0, n_loop, body, 0)
    for c in range(n_loop * FFN_SLOTS, N_FF_CHUNKS - 1, 2):
        pair(c, c % FFN_SLOTS)
    last = N_FF_CHUNKS - 1
    down(last, act(last, last % FFN_SLOTS))


def _outproj_ffn(l, x, oa, ob, wp):
    b, s, _ = x.shape

    def tok(width):
        return pl.BlockSpec((None, TM_FFN, width), lambda bi, i: (bi, i, 0))

    def lw(shape):
        return pl.BlockSpec((None,) + shape, lambda bi, i: (l,) + (0,) * len(shape),
                            pipeline_mode=pl.Buffered(1))

    return pl.pallas_call(
        _ffn_kernel,
        grid=(b, s // TM_FFN),
        in_specs=[tok(D_MODEL), tok(MLA_WIDTH), tok(CA_WIDTH),
                  lw((1, MLA_WIDTH)), lw((1, CA_WIDTH)), lw((D_MODEL, D_MODEL)),
                  lw((1, D_MODEL)), lw((N_FF_CHUNKS, D_MODEL, 2 * FF_CHUNK)),
                  lw((N_FF_CHUNKS, CARRY_ROWS, 2 * FF_CHUNK)),
                  lw((N_FF_CHUNKS, 1, 2 * FF_CHUNK)),
                  lw((N_FF_CHUNKS, FF_CHUNK, D_MODEL))],
        out_specs=tok(D_MODEL),
        out_shape=jax.ShapeDtypeStruct((b, s, D_MODEL), F32),
        scratch_shapes=[pltpu.VMEM((N_FF_CHUNKS, CARRY_ROWS, 2 * FF_CHUNK), F32),
                        pltpu.VMEM((FFN_SLOTS, CARRY_ROWS + TM_FFN, 2 * FF_CHUNK), F32),
                        pltpu.VMEM((TM_FFN, D_MODEL), BF16)],
        compiler_params=pltpu.CompilerParams(
            dimension_semantics=("arbitrary", "arbitrary"), vmem_limit_bytes=VMEM_LIMIT),
    )(x, oa, ob, wp["g_oa"], wp["g_ob"], wp["w_out"], wp["g_ffn"], wp["w_up"], wp["conv_w"],
      wp["conv_b"], wp["w_down"])


def _rope_slot(first, second, axis_pad):
    t = jnp.concatenate([first, second], axis=-1)
    return jnp.pad(t, axis_pad + ((MLA_NOPE, HEAD_SLOT - MLA_QK),))


def _prepare(g_mix, w_in, w_uq, w_ukv, g_q_lora, g_kv_lora, g_mla_q, g_mla_k, g_ca_q, g_ca_k,
             rel_bias, g_out_mla, g_out_ca, w_out, g_ffn, w_up, conv_w, conv_b, w_down):
    depth = w_in.shape[0]
    row = lambda g: g[:, None, :]
    zpad = HEAD_SLOT - MLA_QK
    r1 = slice(MLA_NOPE, MLA_NOPE + HALF_ROPE)
    r2 = slice(MLA_NOPE + HALF_ROPE, MLA_QK)

    w_kr = w_in[:, :, OFF_KR:OFF_CA]
    kr_slot = _rope_slot(w_kr[..., :HALF_ROPE], w_kr[..., HALF_ROPE:], ((0, 0), (0, 0)))
    kr_swap = _rope_slot(w_kr[..., HALF_ROPE:], w_kr[..., :HALF_ROPE], ((0, 0), (0, 0)))
    w1 = jnp.concatenate([w_in[:, :, OFF_CQ:OFF_KR], kr_slot, kr_swap, w_in[:, :, OFF_CA:]],
                         axis=-1)

    wq = w_uq.reshape(depth, Q_LORA, MLA_HEADS, MLA_QK)
    wq_slot = jnp.pad(wq, ((0, 0), (0, 0), (0, 0), (0, zpad))).reshape(depth, Q_LORA, -1)
    wq_swap = _rope_slot(wq[..., r2], wq[..., r1], ((0, 0), (0, 0), (0, 0)))
    wq2 = jnp.concatenate([wq_slot, wq_swap.reshape(depth, Q_LORA, -1)], axis=-1)
    wkv = w_ukv.reshape(depth, KV_LORA, MLA_HEADS, MLA_NOPE + MLA_V)
    wk = jnp.pad(wkv[..., :MLA_NOPE], ((0, 0), (0, 0), (0, 0), (0, HEAD_SLOT - MLA_NOPE)))
    wk = wk.reshape(depth, KV_LORA, -1)
    wv = wkv[..., MLA_NOPE:].reshape(depth, KV_LORA, MLA_WIDTH)

    gq = g_mla_q * (MLA_QK ** -0.5 * LOG2_E)
    g_q = jnp.pad(gq, ((0, 0), (0, zpad)))
    g_qs = _rope_slot(gq[:, r2], gq[:, r1], ((0, 0),))
    g_k = jnp.pad(g_mla_k, ((0, 0), (0, zpad)))
    g_ks = _rope_slot(g_mla_k[:, r2], g_mla_k[:, r1], ((0, 0),))
    g_caq = jnp.tile(g_ca_q * (CA_HEAD_DIM ** -0.5 * LOG2_E), (1, CA_HEADS))
    g_cak = jnp.tile(g_ca_k, (1, CA_HEADS))
    head = jnp.arange(BD_WIDTH) // CA_HEAD_DIM
    bd = (head[:, None] == head[None, :]).astype(BF16)

    ext = jnp.pad(rel_bias, ((0, 0), (0, 0), (CA_TQ - 1 - REL_CLIP, CA_WIN - REL_CLIP)),
                  mode="edge")
    base = ext[:, :, ::-1].reshape(depth * CA_HEADS, 1, BASE_W)

    def chunks(w):
        g = w[..., :D_FF].reshape(w.shape[:-1] + (N_FF_CHUNKS, FF_CHUNK))
        v = w[..., D_FF:].reshape(w.shape[:-1] + (N_FF_CHUNKS, FF_CHUNK))
        return jnp.concatenate([g, v], axis=-1)

    w_up_c = jnp.moveaxis(chunks(w_up), 2, 1)
    conv_c = jnp.moveaxis(chunks(conv_w), 2, 1)
    conv_c = jnp.pad(conv_c, ((0, 0), (0, 0), (0, CARRY_ROWS - CONV_W), (0, 0)))
    cb_c = chunks(conv_b)[:, :, None, :]
    return {
        "g_mix": row(g_mix), "w1": w1.astype(BF16), "g_qa": row(g_q_lora),
        "g_kva": row(g_kv_lora), "w_uq": wq2.astype(BF16), "w_uk": wk.astype(BF16),
        "w_uv": wv.astype(BF16), "g_q": row(g_q), "g_qs": row(g_qs), "g_k": row(g_k),
        "g_ks": row(g_ks), "g_caq": row(g_caq), "g_cak": row(g_cak), "bd": bd, "base": base,
        "g_oa": row(g_out_mla), "g_ob": row(g_out_ca), "w_out": w_out.astype(BF16),
        "g_ffn": row(g_ffn), "w_up": w_up_c.astype(BF16), "conv_w": conv_c, "conv_b": cb_c,
        "w_down": w_down.reshape(depth, N_FF_CHUNKS, FF_CHUNK, D_MODEL).astype(BF16),
    }


def kernel(x, positions, g_mix, w_in, w_uq, w_ukv, g_q_lora, g_kv_lora, g_mla_q, g_mla_k,
           g_ca_q, g_ca_k, rel_bias, g_out_mla, g_out_ca, w_out, g_ffn, w_up, conv_w, conv_b,
           w_down):
    b, s, _ = x.shape
    depth = w_in.shape[0]
    assert s % TM_IN == 0 and s % TM_FFN == 0 and KV_PAD % TM_IN == 0
    assert (KV_PAD - CA_PAD) % BLK == 0
    wp = _prepare(g_mix, w_in, w_uq, w_ukv, g_q_lora, g_kv_lora, g_mla_q, g_mla_k, g_ca_q,
                  g_ca_k, rel_bias, g_out_mla, g_out_ca, w_out, g_ffn, w_up, conv_w, conv_b,
                  w_down)

    cos, sin = _rope_tables(positions)
    bias = _bias_tiles(wp["base"], depth)

    kc = jnp.zeros((b, s + KV_PAD, CA_WIDTH), BF16)
    vct = jnp.zeros((b, CA_HEADS, (s + KV_PAD) // BLK, CA_HEAD_DIM, BLK), BF16)
    for l in range(depth):
        qt, km, vt, qct, kc, vct = _inproj(l, x, cos, sin, wp, kc, vct)
        oa = _mla_attention(qt, km, vt)
        ob = _chunk_attention(l, qct, kc, vct, bias)
        x = _outproj_ffn(l, x, oa, ob, wp)
    return x
```

```python
import functools

import jax
import jax.numpy as jnp
from jax import lax
from jax.experimental import pallas as pl
from jax.experimental.pallas import tpu as pltpu

F32 = jnp.float32
BF16 = jnp.bfloat16

D_MODEL = 1024
CHUNK = 64
EPS = 1e-6
NEG_INF = -1e30
LOG2_E = 1.4426950408889634

MLA_HEADS = 8
MLA_NOPE = 64
MLA_ROPE = 32
MLA_QK = MLA_NOPE + MLA_ROPE
MLA_V = 64
Q_LORA = 256
KV_LORA = 128
ROPE_BASE = 10000.0
MLA_WIDTH = MLA_HEADS * MLA_V

CA_HEADS = 8
CA_HEAD_DIM = 64
CA_WIDTH = CA_HEADS * CA_HEAD_DIM
LEFT_CHUNKS = 8
REL_CLIP = 128

OFF_CQ = 0
OFF_CKV = OFF_CQ + Q_LORA
OFF_KR = OFF_CKV + KV_LORA
OFF_CA = OFF_KR + MLA_ROPE

D_FF = 2816
CONV_W = 3

LANES = 128
HEAD_SLOT = LANES
HALF_ROPE = MLA_ROPE // 2
ROPE_PACK = LANES // MLA_ROPE
BD_WIDTH = 256
V_DIM = MLA_V
assert V_DIM == CA_HEAD_DIM and MLA_HEADS == CA_HEADS and 2 * V_DIM == LANES
MLA_SLOTS = MLA_HEADS * HEAD_SLOT

P_CQ = 0
P_CKV = P_CQ + Q_LORA
P_KR = P_CKV + KV_LORA
P_KRS = P_KR + HEAD_SLOT
P_CAQ = P_KRS + HEAD_SLOT
P_CAK = P_CAQ + CA_WIDTH
P_CAV = P_CAK + CA_WIDTH
P_COLS = P_CAV + CA_WIDTH

SUM_ROWS = 16
BLK = 256
TM_IN = 1024
CA_PAD = LEFT_CHUNKS * CHUNK
KV_PAD = max(CA_PAD, TM_IN)
CA_TQ = BLK
CA_WIN = CA_PAD + CA_TQ
BASE_W = CA_WIN + CA_TQ
CA_VARIANTS = CA_PAD // CA_TQ + 1
CA_LOOKAHEAD = 3
CA_PV_DELAY = 3
CA_QBLOCKS = 4
MLA_TQ = BLK
MLA_HPS = 8
MLA_LOOKAHEAD = 4
MLA_PV_DELAY = 1
TM_FFN = 512
FF_CHUNK = 256
N_FF_CHUNKS = D_FF // FF_CHUNK
CARRY_ROWS = 8
FFN_SLOTS = 4

V7X_VMEM_BYTES = 64 * 1024 * 1024
VMEM_LIMIT = V7X_VMEM_BYTES * 7 // 8


def _rms(x, g):
    return x * lax.rsqrt(jnp.mean(x * x, axis=-1, keepdims=True) + EPS) * g


def _dot(a, b):
    return jnp.dot(a, b, preferred_element_type=F32)


def _rope_kernel(pos_ref, inv_ref, c_ref, s_ref):
    lane = lax.broadcasted_iota(jnp.int32, (1, LANES), 1)
    pos = pos_ref[:, 0:1]
    for m in range(1, ROPE_PACK):
        pos = jnp.where(lane // MLA_ROPE == m, pos_ref[:, m:m + 1], pos)
    ang = pos * inv_ref[...]
    c = jnp.cos(ang)
    s = jnp.sin(ang)
    s = jnp.where(lane % MLA_ROPE < HALF_ROPE, -s, s)
    rotary = (lane >= MLA_NOPE) & (lane < MLA_QK)
    for m in range(ROPE_PACK):
        shift = (MLA_NOPE - m * MLA_ROPE) % LANES
        c_ref[m] = jnp.where(rotary, pltpu.roll(c, shift, 1) if shift else c, 1.0)
        s_ref[m] = jnp.where(rotary, pltpu.roll(s, shift, 1) if shift else s, 0.0)


def _rope_tables(positions):
    t = positions.size
    quarter = t // ROPE_PACK
    rows = 256
    assert t % (ROPE_PACK * rows) == 0
    pos = positions.astype(F32).reshape(ROPE_PACK, quarter).T
    inv = ROPE_BASE ** (-jnp.arange(0, MLA_ROPE, 2, dtype=F32) / MLA_ROPE)
    inv = jnp.tile(jnp.concatenate([inv, inv]), ROPE_PACK)[None, :]
    out = jax.ShapeDtypeStruct((ROPE_PACK, quarter, LANES), F32)
    cos, sin = pl.pallas_call(
        _rope_kernel,
        grid=(quarter // rows,),
        in_specs=[pl.BlockSpec((rows, ROPE_PACK), lambda i: (i, 0)),
                  pl.BlockSpec((1, LANES), lambda i: (0, 0))],
        out_specs=[pl.BlockSpec((ROPE_PACK, rows, LANES), lambda i: (0, i, 0))] * 2,
        out_shape=[out] * 2,
        compiler_params=pltpu.CompilerParams(dimension_semantics=("arbitrary",)),
    )(pos, inv)
    shape = positions.shape + (LANES,)
    return cos.reshape(shape), sin.reshape(shape)


def _bias_kernel(base_ref, out_ref):
    x = pltpu.roll(jnp.broadcast_to(base_ref[...], (CA_TQ, BASE_W)), 0, 1,
                   stride=1, stride_axis=0)
    t = x[:, CA_TQ:] * LOG2_E
    i = lax.broadcasted_iota(jnp.int32, (CA_TQ, CA_WIN), 0) // CHUNK
    j = lax.broadcasted_iota(jnp.int32, (CA_TQ, CA_WIN), 1) // CHUNK
    band = (j >= i) & (j <= i + LEFT_CHUNKS)
    tile = jnp.where(band, t, NEG_INF).T
    key = lax.broadcasted_iota(jnp.int32, (CA_WIN, CA_TQ), 0)
    for v in range(CA_VARIANTS):
        out_ref[v] = jnp.where(key >= CA_PAD - v * CA_TQ, tile, NEG_INF)


def _bias_tiles(base, depth):
    return pl.pallas_call(
        _bias_kernel,
        grid=(depth * CA_HEADS,),
        in_specs=[pl.BlockSpec((None, 1, BASE_W), lambda i: (i, 0, 0))],
        out_specs=pl.BlockSpec((None, CA_VARIANTS, None, CA_WIN, CA_TQ),
                               lambda i: (i // CA_HEADS, 0, i % CA_HEADS, 0, 0)),
        out_shape=jax.ShapeDtypeStruct((depth, CA_VARIANTS, CA_HEADS, CA_WIN, CA_TQ), F32),
        compiler_params=pltpu.CompilerParams(dimension_semantics=("arbitrary",)),
    )(base)


def _inproj_kernel(x_ref, c_ref, s_ref, gmix_ref, w1_ref, gqa_ref, gkva_ref, wuq_ref, wuk_ref,
                   wuv_ref, gq_ref, gqs_ref, gk_ref, gks_ref, gcaq_ref, gcak_ref, bd_ref,
                   kpad_ref, vpad_ref,
                   qt_ref, km_ref, vt_ref, qct_ref, kc_ref, vct_ref):
    del kpad_ref, vpad_ref
    n_sub = TM_IN // BLK

    def project(r):
        rows = slice(r * BLK, (r + 1) * BLK)
        h = _rms(x_ref[rows, :], gmix_ref[...]).astype(BF16)
        return _dot(h, w1_ref[...])

    def head_sums(t):
        sq = (t * t).astype(BF16)
        return jnp.concatenate([_dot(sq[:, c:c + BD_WIDTH], bd_ref[...])
                                for c in range(0, CA_WIDTH, BD_WIDTH)], axis=1)

    def up_project(proj):
        cqn = _rms(proj[:, P_CQ:P_CKV], gqa_ref[...]).astype(BF16)
        ckvn = _rms(proj[:, P_CKV:P_KR], gkva_ref[...]).astype(BF16)
        caq = proj[:, P_CAQ:P_CAK]
        cak = proj[:, P_CAK:P_CAV]
        return {
            "q2": _dot(cqn, wuq_ref[...]),
            "kn": _dot(ckvn, wuk_ref[...]),
            "v": _dot(ckvn, wuv_ref[...]),
            "ssq": head_sums(caq),
            "ssk": head_sums(cak),
        }

    def transpose_heads(t, ref, r):
        for p in range(t.shape[1] // LANES):
            tt = t[:, p * LANES:(p + 1) * LANES].T.astype(BF16)
            ref[2 * p, r] = tt[:V_DIM]
            ref[2 * p + 1, r] = tt[V_DIM:]

    def finish(r, proj, up):
        rows = slice(r * BLK, (r + 1) * BLK)
        cos = c_ref[rows, :]
        sin = s_ref[rows, :]
        gq_c, gq_s = gq_ref[...] * cos, gqs_ref[...] * sin
        gk_c, gk_s = gk_ref[...] * cos, gks_ref[...] * sin
        kr = proj[:, P_KR:P_KRS]
        kr_s = proj[:, P_KRS:P_CAQ] * gk_s
        for hd in range(MLA_HEADS):
            sl = slice(hd * HEAD_SLOT, (hd + 1) * HEAD_SLOT)
            ssl = slice(MLA_SLOTS + hd * HEAD_SLOT, MLA_SLOTS + (hd + 1) * HEAD_SLOT)
            q = up["q2"][:, sl]
            rq = lax.rsqrt(jnp.sum(q * q, axis=-1, keepdims=True) * (1.0 / MLA_QK) + EPS)
            qr = (q * gq_c + up["q2"][:, ssl] * gq_s) * rq
            qt_ref[hd, :, rows] = qr.T.astype(BF16)
            k = up["kn"][:, sl] + kr
            rk = lax.rsqrt(jnp.sum(k * k, axis=-1, keepdims=True) * (1.0 / MLA_QK) + EPS)
            km_ref[rows, sl] = ((k * gk_c + kr_s) * rk).astype(BF16)
        transpose_heads(up["v"], vt_ref, r)
        caq = proj[:, P_CAQ:P_CAK]
        cak = proj[:, P_CAK:P_CAV]
        qcn = caq * lax.rsqrt(up["ssq"] * (1.0 / CA_HEAD_DIM) + EPS) * gcaq_ref[...]
        for p in range(CA_WIDTH // LANES):
            qct_ref[p, :, rows] = qcn[:, p * LANES:(p + 1) * LANES].T.astype(BF16)
        kc_ref[rows, :] = (cak * lax.rsqrt(up["ssk"] * (1.0 / CA_HEAD_DIM) + EPS)
                           * gcak_ref[...]).astype(BF16)
        transpose_heads(proj[:, P_CAV:P_COLS], vct_ref, r)

    proj = project(0)
    for r in range(n_sub):
        nxt = project(r + 1) if r + 1 < n_sub else None
        finish(r, proj, up_project(proj))
        proj = nxt


def _inproj(l, x, cos, sin, wp, kpad, vpad):
    b, s, _ = x.shape
    n_sub = TM_IN // BLK
    pad_tiles = KV_PAD // TM_IN

    def tok(width):
        return pl.BlockSpec((None, TM_IN, width), lambda bi, i: (bi, i, 0))

    def lw(shape):
        return pl.BlockSpec((None,) + shape, lambda bi, i: (l,) + (0,) * len(shape),
                            pipeline_mode=pl.Buffered(1))

    const = pl.BlockSpec((BD_WIDTH, BD_WIDTH), lambda bi, i: (0, 0),
                         pipeline_mode=pl.Buffered(1))
    any_spec = pl.BlockSpec(memory_space=pl.ANY)
    heads_t = lambda off: pl.BlockSpec((None, MLA_HEADS, n_sub, V_DIM, BLK),
                                       lambda bi, i: (bi, 0, i + off, 0, 0))
    tok_shape = lambda w: jax.ShapeDtypeStruct((b, s, w), BF16)
    return pl.pallas_call(
        _inproj_kernel,
        grid=(b, s // TM_IN),
        in_specs=[tok(D_MODEL), tok(LANES), tok(LANES),
                  lw((1, D_MODEL)), lw((D_MODEL, P_COLS)), lw((1, Q_LORA)), lw((1, KV_LORA)),
                  lw((Q_LORA, 2 * MLA_SLOTS)), lw((KV_LORA, MLA_SLOTS)),
                  lw((KV_LORA, MLA_WIDTH)), lw((1, HEAD_SLOT)), lw((1, HEAD_SLOT)),
                  lw((1, HEAD_SLOT)), lw((1, HEAD_SLOT)),
                  lw((1, CA_WIDTH)), lw((1, CA_WIDTH)), const, any_spec, any_spec],
        out_specs=[pl.BlockSpec((None, MLA_HEADS, HEAD_SLOT, TM_IN), lambda bi, i: (bi, 0, 0, i)),
                   tok(MLA_SLOTS), heads_t(0),
                   pl.BlockSpec((None, CA_WIDTH // LANES, LANES, TM_IN),
                                lambda bi, i: (bi, 0, 0, i)),
                   pl.BlockSpec((None, TM_IN, CA_WIDTH), lambda bi, i: (bi, i + pad_tiles, 0)),
                   heads_t(pad_tiles)],
        out_shape=[jax.ShapeDtypeStruct((b, MLA_HEADS, HEAD_SLOT, s), BF16), tok_shape(MLA_SLOTS),
                   jax.ShapeDtypeStruct((b, MLA_HEADS, s // BLK, MLA_V, BLK), BF16),
                   jax.ShapeDtypeStruct((b, CA_WIDTH // LANES, LANES, s), BF16),
                   jax.ShapeDtypeStruct(kpad.shape, kpad.dtype),
                   jax.ShapeDtypeStruct(vpad.shape, vpad.dtype)],
        input_output_aliases={17: 4, 18: 5},
        compiler_params=pltpu.CompilerParams(
            dimension_semantics=("arbitrary", "arbitrary"), vmem_limit_bytes=VMEM_LIMIT),
    )(x, cos, sin, wp["g_mix"], wp["w1"], wp["g_qa"], wp["g_kva"], wp["w_uq"], wp["w_uk"],
      wp["w_uv"], wp["g_q"], wp["g_qs"], wp["g_k"], wp["g_ks"], wp["g_caq"], wp["g_cak"],
      wp["bd"], kpad, vpad)


def _mla_kernel(qt_ref, k_ref, vt_ref, o_ref, m_ref, acc_ref, s_ref):
    i = pl.program_id(2)
    t = MLA_TQ
    ones = jnp.ones((SUM_ROWS, t), BF16)
    m_ref[...] = jnp.full(m_ref.shape, NEG_INF, F32)
    acc_ref[...] = jnp.zeros(acc_ref.shape, F32)

    def scores(j, hh):
        rows = pl.ds(pl.multiple_of(j * t, t), t)
        sl = slice(hh * HEAD_SLOT, (hh + 1) * HEAD_SLOT)
        return _dot(k_ref[rows, sl], qt_ref[hh])

    def step(j, last):
        def accumulate(hh, p, alpha):
            v1 = jnp.concatenate([vt_ref[hh, j], ones], axis=0)
            acc_ref[hh] = acc_ref[hh] * alpha + _dot(v1, p)

        pending = {hh: s_ref[hh] for hh in range(MLA_LOOKAHEAD)}
        done = {}
        for hh in range(MLA_HPS):
            st = pending.pop(hh)
            ahead = hh + MLA_LOOKAHEAD
            if ahead < MLA_HPS:
                pending[ahead] = scores(j, ahead)
            elif not last:
                s_ref[ahead - MLA_HPS] = scores(j + 1, ahead - MLA_HPS)
            if last:
                kc = lax.broadcasted_iota(jnp.int32, st.shape, 0) // CHUNK
                qc = lax.broadcasted_iota(jnp.int32, st.shape, 1) // CHUNK
                st = jnp.where(kc <= qc, st, NEG_INF)
            m_prev = m_ref[hh]
            m_new = jnp.maximum(m_prev, jnp.max(st, axis=0, keepdims=True))
            m_ref[hh] = m_new
            done[hh] = (jnp.exp2(st - m_new).astype(BF16), jnp.exp2(m_prev - m_new))
            if hh >= MLA_PV_DELAY:
                accumulate(hh - MLA_PV_DELAY, *done.pop(hh - MLA_PV_DELAY))
        for hh in sorted(done):
            accumulate(hh, *done[hh])

    for hh in range(MLA_LOOKAHEAD):
        s_ref[hh] = scores(0, hh)

    def body(k, carry):
        for n in range(3):
            step(3 * k + n, False)
        return carry

    lax.fori_loop(0, i // 3, body, 0)

    def single(j, carry):
        step(j, False)
        return carry

    lax.fori_loop((i // 3) * 3, i, single, 0)
    step(i, True)
    for pr in range(MLA_HPS // 2):
        ot = jnp.concatenate([acc_ref[hh, :MLA_V, :] / acc_ref[hh, MLA_V:MLA_V + 1, :]
                              for hh in (2 * pr, 2 * pr + 1)], axis=0)
        o_ref[:, pr * LANES:(pr + 1) * LANES] = ot.T.astype(BF16)


def _mla_attention(qt, km, vt):
    b, s, _ = km.shape
    groups = MLA_HEADS // MLA_HPS
    nk = s // MLA_TQ
    return pl.pallas_call(
        _mla_kernel,
        grid=(b, groups, s // MLA_TQ),
        in_specs=[pl.BlockSpec((None, MLA_HPS, HEAD_SLOT, MLA_TQ), lambda bi, g, i: (bi, g, 0, i)),
                  pl.BlockSpec((None, s, MLA_HPS * HEAD_SLOT), lambda bi, g, i: (bi, 0, g)),
                  pl.BlockSpec((None, MLA_HPS, nk, MLA_V, MLA_TQ),
                               lambda bi, g, i: (bi, g, 0, 0, 0))],
        out_specs=pl.BlockSpec((None, MLA_TQ, MLA_HPS * MLA_V), lambda bi, g, i: (bi, i, g)),
        out_shape=jax.ShapeDtypeStruct((b, s, MLA_WIDTH), BF16),
        scratch_shapes=[pltpu.VMEM((MLA_HPS, 1, MLA_TQ), F32),
                        pltpu.VMEM((MLA_HPS, MLA_V + SUM_ROWS, MLA_TQ), F32),
                        pltpu.VMEM((MLA_LOOKAHEAD, MLA_TQ, MLA_TQ), F32)],
        compiler_params=pltpu.CompilerParams(
            dimension_semantics=("arbitrary", "arbitrary", "arbitrary"),
            vmem_limit_bytes=VMEM_LIMIT),
    )(qt, km, vt)


def _ca_kernel(qt_ref, k_ref, vt_ref, bias_ref, o_ref):
    lead = KV_PAD - CA_PAD
    first = lax.broadcasted_iota(jnp.int32, (LANES, 1), 0) < CA_HEAD_DIM
    ones = jnp.ones((SUM_ROWS, BLK), BF16)
    items = [(sub, h) for sub in range(CA_QBLOCKS) for h in range(CA_HEADS)]
    pieces = range(CA_WIN // BLK)

    def block(sub):
        return pl.program_id(1) * CA_QBLOCKS + sub

    def scores(sub, h):
        sl = slice((h // 2) * LANES, (h // 2 + 1) * LANES)
        qt = qt_ref[h // 2, :, sub * CA_TQ:(sub + 1) * CA_TQ]
        qh = jnp.where(first if h % 2 == 0 else ~first, qt, jnp.zeros_like(qt))
        row0 = lead + block(sub) * CA_TQ
        return [_dot(k_ref[pl.ds(pl.multiple_of(row0 + kb * BLK, BLK), BLK), sl], qh)
                for kb in pieces]

    def attend(sub, h, probs):
        blk0 = lead // BLK + block(sub)
        acc = None
        for kb in pieces:
            v1 = jnp.concatenate([vt_ref[h, blk0 + kb], ones], axis=0)
            part = _dot(v1, probs[kb])
            acc = part if acc is None else acc + part
        return acc[:CA_HEAD_DIM] / acc[CA_HEAD_DIM:CA_HEAD_DIM + 1]

    pending = {it: scores(*it) for it in items[:CA_LOOKAHEAD]}
    probs = {}
    outs = {}

    def flush(sub, h):
        outs[h] = attend(sub, h, probs.pop((sub, h)))
        if h % 2 == 1:
            ot = jnp.concatenate([outs.pop(h - 1), outs.pop(h)], axis=0)
            o_ref[sub * CA_TQ:(sub + 1) * CA_TQ, (h // 2) * LANES:(h // 2 + 1) * LANES] = (
                ot.T.astype(BF16))

    for n, (sub, h) in enumerate(items):
        st = pending.pop((sub, h))
        if n + CA_LOOKAHEAD < len(items):
            ahead = items[n + CA_LOOKAHEAD]
            pending[ahead] = scores(*ahead)
        variant = jnp.minimum(block(sub), CA_VARIANTS - 1)
        st = [st[kb] + bias_ref[variant, h, kb * BLK:(kb + 1) * BLK, :] for kb in pieces]
        m = functools.reduce(jnp.maximum, [jnp.max(t, axis=0, keepdims=True) for t in st])
        probs[(sub, h)] = [jnp.exp2(t - m).astype(BF16) for t in st]
        if n >= CA_PV_DELAY:
            flush(*items[n - CA_PV_DELAY])
    for it in items[len(items) - CA_PV_DELAY:]:
        flush(*it)


def _chunk_attention(l, qct, kc, vct, bias):
    b, _, _, s = qct.shape
    nblk = vct.shape[2]
    tq = CA_QBLOCKS * CA_TQ
    return pl.pallas_call(
        _ca_kernel,
        grid=(b, s // tq),
        in_specs=[pl.BlockSpec((None, CA_WIDTH // LANES, LANES, tq), lambda bi, i: (bi, 0, 0, i)),
                  pl.BlockSpec((None, s + KV_PAD, CA_WIDTH), lambda bi, i: (bi, 0, 0)),
                  pl.BlockSpec((None, CA_HEADS, nblk, CA_HEAD_DIM, BLK),
                               lambda bi, i: (bi, 0, 0, 0, 0)),
                  pl.BlockSpec((None, CA_VARIANTS, CA_HEADS, CA_WIN, CA_TQ),
                               lambda bi, i: (l, 0, 0, 0, 0), pipeline_mode=pl.Buffered(1))],
        out_specs=pl.BlockSpec((None, tq, CA_WIDTH), lambda bi, i: (bi, i, 0)),
        out_shape=jax.ShapeDtypeStruct((b, s, CA_WIDTH), BF16),
        compiler_params=pltpu.CompilerParams(
            dimension_semantics=("arbitrary", "arbitrary"), vmem_limit_bytes=VMEM_LIMIT),
    )(qct, kc, vct, bias)


def _ffn_kernel(layer, x_ref, oa_ref, ob_ref, goa_ref, gob_ref, wout_ref, gffn_ref, wup_hbm,
                cw_ref, cb_ref, wdown_hbm, out_ref, carry_ref, u_ref, h_ref, wup_ref, wdown_ref,
                sem):
    tm = TM_FFN
    first_step = (pl.program_id(0) == 0) & (pl.program_id(1) == 0)

    def weight_copies():
        return (pltpu.make_async_copy(wup_hbm.at[layer], wup_ref, sem.at[0]),
                pltpu.make_async_copy(wdown_hbm.at[layer], wdown_ref, sem.at[1]))

    @pl.when(first_step)
    def _():
        for copy in weight_copies():
            copy.start()

    @pl.when(pl.program_id(1) == 0)
    def _():
        carry_ref[...] = jnp.zeros(carry_ref.shape, F32)

    na = _rms(oa_ref[...].astype(F32), goa_ref[...]).astype(BF16)
    nb = _rms(ob_ref[...].astype(F32), gob_ref[...]).astype(BF16)
    x1 = (x_ref[...] + _dot(na, wout_ref[0:MLA_WIDTH, :])
          + _dot(nb, wout_ref[MLA_WIDTH:MLA_WIDTH + CA_WIDTH, :]))
    out_ref[...] = x1
    h_ref[...] = _rms(x1, gffn_ref[...]).astype(BF16)

    @pl.when(first_step)
    def _():
        for copy in weight_copies():
            copy.wait()

    def up(c, slot):
        u = _dot(h_ref[...], wup_ref[c])
        u_ref[slot, 0:CARRY_ROWS, :] = carry_ref[c]
        u_ref[slot, CARRY_ROWS:CARRY_ROWS + tm, :] = u
        carry_ref[c] = u[tm - CARRY_ROWS:, :]

    def act(c, slot):
        cw = cw_ref[c]
        y = cb_ref[c]
        for tap in range(CONV_W):
            off = CARRY_ROWS - (CONV_W - 1) + tap
            y = y + u_ref[slot, pl.ds(off, tm), :] * cw[tap:tap + 1, :]
        g = y[:, :FF_CHUNK]
        return (g * (1.0 / (1.0 + jnp.exp(-g))) * y[:, FF_CHUNK:]).astype(BF16)

    def down(c, a):
        out_ref[...] += _dot(a, wdown_ref[c])

    up(0, 0)

    def pair(c, d):
        up(c + 1, (d + 1) % FFN_SLOTS)
        a0 = act(c, d)
        up(c + 2, (d + 2) % FFN_SLOTS)
        down(c, a0)
        a1 = act(c + 1, (d + 1) % FFN_SLOTS)
        down(c + 1, a1)

    def body(k, carry):
        c = FFN_SLOTS * k
        for d in range(0, FFN_SLOTS, 2):
            pair(c + d, d)
        return carry

    n_pairs = (N_FF_CHUNKS - 1) // 2
    n_loop = n_pairs // (FFN_SLOTS // 2)
    lax.fori_loop(0, n_loop, body, 0)
    for c in range(n_loop * FFN_SLOTS, N_FF_CHUNKS - 1, 2):
        pair(c, c % FFN_SLOTS)
    last = N_FF_CHUNKS - 1
    down(last, act(last, last % FFN_SLOTS))


def _outproj_ffn(l, x, oa, ob, wp):
    b, s, _ = x.shape

    def tok(width):
        return pl.BlockSpec((None, TM_FFN, width), lambda bi, i: (bi, i, 0))

    def lw(shape):
        return pl.BlockSpec((None,) + shape, lambda bi, i: (l,) + (0,) * len(shape),
                            pipeline_mode=pl.Buffered(1))

    in_hbm = pl.BlockSpec(memory_space=pl.ANY)
    return pl.pallas_call(
        functools.partial(_ffn_kernel, l),
        grid=(b, s // TM_FFN),
        in_specs=[tok(D_MODEL), tok(MLA_WIDTH), tok(CA_WIDTH),
                  lw((1, MLA_WIDTH)), lw((1, CA_WIDTH)), lw((D_MODEL, D_MODEL)),
                  lw((1, D_MODEL)), in_hbm,
                  lw((N_FF_CHUNKS, CARRY_ROWS, 2 * FF_CHUNK)),
                  lw((N_FF_CHUNKS, 1, 2 * FF_CHUNK)),
                  in_hbm],
        out_specs=tok(D_MODEL),
        out_shape=jax.ShapeDtypeStruct((b, s, D_MODEL), F32),
        scratch_shapes=[pltpu.VMEM((N_FF_CHUNKS, CARRY_ROWS, 2 * FF_CHUNK), F32),
                        pltpu.VMEM((FFN_SLOTS, CARRY_ROWS + TM_FFN, 2 * FF_CHUNK), F32),
                        pltpu.VMEM((TM_FFN, D_MODEL), BF16),
                        pltpu.VMEM((N_FF_CHUNKS, D_MODEL, 2 * FF_CHUNK), BF16),
                        pltpu.VMEM((N_FF_CHUNKS, FF_CHUNK, D_MODEL), BF16),
                        pltpu.SemaphoreType.DMA((2,))],
        compiler_params=pltpu.CompilerParams(
            dimension_semantics=("arbitrary", "arbitrary"), vmem_limit_bytes=VMEM_LIMIT),
    )(x, oa, ob, wp["g_oa"], wp["g_ob"], wp["w_out"], wp["g_ffn"], wp["w_up"], wp["conv_w"],
      wp["conv_b"], wp["w_down"])


def _rope_slot(first, second, axis_pad):
    t = jnp.concatenate([first, second], axis=-1)
    return jnp.pad(t, axis_pad + ((MLA_NOPE, HEAD_SLOT - MLA_QK),))


def _prepare(g_mix, w_in, w_uq, w_ukv, g_q_lora, g_kv_lora, g_mla_q, g_mla_k, g_ca_q, g_ca_k,
             rel_bias, g_out_mla, g_out_ca, w_out, g_ffn, w_up, conv_w, conv_b, w_down):
    depth = w_in.shape[0]
    row = lambda g: g[:, None, :]
    zpad = HEAD_SLOT - MLA_QK
    r1 = slice(MLA_NOPE, MLA_NOPE + HALF_ROPE)
    r2 = slice(MLA_NOPE + HALF_ROPE, MLA_QK)

    w_kr = w_in[:, :, OFF_KR:OFF_CA]
    kr_slot = _rope_slot(w_kr[..., :HALF_ROPE], w_kr[..., HALF_ROPE:], ((0, 0), (0, 0)))
    kr_swap = _rope_slot(w_kr[..., HALF_ROPE:], w_kr[..., :HALF_ROPE], ((0, 0), (0, 0)))
    w1 = jnp.concatenate([w_in[:, :, OFF_CQ:OFF_KR], kr_slot, kr_swap, w_in[:, :, OFF_CA:]],
                         axis=-1)

    wq = w_uq.reshape(depth, Q_LORA, MLA_HEADS, MLA_QK)
    wq_slot = jnp.pad(wq, ((0, 0), (0, 0), (0, 0), (0, zpad))).reshape(depth, Q_LORA, -1)
    wq_swap = _rope_slot(wq[..., r2], wq[..., r1], ((0, 0), (0, 0), (0, 0)))
    wq2 = jnp.concatenate([wq_slot, wq_swap.reshape(depth, Q_LORA, -1)], axis=-1)
    wkv = w_ukv.reshape(depth, KV_LORA, MLA_HEADS, MLA_NOPE + MLA_V)
    wk = jnp.pad(wkv[..., :MLA_NOPE], ((0, 0), (0, 0), (0, 0), (0, HEAD_SLOT - MLA_NOPE)))
    wk = wk.reshape(depth, KV_LORA, -1)
    wv = wkv[..., MLA_NOPE:].reshape(depth, KV_LORA, MLA_WIDTH)

    gq = g_mla_q * (MLA_QK ** -0.5 * LOG2_E)
    g_q = jnp.pad(gq, ((0, 0), (0, zpad)))
    g_qs = _rope_slot(gq[:, r2], gq[:, r1], ((0, 0),))
    g_k = jnp.pad(g_mla_k, ((0, 0), (0, zpad)))
    g_ks = _rope_slot(g_mla_k[:, r2], g_mla_k[:, r1], ((0, 0),))
    g_caq = jnp.tile(g_ca_q * (CA_HEAD_DIM ** -0.5 * LOG2_E), (1, CA_HEADS))
    g_cak = jnp.tile(g_ca_k, (1, CA_HEADS))
    head = jnp.arange(BD_WIDTH) // CA_HEAD_DIM
    bd = (head[:, None] == head[None, :]).astype(BF16)

    ext = jnp.pad(rel_bias, ((0, 0), (0, 0), (CA_TQ - 1 - REL_CLIP, CA_WIN - REL_CLIP)),
                  mode="edge")
    base = ext[:, :, ::-1].reshape(depth * CA_HEADS, 1, BASE_W)

    def chunks(w):
        g = w[..., :D_FF].reshape(w.shape[:-1] + (N_FF_CHUNKS, FF_CHUNK))
        v = w[..., D_FF:].reshape(w.shape[:-1] + (N_FF_CHUNKS, FF_CHUNK))
        return jnp.concatenate([g, v], axis=-1)

    w_up_c = jnp.moveaxis(chunks(w_up), 2, 1)
    conv_c = jnp.moveaxis(chunks(conv_w), 2, 1)
    conv_c = jnp.pad(conv_c, ((0, 0), (0, 0), (0, CARRY_ROWS - CONV_W), (0, 0)))
    cb_c = chunks(conv_b)[:, :, None, :]
    return {
        "g_mix": row(g_mix), "w1": w1.astype(BF16), "g_qa": row(g_q_lora),
        "g_kva": row(g_kv_lora), "w_uq": wq2.astype(BF16), "w_uk": wk.astype(BF16),
        "w_uv": wv.astype(BF16), "g_q": row(g_q), "g_qs": row(g_qs), "g_k": row(g_k),
        "g_ks": row(g_ks), "g_caq": row(g_caq), "g_cak": row(g_cak), "bd": bd, "base": base,
        "g_oa": row(g_out_mla), "g_ob": row(g_out_ca), "w_out": w_out.astype(BF16),
        "g_ffn": row(g_ffn), "w_up": w_up_c.astype(BF16), "conv_w": conv_c, "conv_b": cb_c,
        "w_down": w_down.reshape(depth, N_FF_CHUNKS, FF_CHUNK, D_MODEL).astype(BF16),
    }


def kernel(x, positions, g_mix, w_in, w_uq, w_ukv, g_q_lora, g_kv_lora, g_mla_q, g_mla_k,
           g_ca_q, g_ca_k, rel_bias, g_out_mla, g_out_ca, w_out, g_ffn, w_up, conv_w, conv_b,
           w_down):
    b, s, _ = x.shape
    depth = w_in.shape[0]
    assert s % TM_IN == 0 and s % TM_FFN == 0 and KV_PAD % TM_IN == 0
    assert (KV_PAD - CA_PAD) % BLK == 0
    wp = _prepare(g_mix, w_in, w_uq, w_ukv, g_q_lora, g_kv_lora, g_mla_q, g_mla_k, g_ca_q,
                  g_ca_k, rel_bias, g_out_mla, g_out_ca, w_out, g_ffn, w_up, conv_w, conv_b,
                  w_down)

    cos, sin = _rope_tables(positions)
    bias = _bias_tiles(wp["base"], depth)

    kc = jnp.zeros((b, s + KV_PAD, CA_WIDTH), BF16)
    vct = jnp.zeros((b, CA_HEADS, (s + KV_PAD) // BLK, CA_HEAD_DIM, BLK), BF16)
    for l in range(depth):
        qt, km, vt, qct, kc, vct = _inproj(l, x, cos, sin, wp, kc, vct)
        oa = _mla_attention(qt, km, vt)
        ob = _chunk_attention(l, qct, kc, vct, bias)
        x = _outproj_ffn(l, x, oa, ob, wp)
    return x
```

```python
import functools

import jax
import jax.numpy as jnp
from jax import lax
from jax.experimental import pallas as pl
from jax.experimental.pallas import tpu as pltpu

F32 = jnp.float32
BF16 = jnp.bfloat16

D_MODEL = 1024
CHUNK = 64
EPS = 1e-6
NEG_INF = -1e30
LOG2_E = 1.4426950408889634

MLA_HEADS = 8
MLA_NOPE = 64
MLA_ROPE = 32
MLA_QK = MLA_NOPE + MLA_ROPE
MLA_V = 64
Q_LORA = 256
KV_LORA = 128
ROPE_BASE = 10000.0
MLA_WIDTH = MLA_HEADS * MLA_V

CA_HEADS = 8
CA_HEAD_DIM = 64
CA_WIDTH = CA_HEADS * CA_HEAD_DIM
LEFT_CHUNKS = 8
REL_CLIP = 128

OFF_CQ = 0
OFF_CKV = OFF_CQ + Q_LORA
OFF_KR = OFF_CKV + KV_LORA
OFF_CA = OFF_KR + MLA_ROPE

D_FF = 2816
CONV_W = 3

LANES = 128
HEAD_SLOT = LANES
HALF_ROPE = MLA_ROPE // 2
ROPE_PACK = LANES // MLA_ROPE
BD_WIDTH = 256
V_DIM = MLA_V
assert V_DIM == CA_HEAD_DIM and MLA_HEADS == CA_HEADS and 2 * V_DIM == LANES
MLA_SLOTS = MLA_HEADS * HEAD_SLOT

P_CQ = 0
P_CKV = P_CQ + Q_LORA
P_KR = P_CKV + KV_LORA
P_KRS = P_KR + HEAD_SLOT
P_CAQ = P_KRS + HEAD_SLOT
P_CAK = P_CAQ + CA_WIDTH
P_CAV = P_CAK + CA_WIDTH
P_COLS = P_CAV + CA_WIDTH

SUM_ROWS = 16
BLK = 256
TM_IN = 1024
CA_PAD = LEFT_CHUNKS * CHUNK
KV_PAD = max(CA_PAD, TM_IN)
CA_TQ = BLK
CA_WIN = CA_PAD + CA_TQ
BASE_W = CA_WIN + CA_TQ
CA_VARIANTS = CA_PAD // CA_TQ + 1
CA_LOOKAHEAD = 3
CA_PV_DELAY = 3
CA_QBLOCKS = 4
MLA_TQ = BLK
MLA_HPS = 8
MLA_LOOKAHEAD = 4
MLA_PV_DELAY = 1
TM_FFN = 512
FF_CHUNK = 256
N_FF_CHUNKS = D_FF // FF_CHUNK
CARRY_ROWS = 8
FFN_SLOTS = 4

V7X_VMEM_BYTES = 64 * 1024 * 1024
VMEM_LIMIT = V7X_VMEM_BYTES * 7 // 8


def _rms(x, g):
    return x * lax.rsqrt(jnp.mean(x * x, axis=-1, keepdims=True) + EPS) * g


def _dot(a, b):
    return jnp.dot(a, b, preferred_element_type=F32)


def _rope_kernel(pos_ref, inv_ref, c_ref, s_ref):
    lane = lax.broadcasted_iota(jnp.int32, (1, LANES), 1)
    pos = pos_ref[:, 0:1]
    for m in range(1, ROPE_PACK):
        pos = jnp.where(lane // MLA_ROPE == m, pos_ref[:, m:m + 1], pos)
    ang = pos * inv_ref[...]
    c = jnp.cos(ang)
    s = jnp.sin(ang)
    s = jnp.where(lane % MLA_ROPE < HALF_ROPE, -s, s)
    rotary = (lane >= MLA_NOPE) & (lane < MLA_QK)
    for m in range(ROPE_PACK):
        shift = (MLA_NOPE - m * MLA_ROPE) % LANES
        c_ref[m] = jnp.where(rotary, pltpu.roll(c, shift, 1) if shift else c, 1.0)
        s_ref[m] = jnp.where(rotary, pltpu.roll(s, shift, 1) if shift else s, 0.0)


def _rope_tables(positions):
    t = positions.size
    quarter = t // ROPE_PACK
    rows = 256
    assert t % (ROPE_PACK * rows) == 0
    pos = positions.astype(F32).reshape(ROPE_PACK, quarter).T
    inv = ROPE_BASE ** (-jnp.arange(0, MLA_ROPE, 2, dtype=F32) / MLA_ROPE)
    inv = jnp.tile(jnp.concatenate([inv, inv]), ROPE_PACK)[None, :]
    out = jax.ShapeDtypeStruct((ROPE_PACK, quarter, LANES), F32)
    cos, sin = pl.pallas_call(
        _rope_kernel,
        grid=(quarter // rows,),
        in_specs=[pl.BlockSpec((rows, ROPE_PACK), lambda i: (i, 0)),
                  pl.BlockSpec((1, LANES), lambda i: (0, 0))],
        out_specs=[pl.BlockSpec((ROPE_PACK, rows, LANES), lambda i: (0, i, 0))] * 2,
        out_shape=[out] * 2,
        compiler_params=pltpu.CompilerParams(dimension_semantics=("arbitrary",)),
    )(pos, inv)
    shape = positions.shape + (LANES,)
    return cos.reshape(shape), sin.reshape(shape)


def _bias_kernel(base_ref, out_ref):
    x = pltpu.roll(jnp.broadcast_to(base_ref[...], (CA_TQ, BASE_W)), 0, 1,
                   stride=1, stride_axis=0)
    t = x[:, CA_TQ:] * LOG2_E
    i = lax.broadcasted_iota(jnp.int32, (CA_TQ, CA_WIN), 0) // CHUNK
    j = lax.broadcasted_iota(jnp.int32, (CA_TQ, CA_WIN), 1) // CHUNK
    band = (j >= i) & (j <= i + LEFT_CHUNKS)
    tile = jnp.where(band, t, NEG_INF).T
    key = lax.broadcasted_iota(jnp.int32, (CA_WIN, CA_TQ), 0)
    for v in range(CA_VARIANTS):
        out_ref[v] = jnp.where(key >= CA_PAD - v * CA_TQ, tile, NEG_INF)


def _bias_tiles(base, depth):
    return pl.pallas_call(
        _bias_kernel,
        grid=(depth * CA_HEADS,),
        in_specs=[pl.BlockSpec((None, 1, BASE_W), lambda i: (i, 0, 0))],
        out_specs=pl.BlockSpec((None, CA_VARIANTS, None, CA_WIN, CA_TQ),
                               lambda i: (i // CA_HEADS, 0, i % CA_HEADS, 0, 0)),
        out_shape=jax.ShapeDtypeStruct((depth, CA_VARIANTS, CA_HEADS, CA_WIN, CA_TQ), F32),
        compiler_params=pltpu.CompilerParams(dimension_semantics=("arbitrary",)),
    )(base)


def _inproj_kernel(x_ref, c_ref, s_ref, gmix_ref, w1_ref, gqa_ref, gkva_ref, wuq_ref, wuk_ref,
                   wuv_ref, gq_ref, gqs_ref, gk_ref, gks_ref, gcaq_ref, gcak_ref, bd_ref,
                   kpad_ref, vpad_ref,
                   qt_ref, km_ref, vt_ref, qct_ref, kc_ref, vct_ref):
    del kpad_ref, vpad_ref
    n_sub = TM_IN // BLK

    def project(r):
        rows = slice(r * BLK, (r + 1) * BLK)
        h = _rms(x_ref[rows, :], gmix_ref[...]).astype(BF16)
        return _dot(h, w1_ref[...])

    def head_sums(t):
        sq = (t * t).astype(BF16)
        return jnp.concatenate([_dot(sq[:, c:c + BD_WIDTH], bd_ref[...])
                                for c in range(0, CA_WIDTH, BD_WIDTH)], axis=1)

    def up_project(proj):
        cqn = _rms(proj[:, P_CQ:P_CKV], gqa_ref[...]).astype(BF16)
        ckvn = _rms(proj[:, P_CKV:P_KR], gkva_ref[...]).astype(BF16)
        caq = proj[:, P_CAQ:P_CAK]
        cak = proj[:, P_CAK:P_CAV]
        return {
            "q2": _dot(cqn, wuq_ref[...]),
            "kn": _dot(ckvn, wuk_ref[...]),
            "v": _dot(ckvn, wuv_ref[...]),
            "ssq": head_sums(caq),
            "ssk": head_sums(cak),
        }

    def transpose_heads(t, ref, r):
        for p in range(t.shape[1] // LANES):
            tt = t[:, p * LANES:(p + 1) * LANES].T.astype(BF16)
            ref[2 * p, r] = tt[:V_DIM]
            ref[2 * p + 1, r] = tt[V_DIM:]

    def finish(r, proj, up):
        rows = slice(r * BLK, (r + 1) * BLK)
        cos = c_ref[rows, :]
        sin = s_ref[rows, :]
        gq_c, gq_s = gq_ref[...] * cos, gqs_ref[...] * sin
        gk_c, gk_s = gk_ref[...] * cos, gks_ref[...] * sin
        kr = proj[:, P_KR:P_KRS]
        kr_s = proj[:, P_KRS:P_CAQ] * gk_s
        for hd in range(MLA_HEADS):
            sl = slice(hd * HEAD_SLOT, (hd + 1) * HEAD_SLOT)
            ssl = slice(MLA_SLOTS + hd * HEAD_SLOT, MLA_SLOTS + (hd + 1) * HEAD_SLOT)
            q = up["q2"][:, sl]
            rq = lax.rsqrt(jnp.sum(q * q, axis=-1, keepdims=True) * (1.0 / MLA_QK) + EPS)
            qr = (q * gq_c + up["q2"][:, ssl] * gq_s) * rq
            qt_ref[hd, :, rows] = qr.T.astype(BF16)
            k = up["kn"][:, sl] + kr
            rk = lax.rsqrt(jnp.sum(k * k, axis=-1, keepdims=True) * (1.0 / MLA_QK) + EPS)
            km_ref[rows, sl] = ((k * gk_c + kr_s) * rk).astype(BF16)
        transpose_heads(up["v"], vt_ref, r)
        caq = proj[:, P_CAQ:P_CAK]
        cak = proj[:, P_CAK:P_CAV]
        qcn = caq * lax.rsqrt(up["ssq"] * (1.0 / CA_HEAD_DIM) + EPS) * gcaq_ref[...]
        for p in range(CA_WIDTH // LANES):
            qct_ref[p, :, rows] = qcn[:, p * LANES:(p + 1) * LANES].T.astype(BF16)
        kc_ref[rows, :] = (cak * lax.rsqrt(up["ssk"] * (1.0 / CA_HEAD_DIM) + EPS)
                           * gcak_ref[...]).astype(BF16)
        transpose_heads(proj[:, P_CAV:P_COLS], vct_ref, r)

    proj = project(0)
    for r in range(n_sub):
        nxt = project(r + 1) if r + 1 < n_sub else None
        finish(r, proj, up_project(proj))
        proj = nxt


def _inproj(l, x, cos, sin, wp, kpad, vpad):
    b, s, _ = x.shape
    n_sub = TM_IN // BLK
    pad_tiles = KV_PAD // TM_IN

    def tok(width):
        return pl.BlockSpec((None, TM_IN, width), lambda bi, i: (bi, i, 0))

    def lw(shape):
        return pl.BlockSpec((None,) + shape, lambda bi, i: (l,) + (0,) * len(shape),
                            pipeline_mode=pl.Buffered(1))

    const = pl.BlockSpec((BD_WIDTH, BD_WIDTH), lambda bi, i: (0, 0),
                         pipeline_mode=pl.Buffered(1))
    any_spec = pl.BlockSpec(memory_space=pl.ANY)
    heads_t = lambda off: pl.BlockSpec((None, MLA_HEADS, n_sub, V_DIM, BLK),
                                       lambda bi, i: (bi, 0, i + off, 0, 0))
    tok_shape = lambda w: jax.ShapeDtypeStruct((b, s, w), BF16)
    return pl.pallas_call(
        _inproj_kernel,
        grid=(b, s // TM_IN),
        in_specs=[tok(D_MODEL), tok(LANES), tok(LANES),
                  lw((1, D_MODEL)), lw((D_MODEL, P_COLS)), lw((1, Q_LORA)), lw((1, KV_LORA)),
                  lw((Q_LORA, 2 * MLA_SLOTS)), lw((KV_LORA, MLA_SLOTS)),
                  lw((KV_LORA, MLA_WIDTH)), lw((1, HEAD_SLOT)), lw((1, HEAD_SLOT)),
                  lw((1, HEAD_SLOT)), lw((1, HEAD_SLOT)),
                  lw((1, CA_WIDTH)), lw((1, CA_WIDTH)), const, any_spec, any_spec],
        out_specs=[pl.BlockSpec((None, MLA_HEADS, HEAD_SLOT, TM_IN), lambda bi, i: (bi, 0, 0, i)),
                   tok(MLA_SLOTS), heads_t(0),
                   pl.BlockSpec((None, CA_WIDTH // LANES, LANES, TM_IN),
                                lambda bi, i: (bi, 0, 0, i)),
                   pl.BlockSpec((None, TM_IN, CA_WIDTH), lambda bi, i: (bi, i + pad_tiles, 0)),
                   heads_t(pad_tiles)],
        out_shape=[jax.ShapeDtypeStruct((b, MLA_HEADS, HEAD_SLOT, s), BF16), tok_shape(MLA_SLOTS),
                   jax.ShapeDtypeStruct((b, MLA_HEADS, s // BLK, MLA_V, BLK), BF16),
                   jax.ShapeDtypeStruct((b, CA_WIDTH // LANES, LANES, s), BF16),
                   jax.ShapeDtypeStruct(kpad.shape, kpad.dtype),
                   jax.ShapeDtypeStruct(vpad.shape, vpad.dtype)],
        input_output_aliases={17: 4, 18: 5},
        compiler_params=pltpu.CompilerParams(
            dimension_semantics=("arbitrary", "arbitrary"), vmem_limit_bytes=VMEM_LIMIT),
    )(x, cos, sin, wp["g_mix"], wp["w1"], wp["g_qa"], wp["g_kva"], wp["w_uq"], wp["w_uk"],
      wp["w_uv"], wp["g_q"], wp["g_qs"], wp["g_k"], wp["g_ks"], wp["g_caq"], wp["g_cak"],
      wp["bd"], kpad, vpad)


def _mla_kernel(qt_ref, k_ref, vt_ref, o_ref, m_ref, acc_ref, s_ref):
    i = pl.program_id(2)
    t = MLA_TQ
    ones = jnp.ones((SUM_ROWS, t), BF16)
    m_ref[...] = jnp.full(m_ref.shape, NEG_INF, F32)
    acc_ref[...] = jnp.zeros(acc_ref.shape, F32)

    def scores(j, hh):
        rows = pl.ds(pl.multiple_of(j * t, t), t)
        sl = slice(hh * HEAD_SLOT, (hh + 1) * HEAD_SLOT)
        return _dot(k_ref[rows, sl], qt_ref[hh])

    def step(j, last):
        def accumulate(hh, p, alpha):
            v1 = jnp.concatenate([vt_ref[hh, j], ones], axis=0)
            acc_ref[hh] = acc_ref[hh] * alpha + _dot(v1, p)

        pending = {hh: s_ref[hh] for hh in range(MLA_LOOKAHEAD)}
        done = {}
        for hh in range(MLA_HPS):
            st = pending.pop(hh)
            ahead = hh + MLA_LOOKAHEAD
            if ahead < MLA_HPS:
                pending[ahead] = scores(j, ahead)
            elif not last:
                s_ref[ahead - MLA_HPS] = scores(j + 1, ahead - MLA_HPS)
            if last:
                kc = lax.broadcasted_iota(jnp.int32, st.shape, 0) // CHUNK
                qc = lax.broadcasted_iota(jnp.int32, st.shape, 1) // CHUNK
                st = jnp.where(kc <= qc, st, NEG_INF)
            m_prev = m_ref[hh]
            m_new = jnp.maximum(m_prev, jnp.max(st, axis=0, keepdims=True))
            m_ref[hh] = m_new
            done[hh] = (jnp.exp2(st - m_new).astype(BF16), jnp.exp2(m_prev - m_new))
            if hh >= MLA_PV_DELAY:
                accumulate(hh - MLA_PV_DELAY, *done.pop(hh - MLA_PV_DELAY))
        for hh in sorted(done):
            accumulate(hh, *done[hh])

    for hh in range(MLA_LOOKAHEAD):
        s_ref[hh] = scores(0, hh)

    def body(k, carry):
        for n in range(3):
            step(3 * k + n, False)
        return carry

    lax.fori_loop(0, i // 3, body, 0)

    done = (i // 3) * 3

    @pl.when(i - done == 2)
    def _():
        step(done, False)
        step(done + 1, False)

    @pl.when(i - done == 1)
    def _():
        step(done, False)

    step(i, True)
    for pr in range(MLA_HPS // 2):
        ot = jnp.concatenate([acc_ref[hh, :MLA_V, :] / acc_ref[hh, MLA_V:MLA_V + 1, :]
                              for hh in (2 * pr, 2 * pr + 1)], axis=0)
        o_ref[:, pr * LANES:(pr + 1) * LANES] = ot.T.astype(BF16)


def _mla_attention(qt, km, vt):
    b, s, _ = km.shape
    groups = MLA_HEADS // MLA_HPS
    nk = s // MLA_TQ
    return pl.pallas_call(
        _mla_kernel,
        grid=(b, groups, s // MLA_TQ),
        in_specs=[pl.BlockSpec((None, MLA_HPS, HEAD_SLOT, MLA_TQ), lambda bi, g, i: (bi, g, 0, i)),
                  pl.BlockSpec((None, s, MLA_HPS * HEAD_SLOT), lambda bi, g, i: (bi, 0, g)),
                  pl.BlockSpec((None, MLA_HPS, nk, MLA_V, MLA_TQ),
                               lambda bi, g, i: (bi, g, 0, 0, 0))],
        out_specs=pl.BlockSpec((None, MLA_TQ, MLA_HPS * MLA_V), lambda bi, g, i: (bi, i, g)),
        out_shape=jax.ShapeDtypeStruct((b, s, MLA_WIDTH), BF16),
        scratch_shapes=[pltpu.VMEM((MLA_HPS, 1, MLA_TQ), F32),
                        pltpu.VMEM((MLA_HPS, MLA_V + SUM_ROWS, MLA_TQ), F32),
                        pltpu.VMEM((MLA_LOOKAHEAD, MLA_TQ, MLA_TQ), F32)],
        compiler_params=pltpu.CompilerParams(
            dimension_semantics=("arbitrary", "arbitrary", "arbitrary"),
            vmem_limit_bytes=VMEM_LIMIT),
    )(qt, km, vt)


def _ca_kernel(qt_ref, k_ref, vt_ref, bias_ref, o_ref):
    lead = KV_PAD - CA_PAD
    first = lax.broadcasted_iota(jnp.int32, (LANES, 1), 0) < CA_HEAD_DIM
    ones = jnp.ones((SUM_ROWS, BLK), BF16)
    items = [(sub, h) for sub in range(CA_QBLOCKS) for h in range(CA_HEADS)]
    pieces = range(CA_WIN // BLK)

    def block(sub):
        return pl.program_id(1) * CA_QBLOCKS + sub

    def scores(sub, h):
        sl = slice((h // 2) * LANES, (h // 2 + 1) * LANES)
        qt = qt_ref[h // 2, :, sub * CA_TQ:(sub + 1) * CA_TQ]
        qh = jnp.where(first if h % 2 == 0 else ~first, qt, jnp.zeros_like(qt))
        row0 = lead + block(sub) * CA_TQ
        return [_dot(k_ref[pl.ds(pl.multiple_of(row0 + kb * BLK, BLK), BLK), sl], qh)
                for kb in pieces]

    def attend(sub, h, probs):
        blk0 = lead // BLK + block(sub)
        acc = None
        for kb in pieces:
            v1 = jnp.concatenate([vt_ref[h, blk0 + kb], ones], axis=0)
            part = _dot(v1, probs[kb])
            acc = part if acc is None else acc + part
        return acc[:CA_HEAD_DIM] / acc[CA_HEAD_DIM:CA_HEAD_DIM + 1]

    pending = {it: scores(*it) for it in items[:CA_LOOKAHEAD]}
    probs = {}
    outs = {}

    def flush(sub, h):
        outs[h] = attend(sub, h, probs.pop((sub, h)))
        if h % 2 == 1:
            ot = jnp.concatenate([outs.pop(h - 1), outs.pop(h)], axis=0)
            o_ref[sub * CA_TQ:(sub + 1) * CA_TQ, (h // 2) * LANES:(h // 2 + 1) * LANES] = (
                ot.T.astype(BF16))

    for n, (sub, h) in enumerate(items):
        st = pending.pop((sub, h))
        if n + CA_LOOKAHEAD < len(items):
            ahead = items[n + CA_LOOKAHEAD]
            pending[ahead] = scores(*ahead)
        variant = jnp.minimum(block(sub), CA_VARIANTS - 1)
        st = [st[kb] + bias_ref[variant, h, kb * BLK:(kb + 1) * BLK, :] for kb in pieces]
        m = functools.reduce(jnp.maximum, [jnp.max(t, axis=0, keepdims=True) for t in st])
        probs[(sub, h)] = [jnp.exp2(t - m).astype(BF16) for t in st]
        if n >= CA_PV_DELAY:
            flush(*items[n - CA_PV_DELAY])
    for it in items[len(items) - CA_PV_DELAY:]:
        flush(*it)


def _chunk_attention(l, qct, kc, vct, bias):
    b, _, _, s = qct.shape
    nblk = vct.shape[2]
    tq = CA_QBLOCKS * CA_TQ
    return pl.pallas_call(
        _ca_kernel,
        grid=(b, s // tq),
        in_specs=[pl.BlockSpec((None, CA_WIDTH // LANES, LANES, tq), lambda bi, i: (bi, 0, 0, i)),
                  pl.BlockSpec((None, s + KV_PAD, CA_WIDTH), lambda bi, i: (bi, 0, 0)),
                  pl.BlockSpec((None, CA_HEADS, nblk, CA_HEAD_DIM, BLK),
                               lambda bi, i: (bi, 0, 0, 0, 0)),
                  pl.BlockSpec((None, CA_VARIANTS, CA_HEADS, CA_WIN, CA_TQ),
                               lambda bi, i: (l, 0, 0, 0, 0), pipeline_mode=pl.Buffered(1))],
        out_specs=pl.BlockSpec((None, tq, CA_WIDTH), lambda bi, i: (bi, i, 0)),
        out_shape=jax.ShapeDtypeStruct((b, s, CA_WIDTH), BF16),
        compiler_params=pltpu.CompilerParams(
            dimension_semantics=("arbitrary", "arbitrary"), vmem_limit_bytes=VMEM_LIMIT),
    )(qct, kc, vct, bias)


def _ffn_kernel(x_ref, oa_ref, ob_ref, goa_ref, gob_ref, wout_ref, gffn_ref, wup_ref, cw_ref,
                cb_ref, wdown_ref, out_ref, carry_ref, u_ref, h_ref):
    tm = TM_FFN

    @pl.when(pl.program_id(1) == 0)
    def _():
        carry_ref[...] = jnp.zeros(carry_ref.shape, F32)

    na = _rms(oa_ref[...].astype(F32), goa_ref[...]).astype(BF16)
    nb = _rms(ob_ref[...].astype(F32), gob_ref[...]).astype(BF16)
    x1 = (x_ref[...] + _dot(na, wout_ref[0:MLA_WIDTH, :])
          + _dot(nb, wout_ref[MLA_WIDTH:MLA_WIDTH + CA_WIDTH, :]))
    out_ref[...] = x1
    h_ref[...] = _rms(x1, gffn_ref[...]).astype(BF16)

    def up(c, slot):
        u = _dot(h_ref[...], wup_ref[c])
        u_ref[slot, 0:CARRY_ROWS, :] = carry_ref[c]
        u_ref[slot, CARRY_ROWS:CARRY_ROWS + tm, :] = u
        carry_ref[c] = u[tm - CARRY_ROWS:, :]

    def act(c, slot):
        cw = cw_ref[c]
        y = cb_ref[c]
        for tap in range(CONV_W):
            off = CARRY_ROWS - (CONV_W - 1) + tap
            y = y + u_ref[slot, pl.ds(off, tm), :] * cw[tap:tap + 1, :]
        g = y[:, :FF_CHUNK]
        return (g * (1.0 / (1.0 + jnp.exp(-g))) * y[:, FF_CHUNK:]).astype(BF16)

    def down(c, a):
        out_ref[...] += _dot(a, wdown_ref[c])

    up(0, 0)

    def pair(c, d):
        up(c + 1, (d + 1) % FFN_SLOTS)
        a0 = act(c, d)
        up(c + 2, (d + 2) % FFN_SLOTS)
        down(c, a0)
        a1 = act(c + 1, (d + 1) % FFN_SLOTS)
        down(c + 1, a1)

    def body(k, carry):
        c = FFN_SLOTS * k
        for d in range(0, FFN_SLOTS, 2):
            pair(c + d, d)
        return carry

    n_pairs = (N_FF_CHUNKS - 1) // 2
    n_loop = n_pairs // (FFN_SLOTS // 2)
    lax.fori_loop(0, n_loop, body, 0)
    for c in range(n_loop * FFN_SLOTS, N_FF_CHUNKS - 1, 2):
        pair(c, c % FFN_SLOTS)
    last = N_FF_CHUNKS - 1
    down(last, act(last, last % FFN_SLOTS))


def _outproj_ffn(l, x, oa, ob, wp):
    b, s, _ = x.shape

    def tok(width):
        return pl.BlockSpec((None, TM_FFN, width), lambda bi, i: (bi, i, 0))

    def lw(shape):
        return pl.BlockSpec((None,) + shape, lambda bi, i: (l,) + (0,) * len(shape),
                            pipeline_mode=pl.Buffered(1))

    return pl.pallas_call(
        _ffn_kernel,
        grid=(b, s // TM_FFN),
        in_specs=[tok(D_MODEL), tok(MLA_WIDTH), tok(CA_WIDTH),
                  lw((1, MLA_WIDTH)), lw((1, CA_WIDTH)), lw((D_MODEL, D_MODEL)),
                  lw((1, D_MODEL)), lw((N_FF_CHUNKS, D_MODEL, 2 * FF_CHUNK)),
                  lw((N_FF_CHUNKS, CARRY_ROWS, 2 * FF_CHUNK)),
                  lw((N_FF_CHUNKS, 1, 2 * FF_CHUNK)),
                  lw((N_FF_CHUNKS, FF_CHUNK, D_MODEL))],
        out_specs=tok(D_MODEL),
        out_shape=jax.ShapeDtypeStruct((b, s, D_MODEL), F32),
        scratch_shapes=[pltpu.VMEM((N_FF_CHUNKS, CARRY_ROWS, 2 * FF_CHUNK), F32),
                        pltpu.VMEM((FFN_SLOTS, CARRY_ROWS + TM_FFN, 2 * FF_CHUNK), F32),
                        pltpu.VMEM((TM_FFN, D_MODEL), BF16)],
        compiler_params=pltpu.CompilerParams(
            dimension_semantics=("arbitrary", "arbitrary"), vmem_limit_bytes=VMEM_LIMIT),
    )(x, oa, ob, wp["g_oa"], wp["g_ob"], wp["w_out"], wp["g_ffn"], wp["w_up"], wp["conv_w"],
      wp["conv_b"], wp["w_down"])


def _rope_slot(first, second, axis_pad):
    t = jnp.concatenate([first, second], axis=-1)
    return jnp.pad(t, axis_pad + ((MLA_NOPE, HEAD_SLOT - MLA_QK),))


def _prepare(g_mix, w_in, w_uq, w_ukv, g_q_lora, g_kv_lora, g_mla_q, g_mla_k, g_ca_q, g_ca_k,
             rel_bias, g_out_mla, g_out_ca, w_out, g_ffn, w_up, conv_w, conv_b, w_down):
    depth = w_in.shape[0]
    row = lambda g: g[:, None, :]
    zpad = HEAD_SLOT - MLA_QK
    r1 = slice(MLA_NOPE, MLA_NOPE + HALF_ROPE)
    r2 = slice(MLA_NOPE + HALF_ROPE, MLA_QK)

    w_kr = w_in[:, :, OFF_KR:OFF_CA]
    kr_slot = _rope_slot(w_kr[..., :HALF_ROPE], w_kr[..., HALF_ROPE:], ((0, 0), (0, 0)))
    kr_swap = _rope_slot(w_kr[..., HALF_ROPE:], w_kr[..., :HALF_ROPE], ((0, 0), (0, 0)))
    w1 = jnp.concatenate([w_in[:, :, OFF_CQ:OFF_KR], kr_slot, kr_swap, w_in[:, :, OFF_CA:]],
                         axis=-1)

    wq = w_uq.reshape(depth, Q_LORA, MLA_HEADS, MLA_QK)
    wq_slot = jnp.pad(wq, ((0, 0), (0, 0), (0, 0), (0, zpad))).reshape(depth, Q_LORA, -1)
    wq_swap = _rope_slot(wq[..., r2], wq[..., r1], ((0, 0), (0, 0), (0, 0)))
    wq2 = jnp.concatenate([wq_slot, wq_swap.reshape(depth, Q_LORA, -1)], axis=-1)
    wkv = w_ukv.reshape(depth, KV_LORA, MLA_HEADS, MLA_NOPE + MLA_V)
    wk = jnp.pad(wkv[..., :MLA_NOPE], ((0, 0), (0, 0), (0, 0), (0, HEAD_SLOT - MLA_NOPE)))
    wk = wk.reshape(depth, KV_LORA, -1)
    wv = wkv[..., MLA_NOPE:].reshape(depth, KV_LORA, MLA_WIDTH)

    gq = g_mla_q * (MLA_QK ** -0.5 * LOG2_E)
    g_q = jnp.pad(gq, ((0, 0), (0, zpad)))
    g_qs = _rope_slot(gq[:, r2], gq[:, r1], ((0, 0),))
    g_k = jnp.pad(g_mla_k, ((0, 0), (0, zpad)))
    g_ks = _rope_slot(g_mla_k[:, r2], g_mla_k[:, r1], ((0, 0),))
    g_caq = jnp.tile(g_ca_q * (CA_HEAD_DIM ** -0.5 * LOG2_E), (1, CA_HEADS))
    g_cak = jnp.tile(g_ca_k, (1, CA_HEADS))
    head = jnp.arange(BD_WIDTH) // CA_HEAD_DIM
    bd = (head[:, None] == head[None, :]).astype(BF16)

    ext = jnp.pad(rel_bias, ((0, 0), (0, 0), (CA_TQ - 1 - REL_CLIP, CA_WIN - REL_CLIP)),
                  mode="edge")
    base = ext[:, :, ::-1].reshape(depth * CA_HEADS, 1, BASE_W)

    def chunks(w):
        g = w[..., :D_FF].reshape(w.shape[:-1] + (N_FF_CHUNKS, FF_CHUNK))
        v = w[..., D_FF:].reshape(w.shape[:-1] + (N_FF_CHUNKS, FF_CHUNK))
        return jnp.concatenate([g, v], axis=-1)

    w_up_c = jnp.moveaxis(chunks(w_up), 2, 1)
    conv_c = jnp.moveaxis(chunks(conv_w), 2, 1)
    conv_c = jnp.pad(conv_c, ((0, 0), (0, 0), (0, CARRY_ROWS - CONV_W), (0, 0)))
    cb_c = chunks(conv_b)[:, :, None, :]
    return {
        "g_mix": row(g_mix), "w1": w1.astype(BF16), "g_qa": row(g_q_lora),
        "g_kva": row(g_kv_lora), "w_uq": wq2.astype(BF16), "w_uk": wk.astype(BF16),
        "w_uv": wv.astype(BF16), "g_q": row(g_q), "g_qs": row(g_qs), "g_k": row(g_k),
        "g_ks": row(g_ks), "g_caq": row(g_caq), "g_cak": row(g_cak), "bd": bd, "base": base,
        "g_oa": row(g_out_mla), "g_ob": row(g_out_ca), "w_out": w_out.astype(BF16),
        "g_ffn": row(g_ffn), "w_up": w_up_c.astype(BF16), "conv_w": conv_c, "conv_b": cb_c,
        "w_down": w_down.reshape(depth, N_FF_CHUNKS, FF_CHUNK, D_MODEL).astype(BF16),
    }


def kernel(x, positions, g_mix, w_in, w_uq, w_ukv, g_q_lora, g_kv_lora, g_mla_q, g_mla_k,
           g_ca_q, g_ca_k, rel_bias, g_out_mla, g_out_ca, w_out, g_ffn, w_up, conv_w, conv_b,
           w_down):
    b, s, _ = x.shape
    depth = w_in.shape[0]
    assert s % TM_IN == 0 and s % TM_FFN == 0 and KV_PAD % TM_IN == 0
    assert (KV_PAD - CA_PAD) % BLK == 0
    wp = _prepare(g_mix, w_in, w_uq, w_ukv, g_q_lora, g_kv_lora, g_mla_q, g_mla_k, g_ca_q,
                  g_ca_k, rel_bias, g_out_mla, g_out_ca, w_out, g_ffn, w_up, conv_w, conv_b,
                  w_down)

    cos, sin = _rope_tables(positions)
    bias = _bias_tiles(wp["base"], depth)

    kc = jnp.zeros((b, s + KV_PAD, CA_WIDTH), BF16)
    vct = jnp.zeros((b, CA_HEADS, (s + KV_PAD) // BLK, CA_HEAD_DIM, BLK), BF16)
    for l in range(depth):
        qt, km, vt, qct, kc, vct = _inproj(l, x, cos, sin, wp, kc, vct)
        oa = _mla_attention(qt, km, vt)
        ob = _chunk_attention(l, qct, kc, vct, bias)
        x = _outproj_ffn(l, x, oa, ob, wp)
    return x
```
